```python
import math
import jax, jax.numpy as jnp
from jax import lax
import numpy as np

D_MODEL = 1024
BATCH = 16
SEQ = 2048
DEPTH = 1

N_ATTN_HEADS = 8
N_KV_HEADS = 2
HEAD_DIM = 64
ATTN_WIDTH = N_ATTN_HEADS * HEAD_DIM
N_IDX_HEADS = 4
IDX_DIM = 64
TOPK_MAX = 256
Q_BLOCK = 128
NUM_BUCKETS = 32
MAX_DISTANCE = 128
SSD_D_INNER = D_MODEL // 2
SSD_HEAD_DIM = 64
SSD_N_HEADS = SSD_D_INNER // SSD_HEAD_DIM
SSD_N_GROUPS = 2
SSD_D_STATE = 128
CONV_WIDTH = 4
CHUNK = 128
CONV_DIM = SSD_D_INNER + 2 * SSD_N_GROUPS * SSD_D_STATE
D_FF = 4 * D_MODEL
MIX_WIDTH = ATTN_WIDTH + SSD_D_INNER
EPS = 1e-6

IN_SPLIT_SIZES = (ATTN_WIDTH, N_KV_HEADS * HEAD_DIM, N_KV_HEADS * HEAD_DIM,
                  N_IDX_HEADS * IDX_DIM, IDX_DIM, N_IDX_HEADS,
                  SSD_D_INNER, CONV_DIM, SSD_N_HEADS)
IN_PROJ_DIM = (ATTN_WIDTH + 4 * N_KV_HEADS * HEAD_DIM // 2 + N_IDX_HEADS * IDX_DIM
               + IDX_DIM + N_IDX_HEADS + SSD_D_INNER + CONV_DIM + SSD_N_HEADS)

kernel_name = "hymba_dsa_ssd_hybrid_layer"


def rms_norm(x, w):
    xf = x.astype(jnp.float32)
    y = xf * lax.rsqrt(jnp.mean(xf * xf, axis=-1, keepdims=True) + EPS)
    return (y * w.astype(jnp.float32)).astype(x.dtype)


def layer_norm(x, w, b):
    xf = x.astype(jnp.float32)
    mu = jnp.mean(xf, axis=-1, keepdims=True)
    var = jnp.mean(jnp.square(xf - mu), axis=-1, keepdims=True)
    y = (xf - mu) * lax.rsqrt(var + EPS)
    return (y * w.astype(jnp.float32) + b.astype(jnp.float32)).astype(x.dtype)


def t5_bucket(dist):
    max_exact = NUM_BUCKETS // 2
    is_small = dist < max_exact
    df = jnp.maximum(dist, 1).astype(jnp.float32)
    large = max_exact + (jnp.log(df / max_exact) / math.log(MAX_DISTANCE / max_exact)
                         * (NUM_BUCKETS - max_exact)).astype(jnp.int32)
    large = jnp.minimum(large, NUM_BUCKETS - 1)
    return jnp.where(is_small, dist, large)


def sparse_attention(q, k, v, q_idx, k_idx, w_idx, rel_bias):
    Bn, L = q.shape[0], q.shape[1]
    n_sel = min(TOPK_MAX, L // 4)
    n_blocks = L // Q_BLOCK
    rep = N_ATTN_HEADS // N_KV_HEADS
    key_pos = jnp.arange(L, dtype=jnp.int32)
    k_idx_f = k_idx.astype(jnp.float32)

    def block(i):
        t0 = i * Q_BLOCK
        qb = lax.dynamic_slice_in_dim(q, t0, Q_BLOCK, axis=1)
        qib = lax.dynamic_slice_in_dim(q_idx, t0, Q_BLOCK, axis=1)
        wb = lax.dynamic_slice_in_dim(w_idx, t0, Q_BLOCK, axis=1)
        q_pos = t0 + jnp.arange(Q_BLOCK, dtype=jnp.int32)
        dots = jnp.einsum("bthd,bsd->bths", qib.astype(jnp.float32), k_idx_f) * (IDX_DIM ** -0.5)
        score = jnp.einsum("bth,bths->bts", wb.astype(jnp.float32), jax.nn.relu(dots))
        causal = key_pos[None, :] <= q_pos[:, None]
        score = jnp.where(causal[None], score, -jnp.inf)
        _, idx = lax.top_k(score, n_sel)
        k_sel = jax.vmap(lambda kb, ib: kb[ib])(k, idx)
        v_sel = jax.vmap(lambda vb, ib: vb[ib])(v, idx)
        qg = qb.reshape(Bn, Q_BLOCK, N_KV_HEADS, rep, HEAD_DIM)
        logits = jnp.einsum("btgrd,btkgd->btgrk", qg, k_sel).astype(jnp.float32) * (HEAD_DIM ** -0.5)
        dist = q_pos[None, :, None] - idx
        valid = dist >= 0
        bias = rel_bias[t5_bucket(jnp.maximum(dist, 0))]
        bias = bias.reshape(Bn, Q_BLOCK, n_sel, N_KV_HEADS, rep).transpose(0, 1, 3, 4, 2)
        logits = jnp.where(valid[:, :, None, None, :], logits + bias.astype(jnp.float32), -jnp.inf)
        p = jax.nn.softmax(logits, axis=-1).astype(v.dtype)
        o = jnp.einsum("btgrk,btkgd->btgrd", p, v_sel)
        return o.reshape(Bn, Q_BLOCK, ATTN_WIDTH)

    out = lax.map(block, jnp.arange(n_blocks, dtype=jnp.int32))
    return out.transpose(1, 0, 2, 3).reshape(Bn, L, ATTN_WIDTH)


def segsum(a):
    T = a.shape[-1]
    cs = jnp.cumsum(a, axis=-1)
    diff = cs[..., :, None] - cs[..., None, :]
    mask = jnp.tril(jnp.ones((T, T), dtype=bool))
    return jnp.where(mask, diff, -jnp.inf)


def ssd_mixer(z, xbc, dt, conv_w, conv_b, dt_bias, a_log, d_skip, norm_w):
    Bn, L, _ = xbc.shape
    out_dtype = z.dtype
    xbc = lax.conv_general_dilated(
        xbc, conv_w[:, None, :].astype(xbc.dtype), window_strides=(1,),
        padding=[(CONV_WIDTH - 1, 0)], dimension_numbers=("NWC", "WIO", "NWC"),
        feature_group_count=CONV_DIM) + conv_b
    xbc = jax.nn.silu(xbc).astype(jnp.float32)
    xs = xbc[..., :SSD_D_INNER].reshape(Bn, L, SSD_N_HEADS, SSD_HEAD_DIM)
    gn = SSD_N_GROUPS * SSD_D_STATE
    Bm = xbc[..., SSD_D_INNER:SSD_D_INNER + gn].reshape(Bn, L, SSD_N_GROUPS, SSD_D_STATE)
    Cm = xbc[..., SSD_D_INNER + gn:].reshape(Bn, L, SSD_N_GROUPS, SSD_D_STATE)
    heads_per_group = SSD_N_HEADS // SSD_N_GROUPS
    Bh = jnp.repeat(Bm, heads_per_group, axis=2)
    Ch = jnp.repeat(Cm, heads_per_group, axis=2)
    dt = jax.nn.softplus(dt.astype(jnp.float32) + dt_bias.astype(jnp.float32))
    A = -jnp.exp(a_log.astype(jnp.float32))
    nc = L // CHUNK
    X = (xs * dt[..., None]).reshape(Bn, nc, CHUNK, SSD_N_HEADS, SSD_HEAD_DIM)
    Adt = (dt * A).reshape(Bn, nc, CHUNK, SSD_N_HEADS).transpose(0, 3, 1, 2)
    Bc = Bh.reshape(Bn, nc, CHUNK, SSD_N_HEADS, SSD_D_STATE)
    Cc = Ch.reshape(Bn, nc, CHUNK, SSD_N_HEADS, SSD_D_STATE)
    A_cs = jnp.cumsum(Adt, axis=-1)
    Lmat = jnp.exp(segsum(Adt))
    cb = jnp.einsum("bclhn,bcshn->bhcls", Cc, Bc) * Lmat
    y_diag = jnp.einsum("bhcls,bcshp->bclhp", cb, X)
    decay_states = jnp.exp(A_cs[..., -1:] - A_cs)
    states = jnp.einsum("bclhn,bhcl,bclhp->bchpn", Bc, decay_states, X)
    states = jnp.concatenate([jnp.zeros_like(states[:, :1]), states], axis=1)
    chunk_decay = jnp.exp(segsum(jnp.pad(A_cs[..., -1], ((0, 0), (0, 0), (1, 0)))))
    states = jnp.einsum("bhzc,bchpn->bzhpn", chunk_decay, states)[:, :-1]
    y_off = jnp.einsum("bclhn,bchpn,bhcl->bclhp", Cc, states, jnp.exp(A_cs))
    y = (y_diag + y_off).reshape(Bn, L, SSD_N_HEADS, SSD_HEAD_DIM)
    y = y + xs * d_skip.astype(jnp.float32)[:, None]
    y = y.reshape(Bn, L, SSD_D_INNER) * jax.nn.silu(z.astype(jnp.float32))
    yg = y.reshape(Bn, L, SSD_N_GROUPS, SSD_D_INNER // SSD_N_GROUPS)
    yg = yg * lax.rsqrt(jnp.mean(yg * yg, axis=-1, keepdims=True) + EPS)
    y = yg.reshape(Bn, L, SSD_D_INNER) * norm_w.astype(jnp.float32)
    return y.astype(out_dtype)


def setup_inputs(seed: int = 0) -> dict:
    key = jax.random.key(seed)
    ks = jax.random.split(key, 20)
    f32 = jnp.float32

    def gain(k, n):
        return 1.0 + 0.05 * jax.random.normal(k, (DEPTH, n), f32)

    dt0 = jnp.exp(jax.random.uniform(ks[9], (DEPTH, SSD_N_HEADS), f32)
                  * (math.log(0.1) - math.log(0.001)) + math.log(0.001))
    return {
        "x": jax.random.normal(ks[0], (BATCH, SEQ, D_MODEL), f32),
        "norm_pre_mix": gain(ks[1], D_MODEL),
        "norm_post_mix": gain(ks[2], D_MODEL),
        "norm_pre_mlp": gain(ks[3], D_MODEL),
        "norm_post_mlp": gain(ks[4], D_MODEL),
        "w_in": jax.random.normal(ks[5], (DEPTH, D_MODEL, IN_PROJ_DIM), f32) * D_MODEL ** -0.5,
        "k_idx_ln_w": gain(ks[6], IDX_DIM),
        "k_idx_ln_b": 0.02 * jax.random.normal(ks[7], (DEPTH, IDX_DIM), f32),
        "conv_w": jax.random.normal(ks[8], (DEPTH, CONV_WIDTH, CONV_DIM), f32) * CONV_WIDTH ** -0.5,
        "conv_b": 0.02 * jax.random.normal(ks[10], (DEPTH, CONV_DIM), f32),
        "dt_bias": dt0 + jnp.log(-jnp.expm1(-dt0)),
        "a_log": jnp.log(jax.random.uniform(ks[11], (DEPTH, SSD_N_HEADS), f32, 1.0, 16.0)),
        "d_skip": 1.0 + 0.1 * jax.random.normal(ks[12], (DEPTH, SSD_N_HEADS), f32),
        "ssd_norm_w": gain(ks[13], SSD_D_INNER),
        "w_out": jax.random.normal(ks[14], (DEPTH, MIX_WIDTH, D_MODEL), f32) * MIX_WIDTH ** -0.5,
        "w_mlp_up": jax.random.normal(ks[15], (DEPTH, D_MODEL, D_FF), f32) * D_MODEL ** -0.5,
        "w_mlp_down": jax.random.normal(ks[16], (DEPTH, D_FF, D_MODEL), f32) * D_FF ** -0.5,
        "rel_bias": 0.5 * jax.random.normal(ks[17], (NUM_BUCKETS, N_ATTN_HEADS), f32),
    }


def reference(x, norm_pre_mix, norm_post_mix, norm_pre_mlp, norm_post_mlp, w_in,
              k_idx_ln_w, k_idx_ln_b, conv_w, conv_b, dt_bias, a_log, d_skip,
              ssd_norm_w, w_out, w_mlp_up, w_mlp_down, rel_bias):
    Bn, L, _ = x.shape
    offsets = [int(o) for o in np.cumsum(IN_SPLIT_SIZES)[:-1]]
    h = x
    for i in range(DEPTH):
        u = rms_norm(h, norm_pre_mix[i])
        proj = u @ w_in[i]
        q, k, v, qi, ki, wi, z, xbc, dt = jnp.split(proj, offsets, axis=-1)
        q = q.reshape(Bn, L, N_ATTN_HEADS, HEAD_DIM)
        k = k.reshape(Bn, L, N_KV_HEADS, HEAD_DIM)
        v = v.reshape(Bn, L, N_KV_HEADS, HEAD_DIM)
        qi = qi.reshape(Bn, L, N_IDX_HEADS, IDX_DIM)
        ki = layer_norm(ki, k_idx_ln_w[i], k_idx_ln_b[i])
        wi = wi * (N_IDX_HEADS ** -0.5)
        attn = sparse_attention(q, k, v, qi, ki, wi, rel_bias)
        ssd = ssd_mixer(z, xbc, dt, conv_w[i], conv_b[i], dt_bias[i], a_log[i],
                        d_skip[i], ssd_norm_w[i])
        mix = jnp.concatenate([attn, ssd], axis=-1) @ w_out[i]
        h = h + rms_norm(mix, norm_post_mix[i])
        f = rms_norm(h, norm_pre_mlp[i]) @ w_mlp_up[i]
        f = jnp.square(jax.nn.relu(f)) @ w_mlp_down[i]
        h = h + rms_norm(f, norm_post_mlp[i])
    return h
```

```python
import functools
import math

import jax
import jax.numpy as jnp
from jax import lax
from jax.experimental import pallas as pl
from jax.experimental.pallas import tpu as pltpu

F32 = jnp.float32
BF16 = jnp.bfloat16
I32 = jnp.int32

D_MODEL = 1024
N_ATTN_HEADS = 8
N_KV_HEADS = 2
HEAD_DIM = 64
ATTN_WIDTH = N_ATTN_HEADS * HEAD_DIM
N_IDX_HEADS = 4
IDX_DIM = 64
TOPK_MAX = 256
NUM_BUCKETS = 32
MAX_DISTANCE = 128
SSD_D_INNER = 512
SSD_HEAD_DIM = 64
SSD_N_HEADS = 8
SSD_N_GROUPS = 2
SSD_D_STATE = 128
CONV_WIDTH = 4
CONV_DIM = SSD_D_INNER + 2 * SSD_N_GROUPS * SSD_D_STATE
D_FF = 4 * D_MODEL
EPS = 1e-6

LANES = 128
SUBLANES = 8
TILE = 128
VMEM_LIMIT = 56 * 1024 * 1024

QKV_W = ATTN_WIDTH + 2 * N_KV_HEADS * HEAD_DIM + N_IDX_HEADS * IDX_DIM
COL_QKV = 0
COL_KI = COL_QKV + QKV_W
COL_Z = COL_KI + LANES
COL_XBC = COL_Z + SSD_D_INNER
COL_DTW = COL_XBC + CONV_DIM
W_CAT = COL_DTW + LANES
DTW_WI = SSD_N_HEADS

NEG_INF = float("-inf")


def _rms(x, g):
    return x * lax.rsqrt(jnp.mean(x * x, axis=-1, keepdims=True) + EPS) * g


def _in_proj_kernel(x_ref, g_ref, w_ref, lnw_ref, lnb_ref,
                    qkv_ref, ki_ref, z_ref, xbc_ref, dtw_ref):
    u = _rms(x_ref[...], g_ref[...]).astype(BF16)

    def mm(lo, hi):
        return jnp.dot(u, w_ref[:, lo:hi], preferred_element_type=F32)

    qkv_ref[...] = mm(COL_QKV, COL_KI).astype(BF16)
    ki = mm(COL_KI, COL_Z)[:, :IDX_DIM]
    mu = jnp.mean(ki, axis=-1, keepdims=True)
    var = jnp.mean(jnp.square(ki - mu), axis=-1, keepdims=True)
    ki_ref[...] = ((ki - mu) * lax.rsqrt(var + EPS) * lnw_ref[...] + lnb_ref[...]).astype(BF16)
    z_ref[...] = mm(COL_Z, COL_XBC)
    xbc_ref[...] = mm(COL_XBC, COL_DTW)
    dtw_ref[...] = mm(COL_DTW, W_CAT)


def _in_proj(x2, g, w_cat, lnw, lnb, tm):
    n = x2.shape[0]
    row = lambda i: (i, 0)
    const = lambda i: (0, 0)
    return pl.pallas_call(
        _in_proj_kernel,
        grid=(n // tm,),
        in_specs=[
            pl.BlockSpec((tm, D_MODEL), row),
            pl.BlockSpec((1, D_MODEL), const),
            pl.BlockSpec((D_MODEL, W_CAT), const),
            pl.BlockSpec((1, IDX_DIM), const),
            pl.BlockSpec((1, IDX_DIM), const),
        ],
        out_specs=[
            pl.BlockSpec((tm, QKV_W), row),
            pl.BlockSpec((tm, IDX_DIM), row),
            pl.BlockSpec((tm, SSD_D_INNER), row),
            pl.BlockSpec((tm, CONV_DIM), row),
            pl.BlockSpec((tm, LANES), row),
        ],
        out_shape=[
            jax.ShapeDtypeStruct((n, QKV_W), BF16),
            jax.ShapeDtypeStruct((n, IDX_DIM), BF16),
            jax.ShapeDtypeStruct((n, SSD_D_INNER), F32),
            jax.ShapeDtypeStruct((n, CONV_DIM), F32),
            jax.ShapeDtypeStruct((n, LANES), F32),
        ],
        compiler_params=pltpu.CompilerParams(
            dimension_semantics=("arbitrary",), vmem_limit_bytes=VMEM_LIMIT),
        name="in_proj",
    )(x2, g, w_cat, lnw, lnb)


def _bias_kernel(tbl_ref, out_ref):
    tq = lax.broadcasted_iota(I32, (TILE, TILE), 0)
    sk = lax.broadcasted_iota(I32, (TILE, TILE), 1)
    max_exact = NUM_BUCKETS // 2
    for off in range(2):
        dist = jnp.maximum(off * TILE + tq - sk, 0)
        df = jnp.maximum(dist, 1).astype(F32)
        large = max_exact + (jnp.log(df / max_exact) / math.log(MAX_DISTANCE / max_exact)
                             * (NUM_BUCKETS - max_exact)).astype(I32)
        large = jnp.minimum(large, NUM_BUCKETS - 1)
        bucket = jnp.where(dist < max_exact, dist, large)
        for h in range(N_ATTN_HEADS):
            acc = jnp.zeros((TILE, TILE), F32)
            for b in range(NUM_BUCKETS):
                acc = jnp.where(bucket == b, tbl_ref[b, h], acc)
            out_ref[h, off] = acc - tbl_ref[NUM_BUCKETS - 1, h]
    for h in range(N_ATTN_HEADS):
        out_ref[h, 2] = jnp.zeros((TILE, TILE), F32)


def _bias_tiles(rel_bias):
    return pl.pallas_call(
        _bias_kernel,
        in_specs=[pl.BlockSpec(memory_space=pltpu.SMEM)],
        out_specs=pl.BlockSpec(memory_space=pltpu.VMEM),
        out_shape=jax.ShapeDtypeStruct((N_ATTN_HEADS, 3, TILE, TILE), F32),
        name="bias_tiles",
    )(rel_bias)


def _key_to_f32(u):
    ks = u ^ I32(-2 ** 31)
    bits = jnp.where(ks >= 0, ks, ks ^ I32(0x7FFFFFFF))
    return lax.bitcast_convert_type(bits, F32)


def _attn_kernel(q_ref, kv_ref, qi_ref, ki_ref, dtw_ref, bias_ref, o_ref,
                 sc_ref, mask_ref, lg_ref, m_ref, l_ref, acc_ref, cut_ref, *, seq_len, n_sel):
    i = pl.program_id(1)
    nt = i + 1
    rep = N_ATTN_HEADS // N_KV_HEADS
    gq = rep * TILE
    nt_dims = (((1,), (1,)), ((), ()))

    w_t = dtw_ref[0].T
    idx_scale = (N_IDX_HEADS ** -0.5) * (IDX_DIM ** -0.5)
    w_rows = [w_t[DTW_WI + h:DTW_WI + h + 1, :] * idx_scale for h in range(N_IDX_HEADS)]
    qi = qi_ref[0]
    qi_h = [qi[:, h * IDX_DIM:(h + 1) * IDX_DIM] for h in range(N_IDX_HEADS)]
    s_loc = lax.broadcasted_iota(I32, (TILE, TILE), 0)
    t_loc = lax.broadcasted_iota(I32, (TILE, TILE), 1)

    def score_tile(j):
        kt = ki_ref[0, pl.ds(pl.multiple_of(j * TILE, TILE), TILE), :]
        s = jnp.zeros((TILE, TILE), F32)
        for h in range(N_IDX_HEADS):
            d = lax.dot_general(kt, qi_h[h], nt_dims, preferred_element_type=F32)
            s = s + w_rows[h] * jnp.maximum(d, 0.0)
        return s

    def score_body(j, carry):
        sc_ref[j] = score_tile(j)
        return carry

    lax.fori_loop(0, i, score_body, 0)
    sc_ref[i] = jnp.where(s_loc <= t_loc, score_tile(i), NEG_INF)

    t_glob = i * TILE + lax.broadcasted_iota(I32, (1, TILE), 1)
    k_eff = jnp.minimum(n_sel, t_glob + 1)

    def count_tiles(pred):
        def body(j, acc):
            c = jnp.where(pred(j, sc_ref[j]), 1, 0).astype(I32)
            return acc + c.reshape(TILE // SUBLANES, SUBLANES, TILE).sum(axis=0)
        acc = lax.fori_loop(0, nt, body, jnp.zeros((SUBLANES, TILE), I32))
        return acc.sum(axis=0, keepdims=True)

    def bit_body(b, prefix):
        cand = prefix | lax.shift_left(I32(1), 31 - b)
        thr = _key_to_f32(cand)
        cnt = count_tiles(lambda j, s: s >= thr)
        return jnp.where(cnt >= k_eff, cand, prefix)

    prefix = lax.fori_loop(0, 32, bit_body, jnp.zeros((1, TILE), I32))
    thr = _key_to_f32(prefix)

    cnt_gt = count_tiles(lambda j, s: s > thr)
    cnt_eq = count_tiles(lambda j, s: s == thr)
    need = k_eff - cnt_gt

    def tie_pos(j, s):
        neg_zero = (s == 0.0) & (lax.bitcast_convert_type(s, I32) < 0)
        return j * TILE + s_loc + jnp.where(neg_zero, seq_len, 0)

    pos_bits = int(math.ceil(math.log2(2 * seq_len)))
    cut_ref[...] = jnp.full((1, TILE), 2 ** pos_bits, I32)

    @pl.when(jnp.max(jnp.where(need != cnt_eq, 1, 0)) > 0)
    def _():
        def pos_body(b, c):
            cand = c | lax.shift_left(I32(1), pos_bits - 1 - b)
            cnt = count_tiles(lambda j, s: (s == thr) & (tie_pos(j, s) < cand))
            return jnp.where(cnt < need, cand, c)
        cut_ref[...] = lax.fori_loop(0, pos_bits, pos_body, jnp.zeros((1, TILE), I32))

    cut = cut_ref[...]
    need_pos = need > 0

    def mask_body(j, carry):
        s = sc_ref[j]
        sel = (s > thr) | ((s == thr) & (tie_pos(j, s) <= cut) & need_pos)
        mask_ref[j] = jnp.where(sel, 0.0, NEG_INF).T
        return carry

    lax.fori_loop(0, nt, mask_body, 0)

    q = q_ref[0]
    for g in range(N_KV_HEADS):
        qg = jnp.concatenate(
            [q[:, (g * rep + r) * HEAD_DIM:(g * rep + r + 1) * HEAD_DIM] for r in range(rep)],
            axis=0) * BF16(HEAD_DIM ** -0.5)
        m_ref[...] = jnp.full((gq, TILE), NEG_INF, F32)

        def logit_body(j, carry):
            rows = pl.ds(pl.multiple_of(j * TILE, TILE), TILE)
            kt = kv_ref[0, rows, g * HEAD_DIM:(g + 1) * HEAD_DIM]
            lg = lax.dot_general(qg, kt, nt_dims, preferred_element_type=F32)
            off = jnp.minimum(i - j, 2)
            lg = lg.reshape(rep, TILE, TILE) + mask_ref[j][None] + bias_ref[pl.ds(g * rep, rep), off]
            lg = lg.reshape(gq, TILE)
            lg_ref[j] = lg
            m_ref[...] = jnp.maximum(m_ref[...], lg)
            return carry

        lax.fori_loop(0, nt, logit_body, 0)
        m_row = jnp.max(m_ref[...], axis=-1, keepdims=True)
        l_ref[...] = jnp.zeros((gq, TILE), F32)
        acc_ref[...] = jnp.zeros((gq, HEAD_DIM), F32)

        def pv_body(j, carry):
            rows = pl.ds(pl.multiple_of(j * TILE, TILE), TILE)
            vt = kv_ref[0, rows, (N_KV_HEADS + g) * HEAD_DIM:(N_KV_HEADS + g + 1) * HEAD_DIM]
            p = jnp.exp(lg_ref[j] - m_row)
            l_ref[...] += p
            acc_ref[...] += jnp.dot(p.astype(BF16), vt, preferred_element_type=F32)
            return carry

        lax.fori_loop(0, nt, pv_body, 0)
        out = acc_ref[...] / jnp.sum(l_ref[...], axis=-1, keepdims=True)
        for r in range(rep):
            h = g * rep + r
            o_ref[0, :, h * HEAD_DIM:(h + 1) * HEAD_DIM] = out[r * TILE:(r + 1) * TILE].astype(BF16)


def _attention(qkv3, ki3, dtw3, bias, n_sel):
    bsz, seq_len, _ = qkv3.shape
    nq = seq_len // TILE
    rep = N_ATTN_HEADS // N_KV_HEADS
    kv_w = 2 * N_KV_HEADS * HEAD_DIM
    qi_w = N_IDX_HEADS * IDX_DIM
    kern = functools.partial(_attn_kernel, seq_len=seq_len, n_sel=n_sel)
    return pl.pallas_call(
        kern,
        grid=(bsz, nq),
        in_specs=[
            pl.BlockSpec((1, TILE, ATTN_WIDTH), lambda b, i: (b, i, 0)),
            pl.BlockSpec((1, seq_len, kv_w), lambda b, i: (b, 0, ATTN_WIDTH // kv_w)),
            pl.BlockSpec((1, TILE, qi_w), lambda b, i: (b, i, (ATTN_WIDTH + kv_w) // qi_w)),
            pl.BlockSpec((1, seq_len, IDX_DIM), lambda b, i: (b, 0, 0)),
            pl.BlockSpec((1, TILE, LANES), lambda b, i: (b, i, 0)),
            pl.BlockSpec((N_ATTN_HEADS, 3, TILE, TILE), lambda b, i: (0, 0, 0, 0)),
        ],
        out_specs=pl.BlockSpec((1, TILE, ATTN_WIDTH), lambda b, i: (b, i, 0)),
        out_shape=jax.ShapeDtypeStruct((bsz, seq_len, ATTN_WIDTH), BF16),
        scratch_shapes=[
            pltpu.VMEM((nq, TILE, TILE), F32),
            pltpu.VMEM((nq, TILE, TILE), F32),
            pltpu.VMEM((nq, rep * TILE, TILE), F32),
            pltpu.VMEM((rep * TILE, TILE), F32),
            pltpu.VMEM((rep * TILE, TILE), F32),
            pltpu.VMEM((rep * TILE, HEAD_DIM), F32),
            pltpu.VMEM((1, TILE), I32),
        ],
        compiler_params=pltpu.CompilerParams(
            dimension_semantics=("arbitrary", "arbitrary"), vmem_limit_bytes=VMEM_LIMIT),
        name="sparse_attn",
    )(qkv3, qkv3, qkv3, ki3, dtw3, bias)


def _ssd_kernel(z_ref, xbc_ref, dtw_ref, cw_ref, cb_ref, dtb_ref, alog_ref, dsk_ref, nw_ref,
                o_ref, xpad_ref, st_ref):
    c = pl.program_id(1)
    hp = lax.Precision.HIGHEST
    gn = SSD_N_GROUPS * SSD_D_STATE
    hpg = SSD_N_HEADS // SSD_N_GROUPS
    gw = hpg * SSD_HEAD_DIM

    @pl.when(c == 0)
    def _():
        xpad_ref[0:SUBLANES, :] = jnp.zeros((SUBLANES, CONV_DIM), F32)
        st_ref[...] = jnp.zeros_like(st_ref)

    xpad_ref[SUBLANES:SUBLANES + TILE, :] = xbc_ref[0]
    conv = cb_ref[...]
    for k in range(CONV_WIDTH):
        lo = SUBLANES - (CONV_WIDTH - 1) + k
        conv = conv + cw_ref[k:k + 1, :] * xpad_ref[lo:lo + TILE, :]
    xpad_ref[0:SUBLANES, :] = xpad_ref[TILE:TILE + SUBLANES, :]
    act = conv * (1.0 / (1.0 + jnp.exp(-conv)))
    xs = act[:, :SSD_D_INNER]
    bm = act[:, SSD_D_INNER:SSD_D_INNER + gn].astype(BF16)
    cm = act[:, SSD_D_INNER + gn:].astype(BF16)

    dt_in = dtw_ref[0] + dtb_ref[...]
    dt = jnp.maximum(dt_in, 0.0) + jnp.log1p(jnp.exp(-jnp.abs(dt_in)))
    adt = dt * (-jnp.exp(alog_ref[...]))
    row = lax.broadcasted_iota(I32, (TILE, TILE), 0)
    col = lax.broadcasted_iota(I32, (TILE, TILE), 1)
    causal = row >= col
    acs = jnp.dot(causal.astype(F32), adt, precision=hp, preferred_element_type=F32)
    acs_t = acs.T
    a_last = acs[TILE - 1:TILE, :]
    hsel = (lax.broadcasted_iota(I32, (LANES, SSD_D_INNER), 0)
            == lax.broadcasted_iota(I32, (LANES, SSD_D_INNER), 1) // SSD_HEAD_DIM).astype(F32)
    expand = lambda v: jnp.dot(v, hsel, precision=hp, preferred_element_type=F32)
    dt_x = expand(dt)
    in_decay_x = expand(jnp.exp(acs))
    out_decay_x = expand(jnp.exp(a_last - acs))
    chunk_decay_x = expand(jnp.exp(a_last))

    x_dt = xs * dt_x
    x_dt_b = x_dt.astype(BF16)
    x_out_b = (x_dt * out_decay_x).astype(BF16)
    tn_dims = (((0,), (0,)), ((), ()))
    nt_dims = (((1,), (1,)), ((), ()))
    y_parts = []
    for g in range(SSD_N_GROUPS):
        bg = bm[:, g * SSD_D_STATE:(g + 1) * SSD_D_STATE]
        cg = cm[:, g * SSD_D_STATE:(g + 1) * SSD_D_STATE]
        lanes = slice(g * gw, (g + 1) * gw)
        cb = lax.dot_general(cg, bg, nt_dims, preferred_element_type=F32)
        y_diag = []
        for r in range(hpg):
            h = g * hpg + r
            seg = jnp.where(causal, acs[:, h:h + 1] - acs_t[h:h + 1, :], NEG_INF)
            w = (cb * jnp.exp(seg)).astype(BF16)
            y_diag.append(jnp.dot(w, x_dt_b[:, h * SSD_HEAD_DIM:(h + 1) * SSD_HEAD_DIM],
                                  preferred_element_type=F32))
        st_prev = st_ref[g]
        y_off = jnp.dot(cg, st_prev.astype(BF16), preferred_element_type=F32) * in_decay_x[:, lanes]
        y_parts.append(jnp.concatenate(y_diag, axis=-1) + y_off)
        st_new = lax.dot_general(bg, x_out_b[:, lanes], tn_dims, preferred_element_type=F32)
        st_ref[g] = st_prev * chunk_decay_x[:, lanes] + st_new

    y = jnp.concatenate(y_parts, axis=-1) + xs * dsk_ref[...]
    zv = z_ref[0]
    y = y * (zv * (1.0 / (1.0 + jnp.exp(-zv))))
    ng = SSD_D_INNER // SSD_N_GROUPS
    outs = []
    for g in range(SSD_N_GROUPS):
        yg = y[:, g * ng:(g + 1) * ng]
        outs.append(yg * lax.rsqrt(jnp.mean(yg * yg, axis=-1, keepdims=True) + EPS))
    o_ref[0] = (jnp.concatenate(outs, axis=-1) * nw_ref[...]).astype(BF16)


def _ssd(z3, xbc3, dtw3, conv_w, conv_b, dt_bias, a_log, d_skip, norm_w):
    bsz, seq_len, _ = z3.shape
    nc = seq_len // TILE
    blk = lambda w: pl.BlockSpec((1, TILE, w), lambda b, c: (b, c, 0))
    par = lambda r, w: pl.BlockSpec((r, w), lambda b, c: (0, 0))
    return pl.pallas_call(
        _ssd_kernel,
        grid=(bsz, nc),
        in_specs=[blk(SSD_D_INNER), blk(CONV_DIM), blk(LANES),
                  par(CONV_WIDTH, CONV_DIM), par(1, CONV_DIM), par(1, LANES), par(1, LANES),
                  par(1, SSD_D_INNER), par(1, SSD_D_INNER)],
        out_specs=blk(SSD_D_INNER),
        out_shape=jax.ShapeDtypeStruct((bsz, seq_len, SSD_D_INNER), BF16),
        scratch_shapes=[
            pltpu.VMEM((SUBLANES + TILE, CONV_DIM), F32),
            pltpu.VMEM((SSD_N_GROUPS, SSD_D_STATE, SSD_D_INNER // SSD_N_GROUPS), F32),
        ],
        compiler_params=pltpu.CompilerParams(
            dimension_semantics=("arbitrary", "arbitrary"), vmem_limit_bytes=VMEM_LIMIT),
        name="ssd_mixer",
    )(z3, xbc3, dtw3, conv_w, conv_b, dt_bias, a_log, d_skip, norm_w)


FF_CHUNK = 512


def _mlp_kernel(x_ref, attn_ref, ssd_ref, wo_ref, g1_ref, g2_ref, wu_ref, wd_ref, g3_ref, o_ref):
    mix = (jnp.dot(attn_ref[...], wo_ref[:ATTN_WIDTH, :], preferred_element_type=F32)
           + jnp.dot(ssd_ref[...], wo_ref[ATTN_WIDTH:, :], preferred_element_type=F32))
    h1 = x_ref[...] + _rms(mix, g1_ref[...])
    u = _rms(h1, g2_ref[...]).astype(BF16)
    acc = jnp.zeros(h1.shape, F32)
    for c in range(0, D_FF, FF_CHUNK):
        f = jnp.dot(u, wu_ref[:, c:c + FF_CHUNK], preferred_element_type=F32)
        f = jnp.square(jnp.maximum(f, 0.0)).astype(BF16)
        acc = acc + jnp.dot(f, wd_ref[c:c + FF_CHUNK, :], preferred_element_type=F32)
    o_ref[...] = h1 + _rms(acc, g3_ref[...])


def _mlp(x2, attn2, ssd2, wo, g1, g2, wu, wd, g3, tm):
    n = x2.shape[0]
    row = lambda i: (i, 0)
    const = lambda i: (0, 0)
    single = dict(pipeline_mode=pl.Buffered(1))
    return pl.pallas_call(
        _mlp_kernel,
        grid=(n // tm,),
        in_specs=[
            pl.BlockSpec((tm, D_MODEL), row),
            pl.BlockSpec((tm, ATTN_WIDTH), row),
            pl.BlockSpec((tm, SSD_D_INNER), row),
            pl.BlockSpec((ATTN_WIDTH + SSD_D_INNER, D_MODEL), const, **single),
            pl.BlockSpec((1, D_MODEL), const),
            pl.BlockSpec((1, D_MODEL), const),
            pl.BlockSpec((D_MODEL, D_FF), const, **single),
            pl.BlockSpec((D_FF, D_MODEL), const, **single),
            pl.BlockSpec((1, D_MODEL), const),
        ],
        out_specs=pl.BlockSpec((tm, D_MODEL), row),
        out_shape=jax.ShapeDtypeStruct((n, D_MODEL), F32),
        compiler_params=pltpu.CompilerParams(
            dimension_semantics=("arbitrary",), vmem_limit_bytes=VMEM_LIMIT),
        name="out_proj_mlp",
    )(x2, attn2, ssd2, wo, g1, g2, wu, wd, g3)


def _pack_w_in(w):
    o = 0
    parts = {}
    for name, width in (("qkvqi", QKV_W), ("ki", IDX_DIM), ("wi", N_IDX_HEADS),
                        ("z", SSD_D_INNER), ("xbc", CONV_DIM), ("dt", SSD_N_HEADS)):
        parts[name] = w[:, o:o + width]
        o += width
    zeros = lambda n: jnp.zeros((w.shape[0], n), w.dtype)
    return jnp.concatenate([
        parts["qkvqi"], parts["ki"], zeros(LANES - IDX_DIM), parts["z"], parts["xbc"],
        parts["dt"], parts["wi"], zeros(LANES - SSD_N_HEADS - N_IDX_HEADS)], axis=1).astype(BF16)


def _pad_lanes(v, n):
    return jnp.pad(v, (0, n - v.shape[0])).reshape(1, n)


def kernel(x, norm_pre_mix, norm_post_mix, norm_pre_mlp, norm_post_mlp, w_in, k_idx_ln_w, k_idx_ln_b, conv_w, conv_b, dt_bias, a_log, d_skip, ssd_norm_w, w_out, w_mlp_up, w_mlp_down, rel_bias):
    bsz, seq_len, d = x.shape
    n = bsz * seq_len
    assert d == D_MODEL and seq_len % TILE == 0
    tm = 512 if n % 512 == 0 else TILE
    n_sel = min(TOPK_MAX, seq_len // 4)
    bias = _bias_tiles(rel_bias)
    h = x.reshape(n, d)
    for i in range(norm_pre_mix.shape[0]):
        row = lambda v: v[i].reshape(1, -1)
        qkv, ki, z, xbc, dtw = _in_proj(h, row(norm_pre_mix), _pack_w_in(w_in[i]),
                                        row(k_idx_ln_w), row(k_idx_ln_b), tm)
        r3 = lambda a: a.reshape(bsz, seq_len, a.shape[-1])
        dtw3 = r3(dtw)
        attn = _attention(r3(qkv), r3(ki), dtw3, bias, n_sel)
        ssd = _ssd(r3(z), r3(xbc), dtw3, conv_w[i], row(conv_b),
                   _pad_lanes(dt_bias[i], LANES), _pad_lanes(a_log[i], LANES),
                   jnp.repeat(d_skip[i], SSD_HEAD_DIM).reshape(1, -1), row(ssd_norm_w))
        h = _mlp(h, attn.reshape(n, -1), ssd.reshape(n, -1), w_out[i].astype(BF16),
                 row(norm_post_mix), row(norm_pre_mlp), w_mlp_up[i].astype(BF16),
                 w_mlp_down[i].astype(BF16), row(norm_post_mlp), tm)
    return h.reshape(bsz, seq_len, d)
```

```python
import functools
import math

import jax
import jax.numpy as jnp
from jax import lax
from jax.experimental import pallas as pl
from jax.experimental.pallas import tpu as pltpu

F32 = jnp.float32
BF16 = jnp.bfloat16
I32 = jnp.int32

D_MODEL = 1024
N_ATTN_HEADS = 8
N_KV_HEADS = 2
HEAD_DIM = 64
ATTN_WIDTH = N_ATTN_HEADS * HEAD_DIM
N_IDX_HEADS = 4
IDX_DIM = 64
TOPK_MAX = 256
NUM_BUCKETS = 32
MAX_DISTANCE = 128
SSD_D_INNER = 512
SSD_HEAD_DIM = 64
SSD_N_HEADS = 8
SSD_N_GROUPS = 2
SSD_D_STATE = 128
CONV_WIDTH = 4
CONV_DIM = SSD_D_INNER + 2 * SSD_N_GROUPS * SSD_D_STATE
D_FF = 4 * D_MODEL
EPS = 1e-6

LANES = 128
SUBLANES = 8
TILE = 128
VMEM_LIMIT = 56 * 1024 * 1024

QKV_W = ATTN_WIDTH + 2 * N_KV_HEADS * HEAD_DIM + N_IDX_HEADS * IDX_DIM
COL_QKV = 0
COL_KI = COL_QKV + QKV_W
COL_Z = COL_KI + LANES
COL_XBC = COL_Z + SSD_D_INNER
COL_DTW = COL_XBC + CONV_DIM
W_CAT = COL_DTW + LANES
DTW_WI = SSD_N_HEADS

NEG_INF = float("-inf")


def _rms(x, g):
    return x * lax.rsqrt(jnp.mean(x * x, axis=-1, keepdims=True) + EPS) * g


def _in_proj_kernel(x_ref, g_ref, w_ref, lnw_ref, lnb_ref,
                    qkv_ref, ki_ref, z_ref, xbc_ref, dtw_ref):
    u = _rms(x_ref[...], g_ref[...]).astype(BF16)

    def mm(lo, hi):
        return jnp.dot(u, w_ref[:, lo:hi], preferred_element_type=F32)

    qkv_ref[...] = mm(COL_QKV, COL_KI).astype(BF16)
    ki = mm(COL_KI, COL_Z)[:, :IDX_DIM]
    mu = jnp.mean(ki, axis=-1, keepdims=True)
    var = jnp.mean(jnp.square(ki - mu), axis=-1, keepdims=True)
    ki_ref[...] = ((ki - mu) * lax.rsqrt(var + EPS) * lnw_ref[...] + lnb_ref[...]).astype(BF16)
    z_ref[...] = mm(COL_Z, COL_XBC)
    xbc_ref[...] = mm(COL_XBC, COL_DTW)
    dtw_ref[...] = mm(COL_DTW, W_CAT)


def _in_proj(x2, g, w_cat, lnw, lnb, tm):
    n = x2.shape[0]
    row = lambda i: (i, 0)
    const = lambda i: (0, 0)
    return pl.pallas_call(
        _in_proj_kernel,
        grid=(n // tm,),
        in_specs=[
            pl.BlockSpec((tm, D_MODEL), row),
            pl.BlockSpec((1, D_MODEL), const),
            pl.BlockSpec((D_MODEL, W_CAT), const),
            pl.BlockSpec((1, IDX_DIM), const),
            pl.BlockSpec((1, IDX_DIM), const),
        ],
        out_specs=[
            pl.BlockSpec((tm, QKV_W), row),
            pl.BlockSpec((tm, IDX_DIM), row),
            pl.BlockSpec((tm, SSD_D_INNER), row),
            pl.BlockSpec((tm, CONV_DIM), row),
            pl.BlockSpec((tm, LANES), row),
        ],
        out_shape=[
            jax.ShapeDtypeStruct((n, QKV_W), BF16),
            jax.ShapeDtypeStruct((n, IDX_DIM), BF16),
            jax.ShapeDtypeStruct((n, SSD_D_INNER), F32),
            jax.ShapeDtypeStruct((n, CONV_DIM), F32),
            jax.ShapeDtypeStruct((n, LANES), F32),
        ],
        compiler_params=pltpu.CompilerParams(
            dimension_semantics=("arbitrary",), vmem_limit_bytes=VMEM_LIMIT),
        name="in_proj",
    )(x2, g, w_cat, lnw, lnb)


def _bias_kernel(tbl_ref, out_ref):
    sk = lax.broadcasted_iota(I32, (TILE, TILE), 0)
    tq = lax.broadcasted_iota(I32, (TILE, TILE), 1)
    max_exact = NUM_BUCKETS // 2
    for off in range(2):
        dist = jnp.maximum(off * TILE + tq - sk, 0)
        df = jnp.maximum(dist, 1).astype(F32)
        large = max_exact + (jnp.log(df / max_exact) / math.log(MAX_DISTANCE / max_exact)
                             * (NUM_BUCKETS - max_exact)).astype(I32)
        large = jnp.minimum(large, NUM_BUCKETS - 1)
        bucket = jnp.where(dist < max_exact, dist, large)
        for h in range(N_ATTN_HEADS):
            acc = jnp.zeros((TILE, TILE), F32)
            for b in range(NUM_BUCKETS):
                acc = jnp.where(bucket == b, tbl_ref[b, h], acc)
            out_ref[h, off] = acc - tbl_ref[NUM_BUCKETS - 1, h]
    for h in range(N_ATTN_HEADS):
        out_ref[h, 2] = jnp.zeros((TILE, TILE), F32)


def _bias_tiles(rel_bias):
    return pl.pallas_call(
        _bias_kernel,
        in_specs=[pl.BlockSpec(memory_space=pltpu.SMEM)],
        out_specs=pl.BlockSpec(memory_space=pltpu.VMEM),
        out_shape=jax.ShapeDtypeStruct((N_ATTN_HEADS, 3, TILE, TILE), F32),
        name="bias_tiles",
    )(rel_bias)


def _key_to_f32(u):
    ks = u ^ I32(-2 ** 31)
    bits = jnp.where(ks >= 0, ks, ks ^ I32(0x7FFFFFFF))
    return lax.bitcast_convert_type(bits, F32)


PAIR = 2 * TILE
QUAD = 4 * TILE
M_FLOOR = -3.0e38


def _attn_kernel(q_ref, kv_ref, qi_ref, ki_ref, dtw_ref, bias_ref, o_ref,
                 sc_ref, mask_ref, vt_ref, m_ref, l_ref, acc_ref, cut_ref, *, seq_len, n_sel):
    i = pl.program_id(1)
    n_quads = seq_len // QUAD
    nq4 = i // 4 + 1
    npair = i // 2 + 1
    rep = N_ATTN_HEADS // N_KV_HEADS
    gl = rep * TILE
    nt_dims = (((1,), (1,)), ((), ()))
    v_lo = N_KV_HEADS * HEAD_DIM

    @pl.when(i == 0)
    def _():
        vt_ref[...] = kv_ref[0, :, v_lo:].astype(F32).T.astype(BF16)

    w_t = dtw_ref[0].T
    idx_scale = (N_IDX_HEADS ** -0.5) * (IDX_DIM ** -0.5)
    w_rows = [w_t[DTW_WI + h:DTW_WI + h + 1, :] * idx_scale for h in range(N_IDX_HEADS)]
    qi = qi_ref[0]
    qi_all = jnp.concatenate(
        [qi[:, h * IDX_DIM:(h + 1) * IDX_DIM] for h in range(N_IDX_HEADS)], axis=0)
    s_loc = lax.broadcasted_iota(I32, (PAIR, TILE), 0)
    s_minus_t = s_loc - lax.broadcasted_iota(I32, (PAIR, TILE), 1)

    for c in range(n_quads):
        @pl.when(c < nq4)
        def _(c=c):
            kt = ki_ref[0, c * QUAD:(c + 1) * QUAD, :]
            d = lax.dot_general(kt, qi_all, nt_dims, preferred_element_type=F32)
            s = jnp.zeros((QUAD, TILE), F32)
            for h in range(N_IDX_HEADS):
                s = s + w_rows[h] * jnp.maximum(d[:, h * TILE:(h + 1) * TILE], 0.0)
            for u in range(QUAD // PAIR):
                pr = c * (QUAD // PAIR) + u
                sc_ref[pr] = jnp.where(s_minus_t <= i * TILE - pr * PAIR,
                                       s[u * PAIR:(u + 1) * PAIR], NEG_INF)

    t_glob = i * TILE + lax.broadcasted_iota(I32, (1, TILE), 1)
    k_eff = jnp.minimum(n_sel, t_glob + 1).astype(F32)
    acc_rows = 4 * SUBLANES

    def count_pairs(pred):
        def body(c, acc):
            cnt = jnp.where(pred(c, sc_ref[c]), 1.0, 0.0)
            return acc + cnt.reshape(PAIR // acc_rows, acc_rows, TILE).sum(axis=0)
        acc = lax.fori_loop(0, npair, body, jnp.zeros((acc_rows, TILE), F32))
        return acc.sum(axis=0, keepdims=True)

    def bit_body(b, prefix):
        cand = prefix | lax.shift_left(I32(1), 31 - b)
        thr = _key_to_f32(cand)
        cnt = count_pairs(lambda c, s: s >= thr)
        return jnp.where(cnt >= k_eff, cand, prefix)

    prefix = lax.fori_loop(0, 32, bit_body, jnp.zeros((1, TILE), I32))
    thr = _key_to_f32(prefix)

    cnt_gt = count_pairs(lambda c, s: s > thr)
    cnt_eq = count_pairs(lambda c, s: s == thr)
    need = k_eff - cnt_gt
    has_tie = jnp.max(jnp.where(need != cnt_eq, 1.0, 0.0)) > 0.5
    n_mask_pairs = nq4 * (QUAD // PAIR)

    def tie_pos(c, s):
        neg_zero = (s == 0.0) & (lax.bitcast_convert_type(s, I32) < 0)
        return c * PAIR + s_loc + jnp.where(neg_zero, seq_len, 0)

    @pl.when(has_tie)
    def _():
        pos_bits = int(math.ceil(math.log2(2 * seq_len)))

        def pos_body(b, cur):
            cand = cur | lax.shift_left(I32(1), pos_bits - 1 - b)
            cnt = count_pairs(lambda c, s: (s == thr) & (tie_pos(c, s) < cand))
            return jnp.where(cnt < need, cand, cur)

        cut = lax.fori_loop(0, pos_bits, pos_body, jnp.zeros((1, TILE), I32))
        need_pos = need > 0.0

        def mask_body(c, carry):
            s = sc_ref[c]
            sel = (s > thr) | ((s == thr) & (tie_pos(c, s) <= cut) & need_pos)
            mask_ref[c] = jnp.where(sel, 0.0, NEG_INF)
            return carry

        lax.fori_loop(0, n_mask_pairs, mask_body, 0)

    @pl.when(jnp.logical_not(has_tie))
    def _():
        def mask_body(c, carry):
            mask_ref[c] = jnp.where(sc_ref[c] >= thr, 0.0, NEG_INF)
            return carry

        lax.fori_loop(0, n_mask_pairs, mask_body, 0)

    q = q_ref[0]
    qgs = [jnp.concatenate(
        [q[:, (g * rep + r) * HEAD_DIM:(g * rep + r + 1) * HEAD_DIM] for r in range(rep)],
        axis=0) * BF16(HEAD_DIM ** -0.5) for g in range(N_KV_HEADS)]
    m_ref[...] = jnp.full(m_ref.shape, M_FLOOR, F32)
    l_ref[...] = jnp.zeros(l_ref.shape, F32)
    acc_ref[...] = jnp.zeros(acc_ref.shape, F32)

    def attn_quad(c, with_bias):
        keys = slice(c * QUAD, (c + 1) * QUAD)
        mask = jnp.concatenate(
            [mask_ref[c * (QUAD // PAIR) + u] for u in range(QUAD // PAIR)], axis=0)
        mask = jnp.concatenate([mask] * rep, axis=1)
        for g in range(N_KV_HEADS):
            kt = kv_ref[0, keys, g * HEAD_DIM:(g + 1) * HEAD_DIM]
            lg = lax.dot_general(kt, qgs[g], nt_dims, preferred_element_type=F32) + mask
            if with_bias:
                lg = lg + jnp.concatenate(
                    [jnp.concatenate(
                        [bias_ref[g * rep + r, jnp.clip(i - (c * (QUAD // TILE) + u), 0, 2)]
                         for r in range(rep)], axis=1)
                     for u in range(QUAD // TILE)], axis=0)
            m_old = m_ref[g]
            m_new = jnp.maximum(m_old, jnp.max(lg, axis=0, keepdims=True))
            alpha = jnp.exp(m_old - m_new)
            p = jnp.exp(lg - m_new)
            m_ref[g] = m_new
            l_ref[g] = l_ref[g] * alpha + jnp.sum(p, axis=0, keepdims=True)
            vt = vt_ref[g * HEAD_DIM:(g + 1) * HEAD_DIM, keys]
            acc_ref[g] = acc_ref[g] * alpha + jnp.dot(vt, p.astype(BF16), preferred_element_type=F32)

    tiles_per_quad = QUAD // TILE
    for c in range(n_quads):
        far = (c + 1) * tiles_per_quad - 1 <= i - 2
        pl.when(far)(functools.partial(attn_quad, c, False))
        pl.when((c < nq4) & jnp.logical_not(far))(functools.partial(attn_quad, c, True))

    outs = []
    for g in range(N_KV_HEADS):
        o_g = acc_ref[g] / l_ref[g]
        outs += [o_g[:, r * TILE:(r + 1) * TILE] for r in range(rep)]
    o_ref[0] = jnp.concatenate(outs, axis=0).T.astype(BF16)


def _attention(qkv3, ki3, dtw3, bias, n_sel):
    bsz, seq_len, _ = qkv3.shape
    assert seq_len % QUAD == 0
    nq = seq_len // TILE
    rep = N_ATTN_HEADS // N_KV_HEADS
    kv_w = 2 * N_KV_HEADS * HEAD_DIM
    qi_w = N_IDX_HEADS * IDX_DIM
    kern = functools.partial(_attn_kernel, seq_len=seq_len, n_sel=n_sel)
    return pl.pallas_call(
        kern,
        grid=(bsz, nq),
        in_specs=[
            pl.BlockSpec((1, TILE, ATTN_WIDTH), lambda b, i: (b, i, 0)),
            pl.BlockSpec((1, seq_len, kv_w), lambda b, i: (b, 0, ATTN_WIDTH // kv_w)),
            pl.BlockSpec((1, TILE, qi_w), lambda b, i: (b, i, (ATTN_WIDTH + kv_w) // qi_w)),
            pl.BlockSpec((1, seq_len, IDX_DIM), lambda b, i: (b, 0, 0)),
            pl.BlockSpec((1, TILE, LANES), lambda b, i: (b, i, 0)),
            pl.BlockSpec((N_ATTN_HEADS, 3, TILE, TILE), lambda b, i: (0, 0, 0, 0)),
        ],
        out_specs=pl.BlockSpec((1, TILE, ATTN_WIDTH), lambda b, i: (b, i, 0)),
        out_shape=jax.ShapeDtypeStruct((bsz, seq_len, ATTN_WIDTH), BF16),
        scratch_shapes=[
            pltpu.VMEM((seq_len // PAIR, PAIR, TILE), F32),
            pltpu.VMEM((seq_len // PAIR, PAIR, TILE), F32),
            pltpu.VMEM((N_KV_HEADS * HEAD_DIM, seq_len), BF16),
            pltpu.VMEM((N_KV_HEADS, 1, rep * TILE), F32),
            pltpu.VMEM((N_KV_HEADS, 1, rep * TILE), F32),
            pltpu.VMEM((N_KV_HEADS, HEAD_DIM, rep * TILE), F32),
            pltpu.VMEM((1, TILE), I32),
        ],
        compiler_params=pltpu.CompilerParams(
            dimension_semantics=("arbitrary", "arbitrary"), vmem_limit_bytes=VMEM_LIMIT),
        name="sparse_attn",
    )(qkv3, qkv3, qkv3, ki3, dtw3, bias)


def _ssd_kernel(z_ref, xbc_ref, dtw_ref, cw_ref, cb_ref, dtb_ref, alog_ref, dsk_ref, nw_ref,
                o_ref, xpad_ref, st_ref):
    c = pl.program_id(1)
    hp = lax.Precision.HIGHEST
    gn = SSD_N_GROUPS * SSD_D_STATE
    hpg = SSD_N_HEADS // SSD_N_GROUPS
    gw = hpg * SSD_HEAD_DIM

    @pl.when(c == 0)
    def _():
        xpad_ref[0:SUBLANES, :] = jnp.zeros((SUBLANES, CONV_DIM), F32)
        st_ref[...] = jnp.zeros_like(st_ref)

    xpad_ref[SUBLANES:SUBLANES + TILE, :] = xbc_ref[0]
    conv = cb_ref[...]
    for k in range(CONV_WIDTH):
        lo = SUBLANES - (CONV_WIDTH - 1) + k
        conv = conv + cw_ref[k:k + 1, :] * xpad_ref[lo:lo + TILE, :]
    xpad_ref[0:SUBLANES, :] = xpad_ref[TILE:TILE + SUBLANES, :]
    act = conv * (1.0 / (1.0 + jnp.exp(-conv)))
    xs = act[:, :SSD_D_INNER]
    bm = act[:, SSD_D_INNER:SSD_D_INNER + gn].astype(BF16)
    cm = act[:, SSD_D_INNER + gn:].astype(BF16)

    dt_in = dtw_ref[0] + dtb_ref[...]
    dt = jnp.maximum(dt_in, 0.0) + jnp.log1p(jnp.exp(-jnp.abs(dt_in)))
    adt = dt * (-jnp.exp(alog_ref[...]))
    row = lax.broadcasted_iota(I32, (TILE, TILE), 0)
    col = lax.broadcasted_iota(I32, (TILE, TILE), 1)
    causal = row >= col
    acs = jnp.dot(causal.astype(F32), adt, precision=hp, preferred_element_type=F32)
    acs_t = acs.T
    a_last = acs[TILE - 1:TILE, :]
    hsel = (lax.broadcasted_iota(I32, (LANES, SSD_D_INNER), 0)
            == lax.broadcasted_iota(I32, (LANES, SSD_D_INNER), 1) // SSD_HEAD_DIM).astype(F32)
    expand = lambda v: jnp.dot(v, hsel, precision=hp, preferred_element_type=F32)
    dt_x = expand(dt)
    in_decay_x = expand(jnp.exp(acs))
    out_decay_x = expand(jnp.exp(a_last - acs))
    chunk_decay_x = expand(jnp.exp(a_last))

    x_dt = xs * dt_x
    x_dt_b = x_dt.astype(BF16)
    x_out_b = (x_dt * out_decay_x).astype(BF16)
    tn_dims = (((0,), (0,)), ((), ()))
    nt_dims = (((1,), (1,)), ((), ()))
    y_parts = []
    for g in range(SSD_N_GROUPS):
        bg = bm[:, g * SSD_D_STATE:(g + 1) * SSD_D_STATE]
        cg = cm[:, g * SSD_D_STATE:(g + 1) * SSD_D_STATE]
        lanes = slice(g * gw, (g + 1) * gw)
        cb = lax.dot_general(cg, bg, nt_dims, preferred_element_type=F32)
        y_diag = []
        for r in range(hpg):
            h = g * hpg + r
            seg = jnp.where(causal, acs[:, h:h + 1] - acs_t[h:h + 1, :], NEG_INF)
            w = (cb * jnp.exp(seg)).astype(BF16)
            y_diag.append(jnp.dot(w, x_dt_b[:, h * SSD_HEAD_DIM:(h + 1) * SSD_HEAD_DIM],
                                  preferred_element_type=F32))
        st_prev = st_ref[g]
        y_off = jnp.dot(cg, st_prev.astype(BF16), preferred_element_type=F32) * in_decay_x[:, lanes]
        y_parts.append(jnp.concatenate(y_diag, axis=-1) + y_off)
        st_new = lax.dot_general(bg, x_out_b[:, lanes], tn_dims, preferred_element_type=F32)
        st_ref[g] = st_prev * chunk_decay_x[:, lanes] + st_new

    y = jnp.concatenate(y_parts, axis=-1) + xs * dsk_ref[...]
    zv = z_ref[0]
    y = y * (zv * (1.0 / (1.0 + jnp.exp(-zv))))
    ng = SSD_D_INNER // SSD_N_GROUPS
    outs = []
    for g in range(SSD_N_GROUPS):
        yg = y[:, g * ng:(g + 1) * ng]
        outs.append(yg * lax.rsqrt(jnp.mean(yg * yg, axis=-1, keepdims=True) + EPS))
    o_ref[0] = (jnp.concatenate(outs, axis=-1) * nw_ref[...]).astype(BF16)


def _ssd(z3, xbc3, dtw3, conv_w, conv_b, dt_bias, a_log, d_skip, norm_w):
    bsz, seq_len, _ = z3.shape
    nc = seq_len // TILE
    blk = lambda w: pl.BlockSpec((1, TILE, w), lambda b, c: (b, c, 0))
    par = lambda r, w: pl.BlockSpec((r, w), lambda b, c: (0, 0))
    return pl.pallas_call(
        _ssd_kernel,
        grid=(bsz, nc),
        in_specs=[blk(SSD_D_INNER), blk(CONV_DIM), blk(LANES),
                  par(CONV_WIDTH, CONV_DIM), par(1, CONV_DIM), par(1, LANES), par(1, LANES),
                  par(1, SSD_D_INNER), par(1, SSD_D_INNER)],
        out_specs=blk(SSD_D_INNER),
        out_shape=jax.ShapeDtypeStruct((bsz, seq_len, SSD_D_INNER), BF16),
        scratch_shapes=[
            pltpu.VMEM((SUBLANES + TILE, CONV_DIM), F32),
            pltpu.VMEM((SSD_N_GROUPS, SSD_D_STATE, SSD_D_INNER // SSD_N_GROUPS), F32),
        ],
        compiler_params=pltpu.CompilerParams(
            dimension_semantics=("arbitrary", "arbitrary"), vmem_limit_bytes=VMEM_LIMIT),
        name="ssd_mixer",
    )(z3, xbc3, dtw3, conv_w, conv_b, dt_bias, a_log, d_skip, norm_w)


FF_CHUNK = 512


def _mlp_kernel(x_ref, attn_ref, ssd_ref, wo_ref, g1_ref, g2_ref, wu_ref, wd_ref, g3_ref, o_ref):
    mix = (jnp.dot(attn_ref[...], wo_ref[:ATTN_WIDTH, :], preferred_element_type=F32)
           + jnp.dot(ssd_ref[...], wo_ref[ATTN_WIDTH:, :], preferred_element_type=F32))
    h1 = x_ref[...] + _rms(mix, g1_ref[...])
    u = _rms(h1, g2_ref[...]).astype(BF16)
    acc = jnp.zeros(h1.shape, F32)
    for c in range(0, D_FF, FF_CHUNK):
        f = jnp.dot(u, wu_ref[:, c:c + FF_CHUNK], preferred_element_type=F32)
        f = jnp.square(jnp.maximum(f, 0.0)).astype(BF16)
        acc = acc + jnp.dot(f, wd_ref[c:c + FF_CHUNK, :], preferred_element_type=F32)
    o_ref[...] = h1 + _rms(acc, g3_ref[...])


def _mlp(x2, attn2, ssd2, wo, g1, g2, wu, wd, g3, tm):
    n = x2.shape[0]
    row = lambda i: (i, 0)
    const = lambda i: (0, 0)
    single = dict(pipeline_mode=pl.Buffered(1))
    return pl.pallas_call(
        _mlp_kernel,
        grid=(n // tm,),
        in_specs=[
            pl.BlockSpec((tm, D_MODEL), row),
            pl.BlockSpec((tm, ATTN_WIDTH), row),
            pl.BlockSpec((tm, SSD_D_INNER), row),
            pl.BlockSpec((ATTN_WIDTH + SSD_D_INNER, D_MODEL), const, **single),
            pl.BlockSpec((1, D_MODEL), const),
            pl.BlockSpec((1, D_MODEL), const),
            pl.BlockSpec((D_MODEL, D_FF), const, **single),
            pl.BlockSpec((D_FF, D_MODEL), const, **single),
            pl.BlockSpec((1, D_MODEL), const),
        ],
        out_specs=pl.BlockSpec((tm, D_MODEL), row),
        out_shape=jax.ShapeDtypeStruct((n, D_MODEL), F32),
        compiler_params=pltpu.CompilerParams(
            dimension_semantics=("arbitrary",), vmem_limit_bytes=VMEM_LIMIT),
        name="out_proj_mlp",
    )(x2, attn2, ssd2, wo, g1, g2, wu, wd, g3)


def _pack_w_in(w):
    o = 0
    parts = {}
    for name, width in (("qkvqi", QKV_W), ("ki", IDX_DIM), ("wi", N_IDX_HEADS),
                        ("z", SSD_D_INNER), ("xbc", CONV_DIM), ("dt", SSD_N_HEADS)):
        parts[name] = w[:, o:o + width]
        o += width
    zeros = lambda n: jnp.zeros((w.shape[0], n), w.dtype)
    return jnp.concatenate([
        parts["qkvqi"], parts["ki"], zeros(LANES - IDX_DIM), parts["z"], parts["xbc"],
        parts["dt"], parts["wi"], zeros(LANES - SSD_N_HEADS - N_IDX_HEADS)], axis=1).astype(BF16)


def _pad_lanes(v, n):
    return jnp.pad(v, (0, n - v.shape[0])).reshape(1, n)


def kernel(x, norm_pre_mix, norm_post_mix, norm_pre_mlp, norm_post_mlp, w_in, k_idx_ln_w, k_idx_ln_b, conv_w, conv_b, dt_bias, a_log, d_skip, ssd_norm_w, w_out, w_mlp_up, w_mlp_down, rel_bias):
    bsz, seq_len, d = x.shape
    n = bsz * seq_len
    assert d == D_MODEL and seq_len % TILE == 0
    tm = 512 if n % 512 == 0 else TILE
    n_sel = min(TOPK_MAX, seq_len // 4)
    bias = _bias_tiles(rel_bias)
    h = x.reshape(n, d)
    for i in range(norm_pre_mix.shape[0]):
        row = lambda v: v[i].reshape(1, -1)
        qkv, ki, z, xbc, dtw = _in_proj(h, row(norm_pre_mix), _pack_w_in(w_in[i]),
                                        row(k_idx_ln_w), row(k_idx_ln_b), tm)
        r3 = lambda a: a.reshape(bsz, seq_len, a.shape[-1])
        dtw3 = r3(dtw)
        attn = _attention(r3(qkv), r3(ki), dtw3, bias, n_sel)
        ssd = _ssd(r3(z), r3(xbc), dtw3, conv_w[i], row(conv_b),
                   _pad_lanes(dt_bias[i], LANES), _pad_lanes(a_log[i], LANES),
                   jnp.repeat(d_skip[i], SSD_HEAD_DIM).reshape(1, -1), row(ssd_norm_w))
        h = _mlp(h, attn.reshape(n, -1), ssd.reshape(n, -1), w_out[i].astype(BF16),
                 row(norm_post_mix), row(norm_pre_mlp), w_mlp_up[i].astype(BF16),
                 w_mlp_down[i].astype(BF16), row(norm_post_mlp), tm)
    return h.reshape(bsz, seq_len, d)
```

```python
import functools
import math

import jax
import jax.numpy as jnp
from jax import lax
from jax.experimental import pallas as pl
from jax.experimental.pallas import tpu as pltpu

F32 = jnp.float32
BF16 = jnp.bfloat16
I32 = jnp.int32

D_MODEL = 1024
N_ATTN_HEADS = 8
N_KV_HEADS = 2
HEAD_DIM = 64
ATTN_WIDTH = N_ATTN_HEADS * HEAD_DIM
N_IDX_HEADS = 4
IDX_DIM = 64
TOPK_MAX = 256
NUM_BUCKETS = 32
MAX_DISTANCE = 128
SSD_D_INNER = 512
SSD_HEAD_DIM = 64
SSD_N_HEADS = 8
SSD_N_GROUPS = 2
SSD_D_STATE = 128
CONV_WIDTH = 4
CONV_DIM = SSD_D_INNER + 2 * SSD_N_GROUPS * SSD_D_STATE
D_FF = 4 * D_MODEL
EPS = 1e-6

LANES = 128
SUBLANES = 8
TILE = 128
VMEM_LIMIT = 56 * 1024 * 1024

QKV_W = ATTN_WIDTH + 2 * N_KV_HEADS * HEAD_DIM + N_IDX_HEADS * IDX_DIM
COL_QKV = 0
COL_KI = COL_QKV + QKV_W
COL_Z = COL_KI + LANES
COL_XBC = COL_Z + SSD_D_INNER
COL_DTW = COL_XBC + CONV_DIM
W_CAT = COL_DTW + LANES
DTW_WI = SSD_N_HEADS

NEG_INF = float("-inf")
LOG2E = 1.4426950408889634
Q_SCALE = LOG2E * HEAD_DIM ** -0.5


def _rms(x, g):
    return x * lax.rsqrt(jnp.mean(x * x, axis=-1, keepdims=True) + EPS) * g


def _in_proj_kernel(x_ref, g_ref, w_ref, lnw_ref, lnb_ref,
                    qkv_ref, ki_ref, z_ref, xbc_ref, dtw_ref):
    u = _rms(x_ref[...], g_ref[...]).astype(BF16)

    def mm(lo, hi):
        return jnp.dot(u, w_ref[:, lo:hi], preferred_element_type=F32)

    qkv_ref[:, :ATTN_WIDTH] = (mm(COL_QKV, COL_QKV + ATTN_WIDTH) * Q_SCALE).astype(BF16)
    qkv_ref[:, ATTN_WIDTH:] = mm(COL_QKV + ATTN_WIDTH, COL_KI).astype(BF16)
    ki = mm(COL_KI, COL_Z)[:, :IDX_DIM]
    mu = jnp.mean(ki, axis=-1, keepdims=True)
    var = jnp.mean(jnp.square(ki - mu), axis=-1, keepdims=True)
    ki_ref[...] = ((ki - mu) * lax.rsqrt(var + EPS) * lnw_ref[...] + lnb_ref[...]).astype(BF16)
    z_ref[...] = mm(COL_Z, COL_XBC)
    xbc_ref[...] = mm(COL_XBC, COL_DTW)
    dtw_ref[...] = mm(COL_DTW, W_CAT)


def _in_proj(x2, g, w_cat, lnw, lnb, tm):
    n = x2.shape[0]
    row = lambda i: (i, 0)
    const = lambda i: (0, 0)
    return pl.pallas_call(
        _in_proj_kernel,
        grid=(n // tm,),
        in_specs=[
            pl.BlockSpec((tm, D_MODEL), row),
            pl.BlockSpec((1, D_MODEL), const),
            pl.BlockSpec((D_MODEL, W_CAT), const),
            pl.BlockSpec((1, IDX_DIM), const),
            pl.BlockSpec((1, IDX_DIM), const),
        ],
        out_specs=[
            pl.BlockSpec((tm, QKV_W), row),
            pl.BlockSpec((tm, IDX_DIM), row),
            pl.BlockSpec((tm, SSD_D_INNER), row),
            pl.BlockSpec((tm, CONV_DIM), row),
            pl.BlockSpec((tm, LANES), row),
        ],
        out_shape=[
            jax.ShapeDtypeStruct((n, QKV_W), BF16),
            jax.ShapeDtypeStruct((n, IDX_DIM), BF16),
            jax.ShapeDtypeStruct((n, SSD_D_INNER), F32),
            jax.ShapeDtypeStruct((n, CONV_DIM), F32),
            jax.ShapeDtypeStruct((n, LANES), F32),
        ],
        compiler_params=pltpu.CompilerParams(
            dimension_semantics=("arbitrary",), vmem_limit_bytes=VMEM_LIMIT),
        name="in_proj",
    )(x2, g, w_cat, lnw, lnb)


def _bias_kernel(tbl_ref, out_ref):
    sk = lax.broadcasted_iota(I32, (TILE, TILE), 0)
    tq = lax.broadcasted_iota(I32, (TILE, TILE), 1)
    max_exact = NUM_BUCKETS // 2
    for off in range(2):
        dist = jnp.maximum(off * TILE + tq - sk, 0)
        df = jnp.maximum(dist, 1).astype(F32)
        large = max_exact + (jnp.log(df / max_exact) / math.log(MAX_DISTANCE / max_exact)
                             * (NUM_BUCKETS - max_exact)).astype(I32)
        large = jnp.minimum(large, NUM_BUCKETS - 1)
        bucket = jnp.where(dist < max_exact, dist, large)
        for h in range(N_ATTN_HEADS):
            acc = jnp.zeros((TILE, TILE), F32)
            for b in range(NUM_BUCKETS):
                acc = jnp.where(bucket == b, tbl_ref[b, h], acc)
            out_ref[h, off] = (acc - tbl_ref[NUM_BUCKETS - 1, h]) * LOG2E
    for h in range(N_ATTN_HEADS):
        out_ref[h, 2] = jnp.zeros((TILE, TILE), F32)


def _bias_tiles(rel_bias):
    return pl.pallas_call(
        _bias_kernel,
        in_specs=[pl.BlockSpec(memory_space=pltpu.SMEM)],
        out_specs=pl.BlockSpec(memory_space=pltpu.VMEM),
        out_shape=jax.ShapeDtypeStruct((N_ATTN_HEADS, 3, TILE, TILE), F32),
        name="bias_tiles",
    )(rel_bias)


def _key_to_f32(u):
    ks = u ^ I32(-2 ** 31)
    bits = jnp.where(ks >= 0, ks, ks ^ I32(0x7FFFFFFF))
    return lax.bitcast_convert_type(bits, F32)


PAIR = 2 * TILE
QUAD = 4 * TILE
COARSE_BITS = 16
PV_ROWS = HEAD_DIM + 16


def _trunc_bf16(x):
    return lax.bitcast_convert_type(lax.bitcast_convert_type(x, I32) & I32(-65536), F32)


def _attn_kernel(q_ref, kv_ref, qi_ref, ki_ref, dtw_ref, bias_ref, o_ref,
                 sc_ref, sh_ref, vt_ref, lg_ref, *, seq_len, n_sel):
    i = pl.program_id(1)
    n_quads = seq_len // QUAD
    nq4 = i // 4 + 1
    npair = i // 2 + 1
    rep = N_ATTN_HEADS // N_KV_HEADS
    gl = rep * TILE
    nt_dims = (((1,), (1,)), ((), ()))
    v_lo = N_KV_HEADS * HEAD_DIM

    @pl.when(i == 0)
    def _():
        vt = kv_ref[0, :, v_lo:].astype(F32).T
        ones_row = jnp.where(lax.broadcasted_iota(I32, (PV_ROWS - HEAD_DIM, seq_len), 0) == 0, 1.0, 0.0)
        for g in range(N_KV_HEADS):
            vt_ref[g * PV_ROWS:g * PV_ROWS + HEAD_DIM, :] = vt[g * HEAD_DIM:(g + 1) * HEAD_DIM].astype(BF16)
            vt_ref[g * PV_ROWS + HEAD_DIM:(g + 1) * PV_ROWS, :] = ones_row.astype(BF16)

    w_t = dtw_ref[0].T
    idx_scale = (N_IDX_HEADS ** -0.5) * (IDX_DIM ** -0.5)
    w_rows = [w_t[DTW_WI + h:DTW_WI + h + 1, :] * idx_scale for h in range(N_IDX_HEADS)]
    qi = qi_ref[0]
    qi_all = jnp.concatenate(
        [qi[:, h * IDX_DIM:(h + 1) * IDX_DIM] for h in range(N_IDX_HEADS)], axis=0)
    s_loc = lax.broadcasted_iota(I32, (PAIR, TILE), 0)
    s_minus_t = s_loc - lax.broadcasted_iota(I32, (PAIR, TILE), 1)

    def score_quads(v):
        for c in range(v):
            kt = ki_ref[0, c * QUAD:(c + 1) * QUAD, :]
            d = lax.dot_general(kt, qi_all, nt_dims, preferred_element_type=F32)
            s = jnp.zeros((QUAD, TILE), F32)
            for h in range(N_IDX_HEADS):
                s = s + w_rows[h] * jnp.maximum(d[:, h * TILE:(h + 1) * TILE], 0.0)
            for u in range(QUAD // PAIR):
                pr = c * (QUAD // PAIR) + u
                su = jnp.where(s_minus_t <= i * TILE - pr * PAIR, s[u * PAIR:(u + 1) * PAIR], NEG_INF)
                sc_ref[pr] = su
                sh_ref[pr] = _trunc_bf16(su).astype(BF16)

    for v in range(1, n_quads + 1):
        pl.when(nq4 == v)(functools.partial(score_quads, v))

    t_glob = i * TILE + lax.broadcasted_iota(I32, (1, TILE), 1)
    k_eff = jnp.minimum(n_sel, t_glob + 1).astype(F32)
    acc_rows = 4 * SUBLANES
    acc_rows_b = 4 * 2 * SUBLANES

    def count_pairs(pred):
        def body(c, acc):
            cnt = jnp.where(pred(c, sc_ref[c]), 1.0, 0.0)
            return acc + cnt.reshape(PAIR // acc_rows, acc_rows, TILE).sum(axis=0)
        acc = lax.fori_loop(0, npair, body, jnp.zeros((acc_rows, TILE), F32))
        return acc.sum(axis=0, keepdims=True)

    def count_pairs_coarse(thr_b):
        one, zero = jnp.ones((), BF16), jnp.zeros((), BF16)

        def body(c, acc):
            cnt = jnp.where(sh_ref[c] >= thr_b, one, zero)
            parts = [cnt[k * acc_rows_b:(k + 1) * acc_rows_b] for k in range(PAIR // acc_rows_b)]
            while len(parts) > 1:
                parts = [a + b for a, b in zip(parts[::2], parts[1::2])]
            return acc + parts[0]
        acc = lax.fori_loop(0, npair, body, jnp.zeros((acc_rows_b, TILE), BF16))
        return acc.astype(F32).sum(axis=0, keepdims=True)

    def search_step(b, carry, coarse):
        prefix, cnt_ge = carry
        cand = prefix | lax.shift_left(I32(1), 31 - b)
        thr = _key_to_f32(cand)
        if coarse:
            cnt = count_pairs_coarse(_trunc_bf16(thr).astype(BF16))
        else:
            cnt = count_pairs(lambda c, s: s >= thr)
        ok = cnt >= k_eff
        return jnp.where(ok, cand, prefix), jnp.where(ok, cnt, cnt_ge)

    carry = (jnp.zeros((1, TILE), I32), (t_glob + 1).astype(F32))
    carry = lax.fori_loop(0, COARSE_BITS, functools.partial(search_step, coarse=True), carry)
    prefix, cnt_ge = lax.fori_loop(COARSE_BITS, 32, functools.partial(search_step, coarse=False), carry)
    thr = _key_to_f32(prefix)

    q = q_ref[0]
    qgs = [jnp.concatenate(
        [q[:, (g * rep + r) * HEAD_DIM:(g * rep + r + 1) * HEAD_DIM] for r in range(rep)],
        axis=0) for g in range(N_KV_HEADS)]
    below_diag = (lax.broadcasted_iota(I32, (QUAD, QUAD), 0)
                  > lax.broadcasted_iota(I32, (QUAD, QUAD), 1)).astype(BF16)
    tiles_per_quad = QUAD // TILE
    int_min = I32(-2 ** 31)

    def bias_rows(c, g, u):
        off = jnp.clip(i - (c * tiles_per_quad + u), 0, 2)
        return jnp.concatenate([bias_ref[g * rep + r, off] for r in range(rep)], axis=1)

    def quad_scores(c):
        pr = c * (QUAD // PAIR)
        return jnp.concatenate([sc_ref[pr + u] for u in range(QUAD // PAIR)], axis=0)

    def tied(s):
        eq = jnp.where(s == thr, 1.0, 0.0)
        neg0 = eq * jnp.where(lax.bitcast_convert_type(s, I32) == int_min, 1.0, 0.0)
        return eq - neg0, neg0

    def attend(v):
        ranks, tots = [], []
        for c in range(v):
            e2 = jnp.concatenate(tied(quad_scores(c)), axis=1).astype(BF16)
            r = jnp.dot(below_diag, e2, preferred_element_type=F32)
            ranks.append(r)
            tots.append(r[QUAD - 1:QUAD] + e2[QUAD - 1:QUAD].astype(F32))
        tot = tots[0]
        for t in tots[1:]:
            tot = tot + t
        need = k_eff - (cnt_ge - tot[:, :TILE] - tot[:, TILE:])
        off = jnp.concatenate([jnp.zeros((1, TILE), F32), tot[:, :TILE]], axis=1)

        m8 = [jnp.full((SUBLANES, gl), NEG_INF, F32) for _ in range(N_KV_HEADS)]
        for c in range(v):
            s = quad_scores(c)
            r = ranks[c] + off
            off = off + tots[c]
            neg0 = lax.bitcast_convert_type(s, I32) == int_min
            keep_tie = jnp.where(jnp.where(neg0, r[:, TILE:], r[:, :TILE]) < need, 0.0, NEG_INF)
            mask = jnp.where(s > thr, 0.0, jnp.where(s == thr, keep_tie, NEG_INF))
            mask = jnp.concatenate([mask] * rep, axis=1)
            keys = slice(c * QUAD, (c + 1) * QUAD)
            for g in range(N_KV_HEADS):
                kt = kv_ref[0, keys, g * HEAD_DIM:(g + 1) * HEAD_DIM]
                lg = lax.dot_general(kt, qgs[g], nt_dims, preferred_element_type=F32) + mask
                if c == v - 1:
                    lg = lg + jnp.concatenate([bias_rows(c, g, u) for u in range(tiles_per_quad)], axis=0)
                elif c == v - 2:
                    u = tiles_per_quad - 1
                    lg = jnp.concatenate([lg[:u * TILE], lg[u * TILE:] + bias_rows(c, g, u)], axis=0)
                lg_ref[c, g] = lg
                m8[g] = jnp.maximum(m8[g], lg.reshape(QUAD // SUBLANES, SUBLANES, gl).max(axis=0))

        outs = []
        for g in range(N_KV_HEADS):
            m_row = jnp.max(m8[g], axis=0, keepdims=True)
            acc = jnp.zeros((PV_ROWS, gl), F32)
            for c in range(v):
                p = jnp.exp2((lg_ref[c, g] - m_row).astype(BF16))
                vt = vt_ref[g * PV_ROWS:(g + 1) * PV_ROWS, c * QUAD:(c + 1) * QUAD]
                acc = acc + jnp.dot(vt, p, preferred_element_type=F32)
            o_g = acc[:HEAD_DIM] / acc[HEAD_DIM:HEAD_DIM + 1]
            outs += [o_g[:, r * TILE:(r + 1) * TILE] for r in range(rep)]
        o_ref[0] = jnp.concatenate(outs, axis=0).T.astype(BF16)

    for v in range(1, n_quads + 1):
        pl.when(nq4 == v)(functools.partial(attend, v))


def _attention(qkv3, ki3, dtw3, bias, n_sel):
    bsz, seq_len, _ = qkv3.shape
    assert seq_len % QUAD == 0
    nq = seq_len // TILE
    rep = N_ATTN_HEADS // N_KV_HEADS
    kv_w = 2 * N_KV_HEADS * HEAD_DIM
    qi_w = N_IDX_HEADS * IDX_DIM
    kern = functools.partial(_attn_kernel, seq_len=seq_len, n_sel=n_sel)
    return pl.pallas_call(
        kern,
        grid=(bsz, nq),
        in_specs=[
            pl.BlockSpec((1, TILE, ATTN_WIDTH), lambda b, i: (b, i, 0)),
            pl.BlockSpec((1, seq_len, kv_w), lambda b, i: (b, 0, ATTN_WIDTH // kv_w)),
            pl.BlockSpec((1, TILE, qi_w), lambda b, i: (b, i, (ATTN_WIDTH + kv_w) // qi_w)),
            pl.BlockSpec((1, seq_len, IDX_DIM), lambda b, i: (b, 0, 0)),
            pl.BlockSpec((1, TILE, LANES), lambda b, i: (b, i, 0)),
            pl.BlockSpec((N_ATTN_HEADS, 3, TILE, TILE), lambda b, i: (0, 0, 0, 0)),
        ],
        out_specs=pl.BlockSpec((1, TILE, ATTN_WIDTH), lambda b, i: (b, i, 0)),
        out_shape=jax.ShapeDtypeStruct((bsz, seq_len, ATTN_WIDTH), BF16),
        scratch_shapes=[
            pltpu.VMEM((seq_len // PAIR, PAIR, TILE), F32),
            pltpu.VMEM((seq_len // PAIR, PAIR, TILE), BF16),
            pltpu.VMEM((N_KV_HEADS * PV_ROWS, seq_len), BF16),
            pltpu.VMEM((seq_len // QUAD, N_KV_HEADS, QUAD, rep * TILE), F32),
        ],
        compiler_params=pltpu.CompilerParams(
            dimension_semantics=("arbitrary", "arbitrary"), vmem_limit_bytes=VMEM_LIMIT),
        name="sparse_attn",
    )(qkv3, qkv3, qkv3, ki3, dtw3, bias)


def _ssd_kernel(z_ref, xbc_ref, dtw_ref, cw_ref, cb_ref, dtb_ref, alog_ref, dsk_ref, nw_ref,
                o_ref, xpad_ref, st_ref):
    c = pl.program_id(1)
    hp = lax.Precision.HIGHEST
    gn = SSD_N_GROUPS * SSD_D_STATE
    hpg = SSD_N_HEADS // SSD_N_GROUPS
    gw = hpg * SSD_HEAD_DIM

    @pl.when(c == 0)
    def _():
        xpad_ref[0:SUBLANES, :] = jnp.zeros((SUBLANES, CONV_DIM), F32)
        st_ref[...] = jnp.zeros_like(st_ref)

    xpad_ref[SUBLANES:SUBLANES + TILE, :] = xbc_ref[0]
    conv = cb_ref[...]
    for k in range(CONV_WIDTH):
        lo = SUBLANES - (CONV_WIDTH - 1) + k
        conv = conv + cw_ref[k:k + 1, :] * xpad_ref[lo:lo + TILE, :]
    xpad_ref[0:SUBLANES, :] = xpad_ref[TILE:TILE + SUBLANES, :]
    act = conv * (1.0 / (1.0 + jnp.exp(-conv)))
    xs = act[:, :SSD_D_INNER]
    bm = act[:, SSD_D_INNER:SSD_D_INNER + gn].astype(BF16)
    cm = act[:, SSD_D_INNER + gn:].astype(BF16)

    dt_in = dtw_ref[0] + dtb_ref[...]
    dt = jnp.maximum(dt_in, 0.0) + jnp.log1p(jnp.exp(-jnp.abs(dt_in)))
    adt = dt * (-jnp.exp(alog_ref[...]))
    row = lax.broadcasted_iota(I32, (TILE, TILE), 0)
    col = lax.broadcasted_iota(I32, (TILE, TILE), 1)
    causal = row >= col
    acs = jnp.dot(causal.astype(F32), adt, precision=hp, preferred_element_type=F32)
    acs_t = acs.T
    a_last = acs[TILE - 1:TILE, :]
    hsel = (lax.broadcasted_iota(I32, (LANES, SSD_D_INNER), 0)
            == lax.broadcasted_iota(I32, (LANES, SSD_D_INNER), 1) // SSD_HEAD_DIM).astype(F32)
    expand = lambda v: jnp.dot(v, hsel, precision=hp, preferred_element_type=F32)
    dt_x = expand(dt)
    in_decay_x = expand(jnp.exp(acs))
    out_decay_x = expand(jnp.exp(a_last - acs))
    chunk_decay_x = expand(jnp.exp(a_last))

    x_dt = xs * dt_x
    x_dt_b = x_dt.astype(BF16)
    x_out_b = (x_dt * out_decay_x).astype(BF16)
    tn_dims = (((0,), (0,)), ((), ()))
    nt_dims = (((1,), (1,)), ((), ()))
    y_parts = []
    for g in range(SSD_N_GROUPS):
        bg = bm[:, g * SSD_D_STATE:(g + 1) * SSD_D_STATE]
        cg = cm[:, g * SSD_D_STATE:(g + 1) * SSD_D_STATE]
        lanes = slice(g * gw, (g + 1) * gw)
        cb = lax.dot_general(cg, bg, nt_dims, preferred_element_type=F32)
        y_diag = []
        for r in range(hpg):
            h = g * hpg + r
            seg = jnp.where(causal, acs[:, h:h + 1] - acs_t[h:h + 1, :], NEG_INF)
            w = (cb * jnp.exp(seg)).astype(BF16)
            y_diag.append(jnp.dot(w, x_dt_b[:, h * SSD_HEAD_DIM:(h + 1) * SSD_HEAD_DIM],
                                  preferred_element_type=F32))
        st_prev = st_ref[g]
        y_off = jnp.dot(cg, st_prev.astype(BF16), preferred_element_type=F32) * in_decay_x[:, lanes]
        y_parts.append(jnp.concatenate(y_diag, axis=-1) + y_off)
        st_new = lax.dot_general(bg, x_out_b[:, lanes], tn_dims, preferred_element_type=F32)
        st_ref[g] = st_prev * chunk_decay_x[:, lanes] + st_new

    y = jnp.concatenate(y_parts, axis=-1) + xs * dsk_ref[...]
    zv = z_ref[0]
    y = y * (zv * (1.0 / (1.0 + jnp.exp(-zv))))
    ng = SSD_D_INNER // SSD_N_GROUPS
    outs = []
    for g in range(SSD_N_GROUPS):
        yg = y[:, g * ng:(g + 1) * ng]
        outs.append(yg * lax.rsqrt(jnp.mean(yg * yg, axis=-1, keepdims=True) + EPS))
    o_ref[0] = (jnp.concatenate(outs, axis=-1) * nw_ref[...]).astype(BF16)


def _ssd(z3, xbc3, dtw3, conv_w, conv_b, dt_bias, a_log, d_skip, norm_w):
    bsz, seq_len, _ = z3.shape
    nc = seq_len // TILE
    blk = lambda w: pl.BlockSpec((1, TILE, w), lambda b, c: (b, c, 0))
    par = lambda r, w: pl.BlockSpec((r, w), lambda b, c: (0, 0))
    return pl.pallas_call(
        _ssd_kernel,
        grid=(bsz, nc),
        in_specs=[blk(SSD_D_INNER), blk(CONV_DIM), blk(LANES),
                  par(CONV_WIDTH, CONV_DIM), par(1, CONV_DIM), par(1, LANES), par(1, LANES),
                  par(1, SSD_D_INNER), par(1, SSD_D_INNER)],
        out_specs=blk(SSD_D_INNER),
        out_shape=jax.ShapeDtypeStruct((bsz, seq_len, SSD_D_INNER), BF16),
        scratch_shapes=[
            pltpu.VMEM((SUBLANES + TILE, CONV_DIM), F32),
            pltpu.VMEM((SSD_N_GROUPS, SSD_D_STATE, SSD_D_INNER // SSD_N_GROUPS), F32),
        ],
        compiler_params=pltpu.CompilerParams(
            dimension_semantics=("arbitrary", "arbitrary"), vmem_limit_bytes=VMEM_LIMIT),
        name="ssd_mixer",
    )(z3, xbc3, dtw3, conv_w, conv_b, dt_bias, a_log, d_skip, norm_w)


FF_CHUNK = 512


def _mlp_kernel(x_ref, attn_ref, ssd_ref, wo_ref, g1_ref, g2_ref, wu_ref, wd_ref, g3_ref, o_ref):
    mix = (jnp.dot(attn_ref[...], wo_ref[:ATTN_WIDTH, :], preferred_element_type=F32)
           + jnp.dot(ssd_ref[...], wo_ref[ATTN_WIDTH:, :], preferred_element_type=F32))
    h1 = x_ref[...] + _rms(mix, g1_ref[...])
    u = _rms(h1, g2_ref[...]).astype(BF16)
    acc = jnp.zeros(h1.shape, F32)
    for c in range(0, D_FF, FF_CHUNK):
        f = jnp.dot(u, wu_ref[:, c:c + FF_CHUNK], preferred_element_type=F32)
        f = jnp.square(jnp.maximum(f, 0.0)).astype(BF16)
        acc = acc + jnp.dot(f, wd_ref[c:c + FF_CHUNK, :], preferred_element_type=F32)
    o_ref[...] = h1 + _rms(acc, g3_ref[...])


def _mlp(x2, attn2, ssd2, wo, g1, g2, wu, wd, g3, tm):
    n = x2.shape[0]
    row = lambda i: (i, 0)
    const = lambda i: (0, 0)
    single = dict(pipeline_mode=pl.Buffered(1))
    return pl.pallas_call(
        _mlp_kernel,
        grid=(n // tm,),
        in_specs=[
            pl.BlockSpec((tm, D_MODEL), row),
            pl.BlockSpec((tm, ATTN_WIDTH), row),
            pl.BlockSpec((tm, SSD_D_INNER), row),
            pl.BlockSpec((ATTN_WIDTH + SSD_D_INNER, D_MODEL), const, **single),
            pl.BlockSpec((1, D_MODEL), const),
            pl.BlockSpec((1, D_MODEL), const),
            pl.BlockSpec((D_MODEL, D_FF), const, **single),
            pl.BlockSpec((D_FF, D_MODEL), const, **single),
            pl.BlockSpec((1, D_MODEL), const),
        ],
        out_specs=pl.BlockSpec((tm, D_MODEL), row),
        out_shape=jax.ShapeDtypeStruct((n, D_MODEL), F32),
        compiler_params=pltpu.CompilerParams(
            dimension_semantics=("arbitrary",), vmem_limit_bytes=VMEM_LIMIT),
        name="out_proj_mlp",
    )(x2, attn2, ssd2, wo, g1, g2, wu, wd, g3)


def _pack_w_in(w):
    o = 0
    parts = {}
    for name, width in (("qkvqi", QKV_W), ("ki", IDX_DIM), ("wi", N_IDX_HEADS),
                        ("z", SSD_D_INNER), ("xbc", CONV_DIM), ("dt", SSD_N_HEADS)):
        parts[name] = w[:, o:o + width]
        o += width
    zeros = lambda n: jnp.zeros((w.shape[0], n), w.dtype)
    return jnp.concatenate([
        parts["qkvqi"], parts["ki"], zeros(LANES - IDX_DIM), parts["z"], parts["xbc"],
        parts["dt"], parts["wi"], zeros(LANES - SSD_N_HEADS - N_IDX_HEADS)], axis=1).astype(BF16)


def _pad_lanes(v, n):
    return jnp.pad(v, (0, n - v.shape[0])).reshape(1, n)


def kernel(x, norm_pre_mix, norm_post_mix, norm_pre_mlp, norm_post_mlp, w_in, k_idx_ln_w, k_idx_ln_b, conv_w, conv_b, dt_bias, a_log, d_skip, ssd_norm_w, w_out, w_mlp_up, w_mlp_down, rel_bias):
    bsz, seq_len, d = x.shape
    n = bsz * seq_len
    assert d == D_MODEL and seq_len % TILE == 0
    tm = 512 if n % 512 == 0 else TILE
    n_sel = min(TOPK_MAX, seq_len // 4)
    bias = _bias_tiles(rel_bias)
    h = x.reshape(n, d)
    for i in range(norm_pre_mix.shape[0]):
        row = lambda v: v[i].reshape(1, -1)
        qkv, ki, z, xbc, dtw = _in_proj(h, row(norm_pre_mix), _pack_w_in(w_in[i]),
                                        row(k_idx_ln_w), row(k_idx_ln_b), tm)
        r3 = lambda a: a.reshape(bsz, seq_len, a.shape[-1])
        dtw3 = r3(dtw)
        attn = _attention(r3(qkv), r3(ki), dtw3, bias, n_sel)
        ssd = _ssd(r3(z), r3(xbc), dtw3, conv_w[i], row(conv_b),
                   _pad_lanes(dt_bias[i], LANES), _pad_lanes(a_log[i], LANES),
                   jnp.repeat(d_skip[i], SSD_HEAD_DIM).reshape(1, -1), row(ssd_norm_w))
        h = _mlp(h, attn.reshape(n, -1), ssd.reshape(n, -1), w_out[i].astype(BF16),
                 row(norm_post_mix), row(norm_pre_mlp), w_mlp_up[i].astype(BF16),
                 w_mlp_down[i].astype(BF16), row(norm_post_mlp), tm)
    return h.reshape(bsz, seq_len, d)
```

```python
import functools
import math

import jax
import jax.numpy as jnp
from jax import lax
from jax.experimental import pallas as pl
from jax.experimental.pallas import tpu as pltpu

F32 = jnp.float32
BF16 = jnp.bfloat16
I32 = jnp.int32

D_MODEL = 1024
N_ATTN_HEADS = 8
N_KV_HEADS = 2
HEAD_DIM = 64
ATTN_WIDTH = N_ATTN_HEADS * HEAD_DIM
N_IDX_HEADS = 4
IDX_DIM = 64
TOPK_MAX = 256
NUM_BUCKETS = 32
MAX_DISTANCE = 128
SSD_D_INNER = 512
SSD_HEAD_DIM = 64
SSD_N_HEADS = 8
SSD_N_GROUPS = 2
SSD_D_STATE = 128
CONV_WIDTH = 4
CONV_DIM = SSD_D_INNER + 2 * SSD_N_GROUPS * SSD_D_STATE
D_FF = 4 * D_MODEL
EPS = 1e-6

LANES = 128
SUBLANES = 8
TILE = 128
VMEM_LIMIT = 56 * 1024 * 1024

QKV_W = ATTN_WIDTH + 2 * N_KV_HEADS * HEAD_DIM + N_IDX_HEADS * IDX_DIM
COL_QKV = 0
COL_KI = COL_QKV + QKV_W
COL_Z = COL_KI + LANES
COL_XBC = COL_Z + SSD_D_INNER
COL_DTW = COL_XBC + CONV_DIM
W_CAT = COL_DTW + LANES
DTW_WI = SSD_N_HEADS

NEG_INF = float("-inf")
LOG2E = 1.4426950408889634
Q_SCALE = LOG2E * HEAD_DIM ** -0.5


def _rms(x, g):
    return x * lax.rsqrt(jnp.mean(x * x, axis=-1, keepdims=True) + EPS) * g


def _in_proj_kernel(x_ref, g_ref, w_ref, lnw_ref, lnb_ref,
                    qkv_ref, ki_ref, z_ref, xbc_ref, dtw_ref):
    u = _rms(x_ref[...], g_ref[...]).astype(BF16)

    def mm(lo, hi):
        return jnp.dot(u, w_ref[:, lo:hi], preferred_element_type=F32)

    qkv_ref[:, :ATTN_WIDTH] = (mm(COL_QKV, COL_QKV + ATTN_WIDTH) * Q_SCALE).astype(BF16)
    qkv_ref[:, ATTN_WIDTH:] = mm(COL_QKV + ATTN_WIDTH, COL_KI).astype(BF16)
    ki = mm(COL_KI, COL_Z)[:, :IDX_DIM]
    mu = jnp.mean(ki, axis=-1, keepdims=True)
    var = jnp.mean(jnp.square(ki - mu), axis=-1, keepdims=True)
    ki_ref[...] = ((ki - mu) * lax.rsqrt(var + EPS) * lnw_ref[...] + lnb_ref[...]).astype(BF16)
    z_ref[...] = mm(COL_Z, COL_XBC)
    xbc_ref[...] = mm(COL_XBC, COL_DTW)
    dtw_ref[...] = mm(COL_DTW, W_CAT)


def _in_proj(x2, g, w_cat, lnw, lnb, tm):
    n = x2.shape[0]
    row = lambda i: (i, 0)
    const = lambda i: (0, 0)
    return pl.pallas_call(
        _in_proj_kernel,
        grid=(n // tm,),
        in_specs=[
            pl.BlockSpec((tm, D_MODEL), row),
            pl.BlockSpec((1, D_MODEL), const),
            pl.BlockSpec((D_MODEL, W_CAT), const),
            pl.BlockSpec((1, IDX_DIM), const),
            pl.BlockSpec((1, IDX_DIM), const),
        ],
        out_specs=[
            pl.BlockSpec((tm, QKV_W), row),
            pl.BlockSpec((tm, IDX_DIM), row),
            pl.BlockSpec((tm, SSD_D_INNER), row),
            pl.BlockSpec((tm, CONV_DIM), row),
            pl.BlockSpec((tm, LANES), row),
        ],
        out_shape=[
            jax.ShapeDtypeStruct((n, QKV_W), BF16),
            jax.ShapeDtypeStruct((n, IDX_DIM), BF16),
            jax.ShapeDtypeStruct((n, SSD_D_INNER), F32),
            jax.ShapeDtypeStruct((n, CONV_DIM), F32),
            jax.ShapeDtypeStruct((n, LANES), F32),
        ],
        compiler_params=pltpu.CompilerParams(
            dimension_semantics=("arbitrary",), vmem_limit_bytes=VMEM_LIMIT),
        name="in_proj",
    )(x2, g, w_cat, lnw, lnb)


def _bias_kernel(tbl_ref, out_ref):
    sk = lax.broadcasted_iota(I32, (TILE, TILE), 0)
    tq = lax.broadcasted_iota(I32, (TILE, TILE), 1)
    max_exact = NUM_BUCKETS // 2
    for off in range(2):
        dist = jnp.maximum(off * TILE + tq - sk, 0)
        df = jnp.maximum(dist, 1).astype(F32)
        large = max_exact + (jnp.log(df / max_exact) / math.log(MAX_DISTANCE / max_exact)
                             * (NUM_BUCKETS - max_exact)).astype(I32)
        large = jnp.minimum(large, NUM_BUCKETS - 1)
        bucket = jnp.where(dist < max_exact, dist, large)
        for h in range(N_ATTN_HEADS):
            acc = jnp.zeros((TILE, TILE), F32)
            for b in range(NUM_BUCKETS):
                acc = jnp.where(bucket == b, tbl_ref[b, h], acc)
            out_ref[h, off] = (acc - tbl_ref[NUM_BUCKETS - 1, h]) * LOG2E
    for h in range(N_ATTN_HEADS):
        out_ref[h, 2] = jnp.zeros((TILE, TILE), F32)


def _bias_tiles(rel_bias):
    return pl.pallas_call(
        _bias_kernel,
        in_specs=[pl.BlockSpec(memory_space=pltpu.SMEM)],
        out_specs=pl.BlockSpec(memory_space=pltpu.VMEM),
        out_shape=jax.ShapeDtypeStruct((N_ATTN_HEADS, 3, TILE, TILE), F32),
        name="bias_tiles",
    )(rel_bias)


def _key_to_f32(u):
    ks = u ^ I32(-2 ** 31)
    bits = jnp.where(ks >= 0, ks, ks ^ I32(0x7FFFFFFF))
    return lax.bitcast_convert_type(bits, F32)


PAIR = 2 * TILE
QUAD = 4 * TILE
QT = 2 * TILE
COARSE_BITS = 16
PV_ROWS = HEAD_DIM + 16


def _trunc_bf16(x):
    return lax.bitcast_convert_type(lax.bitcast_convert_type(x, I32) & I32(-65536), F32)


def _attn_kernel(q_ref, kv_ref, qi_ref, ki_ref, dtw_ref, bias_ref, o_ref,
                 sc_ref, sh_ref, rk_ref, vt_ref, lg_ref, *, seq_len, n_sel):
    j = pl.program_id(1)
    i_hi = j * (QT // TILE) + QT // TILE - 1
    n_quads = seq_len // QUAD
    nq4 = i_hi // 4 + 1
    npair = i_hi // 2 + 1
    rep = N_ATTN_HEADS // N_KV_HEADS
    gl = rep * QT
    nt_dims = (((1,), (1,)), ((), ()))
    v_lo = N_KV_HEADS * HEAD_DIM

    @pl.when(j == 0)
    def _():
        vt = kv_ref[0, :, v_lo:].astype(F32).T
        ones_row = jnp.where(lax.broadcasted_iota(I32, (PV_ROWS - HEAD_DIM, seq_len), 0) == 0, 1.0, 0.0)
        for g in range(N_KV_HEADS):
            vt_ref[g * PV_ROWS:g * PV_ROWS + HEAD_DIM, :] = vt[g * HEAD_DIM:(g + 1) * HEAD_DIM].astype(BF16)
            vt_ref[g * PV_ROWS + HEAD_DIM:(g + 1) * PV_ROWS, :] = ones_row.astype(BF16)

    w_t = dtw_ref[0].T
    idx_scale = (N_IDX_HEADS ** -0.5) * (IDX_DIM ** -0.5)
    w_rows = [w_t[DTW_WI + h:DTW_WI + h + 1, :] * idx_scale for h in range(N_IDX_HEADS)]
    qi = qi_ref[0]
    qi_all = jnp.concatenate(
        [qi[:, h * IDX_DIM:(h + 1) * IDX_DIM] for h in range(N_IDX_HEADS)], axis=0)
    s_loc = lax.broadcasted_iota(I32, (PAIR, QT), 0)
    s_minus_t = s_loc - lax.broadcasted_iota(I32, (PAIR, QT), 1)

    def score_quads(v):
        for c in range(v):
            kt = ki_ref[0, c * QUAD:(c + 1) * QUAD, :]
            d = lax.dot_general(kt, qi_all, nt_dims, preferred_element_type=F32)
            s = jnp.zeros((QUAD, QT), F32)
            for h in range(N_IDX_HEADS):
                s = s + w_rows[h] * jnp.maximum(d[:, h * QT:(h + 1) * QT], 0.0)
            for u in range(QUAD // PAIR):
                pr = c * (QUAD // PAIR) + u
                su = jnp.where(s_minus_t <= j * QT - pr * PAIR, s[u * PAIR:(u + 1) * PAIR], NEG_INF)
                sc_ref[pr] = su
                sh_ref[pr] = _trunc_bf16(su).astype(BF16)

    for v in range(1, n_quads + 1):
        pl.when(nq4 == v)(functools.partial(score_quads, v))

    t_glob = j * QT + lax.broadcasted_iota(I32, (1, QT), 1)
    k_eff = jnp.minimum(n_sel, t_glob + 1).astype(F32)
    acc_rows = 4 * SUBLANES
    acc_rows_b = 4 * 2 * SUBLANES

    def count_pairs(pred):
        def body(c, acc):
            cnt = jnp.where(pred(c, sc_ref[c]), 1.0, 0.0)
            return acc + cnt.reshape(PAIR // acc_rows, acc_rows, QT).sum(axis=0)
        acc = lax.fori_loop(0, npair, body, jnp.zeros((acc_rows, QT), F32))
        return acc.sum(axis=0, keepdims=True)

    def count_pairs_coarse(thr_b):
        one, zero = jnp.ones((), BF16), jnp.zeros((), BF16)

        def body(c, acc):
            cnt = jnp.where(sh_ref[c] >= thr_b, one, zero)
            parts = [cnt[k * acc_rows_b:(k + 1) * acc_rows_b] for k in range(PAIR // acc_rows_b)]
            while len(parts) > 1:
                parts = [a + b for a, b in zip(parts[::2], parts[1::2])]
            return acc + parts[0]
        acc = lax.fori_loop(0, npair, body, jnp.zeros((acc_rows_b, QT), BF16))
        return acc.astype(F32).sum(axis=0, keepdims=True)

    def search_step(b, carry, coarse):
        prefix, cnt_ge = carry
        cand = prefix | lax.shift_left(I32(1), 31 - b)
        thr = _key_to_f32(cand)
        if coarse:
            cnt = count_pairs_coarse(_trunc_bf16(thr).astype(BF16))
        else:
            cnt = count_pairs(lambda c, s: s >= thr)
        ok = cnt >= k_eff
        return jnp.where(ok, cand, prefix), jnp.where(ok, cnt, cnt_ge)

    carry = (jnp.zeros((1, QT), I32), (t_glob + 1).astype(F32))
    carry = lax.fori_loop(0, COARSE_BITS, functools.partial(search_step, coarse=True), carry)
    prefix, cnt_ge = lax.fori_loop(COARSE_BITS, 32, functools.partial(search_step, coarse=False), carry)
    thr = _key_to_f32(prefix)

    q = q_ref[0]
    qgs = [jnp.concatenate(
        [q[:, (g * rep + r) * HEAD_DIM:(g * rep + r + 1) * HEAD_DIM] for r in range(rep)],
        axis=0) for g in range(N_KV_HEADS)]
    below_diag = (lax.broadcasted_iota(I32, (PAIR, PAIR), 0)
                  > lax.broadcasted_iota(I32, (PAIR, PAIR), 1)).astype(BF16)
    tiles_per_quad = QUAD // TILE
    pairs_per_quad = QUAD // PAIR
    n_qt = QT // TILE
    int_min = I32(-2 ** 31)

    def bias_rows(c, g, u):
        cols = []
        for r in range(rep):
            for w in range(n_qt):
                off = jnp.clip(j * n_qt + w - (c * tiles_per_quad + u), 0, 2)
                cols.append(bias_ref[g * rep + r, off])
        return jnp.concatenate(cols, axis=1)

    def tied(s):
        eq = jnp.where(s == thr, 1.0, 0.0)
        neg0 = eq * jnp.where(lax.bitcast_convert_type(s, I32) == int_min, 1.0, 0.0)
        return eq - neg0, neg0

    def attend(v):
        tots = []
        for pr in range(v * pairs_per_quad):
            e2 = jnp.concatenate(tied(sc_ref[pr]), axis=1).astype(BF16)
            r = jnp.dot(below_diag, e2, preferred_element_type=F32)
            rk_ref[pr] = r
            tots.append(r[PAIR - 1:PAIR] + e2[PAIR - 1:PAIR].astype(F32))
        tot = tots[0]
        for t in tots[1:]:
            tot = tot + t
        need = k_eff - (cnt_ge - tot[:, :QT] - tot[:, QT:])
        off = jnp.concatenate([jnp.zeros((1, QT), F32), tot[:, :QT]], axis=1)

        m8 = [jnp.full((SUBLANES, gl), NEG_INF, F32) for _ in range(N_KV_HEADS)]
        for c in range(v):
            masks = []
            for u in range(pairs_per_quad):
                pr = c * pairs_per_quad + u
                s = sc_ref[pr]
                r = rk_ref[pr] + off
                off = off + tots[pr]
                neg0 = lax.bitcast_convert_type(s, I32) == int_min
                keep_tie = jnp.where(jnp.where(neg0, r[:, QT:], r[:, :QT]) < need, 0.0, NEG_INF)
                masks.append(jnp.where(s > thr, 0.0, jnp.where(s == thr, keep_tie, NEG_INF)))
            mask = jnp.concatenate(masks, axis=0)
            mask = jnp.concatenate([mask] * rep, axis=1)
            keys = slice(c * QUAD, (c + 1) * QUAD)
            for g in range(N_KV_HEADS):
                kt = kv_ref[0, keys, g * HEAD_DIM:(g + 1) * HEAD_DIM]
                lg = lax.dot_general(kt, qgs[g], nt_dims, preferred_element_type=F32) + mask
                if c == v - 1:
                    lg = lg + jnp.concatenate([bias_rows(c, g, u) for u in range(tiles_per_quad)], axis=0)
                elif c == v - 2:
                    u = tiles_per_quad - 1
                    lg = jnp.concatenate([lg[:u * TILE], lg[u * TILE:] + bias_rows(c, g, u)], axis=0)
                lg_ref[c, g] = lg
                m8[g] = jnp.maximum(m8[g], lg.reshape(QUAD // SUBLANES, SUBLANES, gl).max(axis=0))

        outs = []
        for g in range(N_KV_HEADS):
            m_row = jnp.max(m8[g], axis=0, keepdims=True)
            acc = jnp.zeros((PV_ROWS, gl), F32)
            for c in range(v):
                p = jnp.exp2((lg_ref[c, g] - m_row).astype(BF16))
                vt = vt_ref[g * PV_ROWS:(g + 1) * PV_ROWS, c * QUAD:(c + 1) * QUAD]
                acc = acc + jnp.dot(vt, p, preferred_element_type=F32)
            o_g = acc[:HEAD_DIM] / acc[HEAD_DIM:HEAD_DIM + 1]
            outs += [o_g[:, r * QT:(r + 1) * QT] for r in range(rep)]
        o_ref[0] = jnp.concatenate(outs, axis=0).T.astype(BF16)

    for v in range(1, n_quads + 1):
        pl.when(nq4 == v)(functools.partial(attend, v))


def _attention(qkv3, ki3, dtw3, bias, n_sel):
    bsz, seq_len, _ = qkv3.shape
    assert seq_len % QUAD == 0 and QUAD % QT == 0
    rep = N_ATTN_HEADS // N_KV_HEADS
    kv_w = 2 * N_KV_HEADS * HEAD_DIM
    qi_w = N_IDX_HEADS * IDX_DIM
    kern = functools.partial(_attn_kernel, seq_len=seq_len, n_sel=n_sel)
    return pl.pallas_call(
        kern,
        grid=(bsz, seq_len // QT),
        in_specs=[
            pl.BlockSpec((1, QT, ATTN_WIDTH), lambda b, j: (b, j, 0)),
            pl.BlockSpec((1, seq_len, kv_w), lambda b, j: (b, 0, ATTN_WIDTH // kv_w)),
            pl.BlockSpec((1, QT, qi_w), lambda b, j: (b, j, (ATTN_WIDTH + kv_w) // qi_w)),
            pl.BlockSpec((1, seq_len, IDX_DIM), lambda b, j: (b, 0, 0)),
            pl.BlockSpec((1, QT, LANES), lambda b, j: (b, j, 0)),
            pl.BlockSpec((N_ATTN_HEADS, 3, TILE, TILE), lambda b, j: (0, 0, 0, 0)),
        ],
        out_specs=pl.BlockSpec((1, QT, ATTN_WIDTH), lambda b, j: (b, j, 0)),
        out_shape=jax.ShapeDtypeStruct((bsz, seq_len, ATTN_WIDTH), BF16),
        scratch_shapes=[
            pltpu.VMEM((seq_len // PAIR, PAIR, QT), F32),
            pltpu.VMEM((seq_len // PAIR, PAIR, QT), BF16),
            pltpu.VMEM((seq_len // PAIR, PAIR, 2 * QT), F32),
            pltpu.VMEM((N_KV_HEADS * PV_ROWS, seq_len), BF16),
            pltpu.VMEM((seq_len // QUAD, N_KV_HEADS, QUAD, rep * QT), F32),
        ],
        compiler_params=pltpu.CompilerParams(
            dimension_semantics=("arbitrary", "arbitrary"), vmem_limit_bytes=VMEM_LIMIT),
        name="sparse_attn",
    )(qkv3, qkv3, qkv3, ki3, dtw3, bias)


def _ssd_kernel(z_ref, xbc_ref, dtw_ref, cw_ref, cb_ref, dtb_ref, alog_ref, dsk_ref, nw_ref,
                o_ref, xpad_ref, st_ref):
    c = pl.program_id(1)
    hp = lax.Precision.HIGHEST
    gn = SSD_N_GROUPS * SSD_D_STATE
    hpg = SSD_N_HEADS // SSD_N_GROUPS
    gw = hpg * SSD_HEAD_DIM

    @pl.when(c == 0)
    def _():
        xpad_ref[0:SUBLANES, :] = jnp.zeros((SUBLANES, CONV_DIM), F32)
        st_ref[...] = jnp.zeros_like(st_ref)

    xpad_ref[SUBLANES:SUBLANES + TILE, :] = xbc_ref[0]
    conv = cb_ref[...]
    for k in range(CONV_WIDTH):
        lo = SUBLANES - (CONV_WIDTH - 1) + k
        conv = conv + cw_ref[k:k + 1, :] * xpad_ref[lo:lo + TILE, :]
    xpad_ref[0:SUBLANES, :] = xpad_ref[TILE:TILE + SUBLANES, :]
    act = conv * (1.0 / (1.0 + jnp.exp(-conv)))
    xs = act[:, :SSD_D_INNER]
    bm = act[:, SSD_D_INNER:SSD_D_INNER + gn].astype(BF16)
    cm = act[:, SSD_D_INNER + gn:].astype(BF16)

    dt_in = dtw_ref[0] + dtb_ref[...]
    dt = jnp.maximum(dt_in, 0.0) + jnp.log1p(jnp.exp(-jnp.abs(dt_in)))
    adt = dt * (-jnp.exp(alog_ref[...]))
    row = lax.broadcasted_iota(I32, (TILE, TILE), 0)
    col = lax.broadcasted_iota(I32, (TILE, TILE), 1)
    causal = row >= col
    acs = jnp.dot(causal.astype(F32), adt, precision=hp, preferred_element_type=F32)
    acs_t = acs.T
    a_last = acs[TILE - 1:TILE, :]
    hsel = (lax.broadcasted_iota(I32, (LANES, SSD_D_INNER), 0)
            == lax.broadcasted_iota(I32, (LANES, SSD_D_INNER), 1) // SSD_HEAD_DIM).astype(F32)
    expand = lambda v: jnp.dot(v, hsel, precision=hp, preferred_element_type=F32)
    dt_x = expand(dt)
    in_decay_x = expand(jnp.exp(acs))
    out_decay_x = expand(jnp.exp(a_last - acs))
    chunk_decay_x = expand(jnp.exp(a_last))

    x_dt = xs * dt_x
    x_dt_b = x_dt.astype(BF16)
    x_out_b = (x_dt * out_decay_x).astype(BF16)
    tn_dims = (((0,), (0,)), ((), ()))
    nt_dims = (((1,), (1,)), ((), ()))
    y_parts = []
    for g in range(SSD_N_GROUPS):
        bg = bm[:, g * SSD_D_STATE:(g + 1) * SSD_D_STATE]
        cg = cm[:, g * SSD_D_STATE:(g + 1) * SSD_D_STATE]
        lanes = slice(g * gw, (g + 1) * gw)
        cb = lax.dot_general(cg, bg, nt_dims, preferred_element_type=F32)
        y_diag = []
        for r in range(hpg):
            h = g * hpg + r
            seg = jnp.where(causal, acs[:, h:h + 1] - acs_t[h:h + 1, :], NEG_INF)
            w = (cb * jnp.exp(seg)).astype(BF16)
            y_diag.append(jnp.dot(w, x_dt_b[:, h * SSD_HEAD_DIM:(h + 1) * SSD_HEAD_DIM],
                                  preferred_element_type=F32))
        st_prev = st_ref[g]
        y_off = jnp.dot(cg, st_prev.astype(BF16), preferred_element_type=F32) * in_decay_x[:, lanes]
        y_parts.append(jnp.concatenate(y_diag, axis=-1) + y_off)
        st_new = lax.dot_general(bg, x_out_b[:, lanes], tn_dims, preferred_element_type=F32)
        st_ref[g] = st_prev * chunk_decay_x[:, lanes] + st_new

    y = jnp.concatenate(y_parts, axis=-1) + xs * dsk_ref[...]
    zv = z_ref[0]
    y = y * (zv * (1.0 / (1.0 + jnp.exp(-zv))))
    ng = SSD_D_INNER // SSD_N_GROUPS
    outs = []
    for g in range(SSD_N_GROUPS):
        yg = y[:, g * ng:(g + 1) * ng]
        outs.append(yg * lax.rsqrt(jnp.mean(yg * yg, axis=-1, keepdims=True) + EPS))
    o_ref[0] = (jnp.concatenate(outs, axis=-1) * nw_ref[...]).astype(BF16)


def _ssd(z3, xbc3, dtw3, conv_w, conv_b, dt_bias, a_log, d_skip, norm_w):
    bsz, seq_len, _ = z3.shape
    nc = seq_len // TILE
    blk = lambda w: pl.BlockSpec((1, TILE, w), lambda b, c: (b, c, 0))
    par = lambda r, w: pl.BlockSpec((r, w), lambda b, c: (0, 0))
    return pl.pallas_call(
        _ssd_kernel,
        grid=(bsz, nc),
        in_specs=[blk(SSD_D_INNER), blk(CONV_DIM), blk(LANES),
                  par(CONV_WIDTH, CONV_DIM), par(1, CONV_DIM), par(1, LANES), par(1, LANES),
                  par(1, SSD_D_INNER), par(1, SSD_D_INNER)],
        out_specs=blk(SSD_D_INNER),
        out_shape=jax.ShapeDtypeStruct((bsz, seq_len, SSD_D_INNER), BF16),
        scratch_shapes=[
            pltpu.VMEM((SUBLANES + TILE, CONV_DIM), F32),
            pltpu.VMEM((SSD_N_GROUPS, SSD_D_STATE, SSD_D_INNER // SSD_N_GROUPS), F32),
        ],
        compiler_params=pltpu.CompilerParams(
            dimension_semantics=("arbitrary", "arbitrary"), vmem_limit_bytes=VMEM_LIMIT),
        name="ssd_mixer",
    )(z3, xbc3, dtw3, conv_w, conv_b, dt_bias, a_log, d_skip, norm_w)


FF_CHUNK = 512


def _mlp_kernel(x_ref, attn_ref, ssd_ref, wo_ref, g1_ref, g2_ref, wu_ref, wd_ref, g3_ref, o_ref):
    mix = (jnp.dot(attn_ref[...], wo_ref[:ATTN_WIDTH, :], preferred_element_type=F32)
           + jnp.dot(ssd_ref[...], wo_ref[ATTN_WIDTH:, :], preferred_element_type=F32))
    h1 = x_ref[...] + _rms(mix, g1_ref[...])
    u = _rms(h1, g2_ref[...]).astype(BF16)
    acc = jnp.zeros(h1.shape, F32)
    for c in range(0, D_FF, FF_CHUNK):
        f = jnp.dot(u, wu_ref[:, c:c + FF_CHUNK], preferred_element_type=F32)
        f = jnp.square(jnp.maximum(f, 0.0)).astype(BF16)
        acc = acc + jnp.dot(f, wd_ref[c:c + FF_CHUNK, :], preferred_element_type=F32)
    o_ref[...] = h1 + _rms(acc, g3_ref[...])


def _mlp(x2, attn2, ssd2, wo, g1, g2, wu, wd, g3, tm):
    n = x2.shape[0]
    row = lambda i: (i, 0)
    const = lambda i: (0, 0)
    single = dict(pipeline_mode=pl.Buffered(1))
    return pl.pallas_call(
        _mlp_kernel,
        grid=(n // tm,),
        in_specs=[
            pl.BlockSpec((tm, D_MODEL), row),
            pl.BlockSpec((tm, ATTN_WIDTH), row),
            pl.BlockSpec((tm, SSD_D_INNER), row),
            pl.BlockSpec((ATTN_WIDTH + SSD_D_INNER, D_MODEL), const, **single),
            pl.BlockSpec((1, D_MODEL), const),
            pl.BlockSpec((1, D_MODEL), const),
            pl.BlockSpec((D_MODEL, D_FF), const, **single),
            pl.BlockSpec((D_FF, D_MODEL), const, **single),
            pl.BlockSpec((1, D_MODEL), const),
        ],
        out_specs=pl.BlockSpec((tm, D_MODEL), row),
        out_shape=jax.ShapeDtypeStruct((n, D_MODEL), F32),
        compiler_params=pltpu.CompilerParams(
            dimension_semantics=("arbitrary",), vmem_limit_bytes=VMEM_LIMIT),
        name="out_proj_mlp",
    )(x2, attn2, ssd2, wo, g1, g2, wu, wd, g3)


def _pack_w_in(w):
    o = 0
    parts = {}
    for name, width in (("qkvqi", QKV_W), ("ki", IDX_DIM), ("wi", N_IDX_HEADS),
                        ("z", SSD_D_INNER), ("xbc", CONV_DIM), ("dt", SSD_N_HEADS)):
        parts[name] = w[:, o:o + width]
        o += width
    zeros = lambda n: jnp.zeros((w.shape[0], n), w.dtype)
    return jnp.concatenate([
        parts["qkvqi"], parts["ki"], zeros(LANES - IDX_DIM), parts["z"], parts["xbc"],
        parts["dt"], parts["wi"], zeros(LANES - SSD_N_HEADS - N_IDX_HEADS)], axis=1).astype(BF16)


def _pad_lanes(v, n):
    return jnp.pad(v, (0, n - v.shape[0])).reshape(1, n)


def kernel(x, norm_pre_mix, norm_post_mix, norm_pre_mlp, norm_post_mlp, w_in, k_idx_ln_w, k_idx_ln_b, conv_w, conv_b, dt_bias, a_log, d_skip, ssd_norm_w, w_out, w_mlp_up, w_mlp_down, rel_bias):
    bsz, seq_len, d = x.shape
    n = bsz * seq_len
    assert d == D_MODEL and seq_len % TILE == 0
    tm = 512 if n % 512 == 0 else TILE
    n_sel = min(TOPK_MAX, seq_len // 4)
    bias = _bias_tiles(rel_bias)
    h = x.reshape(n, d)
    for i in range(norm_pre_mix.shape[0]):
        row = lambda v: v[i].reshape(1, -1)
        qkv, ki, z, xbc, dtw = _in_proj(h, row(norm_pre_mix), _pack_w_in(w_in[i]),
                                        row(k_idx_ln_w), row(k_idx_ln_b), tm)
        r3 = lambda a: a.reshape(bsz, seq_len, a.shape[-1])
        dtw3 = r3(dtw)
        attn = _attention(r3(qkv), r3(ki), dtw3, bias, n_sel)
        ssd = _ssd(r3(z), r3(xbc), dtw3, conv_w[i], row(conv_b),
                   _pad_lanes(dt_bias[i], LANES), _pad_lanes(a_log[i], LANES),
                   jnp.repeat(d_skip[i], SSD_HEAD_DIM).reshape(1, -1), row(ssd_norm_w))
        h = _mlp(h, attn.reshape(n, -1), ssd.reshape(n, -1), w_out[i].astype(BF16),
                 row(norm_post_mix), row(norm_pre_mlp), w_mlp_up[i].astype(BF16),
                 w_mlp_down[i].astype(BF16), row(norm_post_mlp), tm)
    return h.reshape(bsz, seq_len, d)
```

```python
import functools
import math

import jax
import jax.numpy as jnp
from jax import lax
from jax.experimental import pallas as pl
from jax.experimental.pallas import tpu as pltpu

F32 = jnp.float32
BF16 = jnp.bfloat16
I32 = jnp.int32

D_MODEL = 1024
N_ATTN_HEADS = 8
N_KV_HEADS = 2
HEAD_DIM = 64
ATTN_WIDTH = N_ATTN_HEADS * HEAD_DIM
N_IDX_HEADS = 4
IDX_DIM = 64
TOPK_MAX = 256
NUM_BUCKETS = 32
MAX_DISTANCE = 128
SSD_D_INNER = 512
SSD_HEAD_DIM = 64
SSD_N_HEADS = 8
SSD_N_GROUPS = 2
SSD_D_STATE = 128
CONV_WIDTH = 4
CONV_DIM = SSD_D_INNER + 2 * SSD_N_GROUPS * SSD_D_STATE
D_FF = 4 * D_MODEL
EPS = 1e-6

LANES = 128
SUBLANES = 8
TILE = 128
VMEM_LIMIT = 56 * 1024 * 1024

QKV_W = ATTN_WIDTH + 2 * N_KV_HEADS * HEAD_DIM + N_IDX_HEADS * IDX_DIM
COL_QKV = 0
COL_KI = COL_QKV + QKV_W
COL_Z = COL_KI + LANES
COL_XBC = COL_Z + SSD_D_INNER
COL_DTW = COL_XBC + CONV_DIM
W_CAT = COL_DTW + LANES
DTW_WI = SSD_N_HEADS

NEG_INF = float("-inf")
LOG2E = 1.4426950408889634
Q_SCALE = LOG2E * HEAD_DIM ** -0.5


def _rms(x, g):
    return x * lax.rsqrt(jnp.mean(x * x, axis=-1, keepdims=True) + EPS) * g


def _in_proj_kernel(x_ref, g_ref, w_ref, lnw_ref, lnb_ref,
                    qkv_ref, ki_ref, z_ref, xbc_ref, dtw_ref):
    u = _rms(x_ref[...], g_ref[...]).astype(BF16)

    def mm(lo, hi):
        return jnp.dot(u, w_ref[:, lo:hi], preferred_element_type=F32)

    qkv_ref[:, :ATTN_WIDTH] = (mm(COL_QKV, COL_QKV + ATTN_WIDTH) * Q_SCALE).astype(BF16)
    qkv_ref[:, ATTN_WIDTH:] = mm(COL_QKV + ATTN_WIDTH, COL_KI).astype(BF16)
    ki = mm(COL_KI, COL_Z)[:, :IDX_DIM]
    mu = jnp.mean(ki, axis=-1, keepdims=True)
    var = jnp.mean(jnp.square(ki - mu), axis=-1, keepdims=True)
    ki_ref[...] = ((ki - mu) * lax.rsqrt(var + EPS) * lnw_ref[...] + lnb_ref[...]).astype(BF16)
    z_ref[...] = mm(COL_Z, COL_XBC)
    xbc_ref[...] = mm(COL_XBC, COL_DTW)
    dtw_ref[...] = mm(COL_DTW, W_CAT)


def _in_proj(x2, g, w_cat, lnw, lnb, tm):
    n = x2.shape[0]
    row = lambda i: (i, 0)
    const = lambda i: (0, 0)
    return pl.pallas_call(
        _in_proj_kernel,
        grid=(n // tm,),
        in_specs=[
            pl.BlockSpec((tm, D_MODEL), row),
            pl.BlockSpec((1, D_MODEL), const),
            pl.BlockSpec((D_MODEL, W_CAT), const),
            pl.BlockSpec((1, IDX_DIM), const),
            pl.BlockSpec((1, IDX_DIM), const),
        ],
        out_specs=[
            pl.BlockSpec((tm, QKV_W), row),
            pl.BlockSpec((tm, IDX_DIM), row),
            pl.BlockSpec((tm, SSD_D_INNER), row),
            pl.BlockSpec((tm, CONV_DIM), row),
            pl.BlockSpec((tm, LANES), row),
        ],
        out_shape=[
            jax.ShapeDtypeStruct((n, QKV_W), BF16),
            jax.ShapeDtypeStruct((n, IDX_DIM), BF16),
            jax.ShapeDtypeStruct((n, SSD_D_INNER), F32),
            jax.ShapeDtypeStruct((n, CONV_DIM), F32),
            jax.ShapeDtypeStruct((n, LANES), F32),
        ],
        compiler_params=pltpu.CompilerParams(
            dimension_semantics=("arbitrary",), vmem_limit_bytes=VMEM_LIMIT),
        name="in_proj",
    )(x2, g, w_cat, lnw, lnb)


def _bias_kernel(tbl_ref, out_ref):
    sk = lax.broadcasted_iota(I32, (TILE, TILE), 0)
    tq = lax.broadcasted_iota(I32, (TILE, TILE), 1)
    max_exact = NUM_BUCKETS // 2
    for off in range(2):
        dist = jnp.maximum(off * TILE + tq - sk, 0)
        df = jnp.maximum(dist, 1).astype(F32)
        large = max_exact + (jnp.log(df / max_exact) / math.log(MAX_DISTANCE / max_exact)
                             * (NUM_BUCKETS - max_exact)).astype(I32)
        large = jnp.minimum(large, NUM_BUCKETS - 1)
        bucket = jnp.where(dist < max_exact, dist, large)
        for h in range(N_ATTN_HEADS):
            acc = jnp.zeros((TILE, TILE), F32)
            for b in range(NUM_BUCKETS):
                acc = jnp.where(bucket == b, tbl_ref[b, h], acc)
            out_ref[h, off] = (acc - tbl_ref[NUM_BUCKETS - 1, h]) * LOG2E
    for h in range(N_ATTN_HEADS):
        out_ref[h, 2] = jnp.zeros((TILE, TILE), F32)


def _bias_tiles(rel_bias):
    return pl.pallas_call(
        _bias_kernel,
        in_specs=[pl.BlockSpec(memory_space=pltpu.SMEM)],
        out_specs=pl.BlockSpec(memory_space=pltpu.VMEM),
        out_shape=jax.ShapeDtypeStruct((N_ATTN_HEADS, 3, TILE, TILE), F32),
        name="bias_tiles",
    )(rel_bias)


def _key_to_f32(u):
    ks = u ^ I32(-2 ** 31)
    bits = jnp.where(ks >= 0, ks, ks ^ I32(0x7FFFFFFF))
    return lax.bitcast_convert_type(bits, F32)


PAIR = 2 * TILE
QUAD = 4 * TILE
QT = 2 * TILE
COARSE_BITS = 16
PV_ROWS = HEAD_DIM + 16


def _trunc_bf16(x):
    return lax.bitcast_convert_type(lax.bitcast_convert_type(x, I32) & I32(-65536), F32)


def _attn_kernel(q_ref, kv_ref, qi_ref, ki_ref, dtw_ref, bias_ref, o_ref,
                 sc_ref, sh_ref, rk_ref, vt_ref, lg_ref, *, seq_len, n_sel):
    j = pl.program_id(1)
    i_hi = j * (QT // TILE) + QT // TILE - 1
    n_quads = seq_len // QUAD
    nq4 = i_hi // 4 + 1
    npair = i_hi // 2 + 1
    rep = N_ATTN_HEADS // N_KV_HEADS
    gl = rep * QT
    nt_dims = (((1,), (1,)), ((), ()))
    v_lo = N_KV_HEADS * HEAD_DIM

    @pl.when(j == 0)
    def _():
        vt = kv_ref[0, :, v_lo:].astype(F32).T
        ones_row = jnp.where(lax.broadcasted_iota(I32, (PV_ROWS - HEAD_DIM, seq_len), 0) == 0, 1.0, 0.0)
        for g in range(N_KV_HEADS):
            vt_ref[g * PV_ROWS:g * PV_ROWS + HEAD_DIM, :] = vt[g * HEAD_DIM:(g + 1) * HEAD_DIM].astype(BF16)
            vt_ref[g * PV_ROWS + HEAD_DIM:(g + 1) * PV_ROWS, :] = ones_row.astype(BF16)

    w_t = dtw_ref[0].T
    idx_scale = (N_IDX_HEADS ** -0.5) * (IDX_DIM ** -0.5)
    w_rows = [w_t[DTW_WI + h:DTW_WI + h + 1, :] * idx_scale for h in range(N_IDX_HEADS)]
    qi = qi_ref[0]
    qi_all = jnp.concatenate(
        [qi[:, h * IDX_DIM:(h + 1) * IDX_DIM] for h in range(N_IDX_HEADS)], axis=0)
    s_loc = lax.broadcasted_iota(I32, (PAIR, QT), 0)
    s_minus_t = s_loc - lax.broadcasted_iota(I32, (PAIR, QT), 1)

    def score_quads(v):
        for c in range(v):
            kt = ki_ref[0, c * QUAD:(c + 1) * QUAD, :]
            d = lax.dot_general(kt, qi_all, nt_dims, preferred_element_type=F32)
            s = jnp.zeros((QUAD, QT), F32)
            for h in range(N_IDX_HEADS):
                s = s + w_rows[h] * jnp.maximum(d[:, h * QT:(h + 1) * QT], 0.0)
            for u in range(QUAD // PAIR):
                pr = c * (QUAD // PAIR) + u
                su = jnp.where(s_minus_t <= j * QT - pr * PAIR, s[u * PAIR:(u + 1) * PAIR], NEG_INF)
                sc_ref[pr] = su
                sh_ref[pr] = _trunc_bf16(su).astype(BF16)

    for v in range(1, n_quads + 1):
        pl.when(nq4 == v)(functools.partial(score_quads, v))

    t_glob = j * QT + lax.broadcasted_iota(I32, (1, QT), 1)
    k_eff = jnp.minimum(n_sel, t_glob + 1).astype(F32)
    acc_rows = 4 * SUBLANES
    acc_rows_b = 4 * 2 * SUBLANES

    def count_pairs(pred):
        def body(c, acc):
            cnt = jnp.where(pred(c, sc_ref[c]), 1.0, 0.0)
            return acc + cnt.reshape(PAIR // acc_rows, acc_rows, QT).sum(axis=0)
        acc = lax.fori_loop(0, npair, body, jnp.zeros((acc_rows, QT), F32))
        return acc.sum(axis=0, keepdims=True)

    def count_pairs_coarse(thr_b):
        one, zero = jnp.ones((), BF16), jnp.zeros((), BF16)

        def body(c, acc):
            cnt = jnp.where(sh_ref[c] >= thr_b, one, zero)
            parts = [cnt[k * acc_rows_b:(k + 1) * acc_rows_b] for k in range(PAIR // acc_rows_b)]
            while len(parts) > 1:
                parts = [a + b for a, b in zip(parts[::2], parts[1::2])]
            return acc + parts[0]
        acc = lax.fori_loop(0, npair, body, jnp.zeros((acc_rows_b, QT), BF16))
        return acc.astype(F32).sum(axis=0, keepdims=True)

    def search_step(b, carry, coarse):
        prefix, cnt_ge = carry
        cand = prefix | lax.shift_left(I32(1), 31 - b)
        thr = _key_to_f32(cand)
        if coarse:
            cnt = count_pairs_coarse(_trunc_bf16(thr).astype(BF16))
        else:
            cnt = count_pairs(lambda c, s: s >= thr)
        ok = cnt >= k_eff
        return jnp.where(ok, cand, prefix), jnp.where(ok, cnt, cnt_ge)

    carry = (jnp.zeros((1, QT), I32), (t_glob + 1).astype(F32))
    carry = lax.fori_loop(0, COARSE_BITS, functools.partial(search_step, coarse=True), carry)
    prefix, cnt_ge = lax.fori_loop(COARSE_BITS, 32, functools.partial(search_step, coarse=False), carry)
    thr = _key_to_f32(prefix)

    q = q_ref[0]
    qgs = [jnp.concatenate(
        [q[:, (g * rep + r) * HEAD_DIM:(g * rep + r + 1) * HEAD_DIM] for r in range(rep)],
        axis=0) for g in range(N_KV_HEADS)]
    below_diag = (lax.broadcasted_iota(I32, (PAIR, PAIR), 0)
                  > lax.broadcasted_iota(I32, (PAIR, PAIR), 1)).astype(BF16)
    tiles_per_quad = QUAD // TILE
    pairs_per_quad = QUAD // PAIR
    n_qt = QT // TILE
    int_min = I32(-2 ** 31)

    def bias_rows(c, g, u):
        cols = []
        for r in range(rep):
            for w in range(n_qt):
                off = jnp.clip(j * n_qt + w - (c * tiles_per_quad + u), 0, 2)
                cols.append(bias_ref[g * rep + r, off])
        return jnp.concatenate(cols, axis=1)

    def tied(s):
        eq = jnp.where(s == thr, 1.0, 0.0)
        neg0 = eq * jnp.where(lax.bitcast_convert_type(s, I32) == int_min, 1.0, 0.0)
        return eq - neg0, neg0

    def attend(v):
        tots = []
        for pr in range(v * pairs_per_quad):
            e2 = jnp.concatenate(tied(sc_ref[pr]), axis=1).astype(BF16)
            r = jnp.dot(below_diag, e2, preferred_element_type=F32)
            rk_ref[pr] = r
            tots.append(r[PAIR - 1:PAIR] + e2[PAIR - 1:PAIR].astype(F32))
        tot = tots[0]
        for t in tots[1:]:
            tot = tot + t
        need = k_eff - (cnt_ge - tot[:, :QT] - tot[:, QT:])
        off = jnp.concatenate([jnp.zeros((1, QT), F32), tot[:, :QT]], axis=1)

        m8 = [jnp.full((SUBLANES, gl), NEG_INF, F32) for _ in range(N_KV_HEADS)]
        for c in range(v):
            masks = []
            for u in range(pairs_per_quad):
                pr = c * pairs_per_quad + u
                s = sc_ref[pr]
                r = rk_ref[pr] + off
                off = off + tots[pr]
                neg0 = lax.bitcast_convert_type(s, I32) == int_min
                keep_tie = jnp.where(jnp.where(neg0, r[:, QT:], r[:, :QT]) < need, 0.0, NEG_INF)
                masks.append(jnp.where(s > thr, 0.0, jnp.where(s == thr, keep_tie, NEG_INF)))
            mask = jnp.concatenate(masks, axis=0)
            mask = jnp.concatenate([mask] * rep, axis=1)
            keys = slice(c * QUAD, (c + 1) * QUAD)
            for g in range(N_KV_HEADS):
                kt = kv_ref[0, keys, g * HEAD_DIM:(g + 1) * HEAD_DIM]
                lg = lax.dot_general(kt, qgs[g], nt_dims, preferred_element_type=F32) + mask
                if c == v - 1:
                    lg = lg + jnp.concatenate([bias_rows(c, g, u) for u in range(tiles_per_quad)], axis=0)
                elif c == v - 2:
                    u = tiles_per_quad - 1
                    lg = jnp.concatenate([lg[:u * TILE], lg[u * TILE:] + bias_rows(c, g, u)], axis=0)
                lg_ref[c, g] = lg
                m8[g] = jnp.maximum(m8[g], lg.reshape(QUAD // SUBLANES, SUBLANES, gl).max(axis=0))

        outs = []
        for g in range(N_KV_HEADS):
            m_row = jnp.max(m8[g], axis=0, keepdims=True)
            acc = jnp.zeros((PV_ROWS, gl), F32)
            for c in range(v):
                p = jnp.exp2((lg_ref[c, g] - m_row).astype(BF16))
                vt = vt_ref[g * PV_ROWS:(g + 1) * PV_ROWS, c * QUAD:(c + 1) * QUAD]
                acc = acc + jnp.dot(vt, p, preferred_element_type=F32)
            o_g = acc[:HEAD_DIM] / acc[HEAD_DIM:HEAD_DIM + 1]
            outs += [o_g[:, r * QT:(r + 1) * QT] for r in range(rep)]
        o_ref[0] = jnp.concatenate(outs, axis=0).T.astype(BF16)

    for v in range(1, n_quads + 1):
        pl.when(nq4 == v)(functools.partial(attend, v))


def _attention(qkv3, ki3, dtw3, bias, n_sel):
    bsz, seq_len, _ = qkv3.shape
    assert seq_len % QUAD == 0 and QUAD % QT == 0
    rep = N_ATTN_HEADS // N_KV_HEADS
    kv_w = 2 * N_KV_HEADS * HEAD_DIM
    qi_w = N_IDX_HEADS * IDX_DIM
    kern = functools.partial(_attn_kernel, seq_len=seq_len, n_sel=n_sel)
    return pl.pallas_call(
        kern,
        grid=(bsz, seq_len // QT),
        in_specs=[
            pl.BlockSpec((1, QT, ATTN_WIDTH), lambda b, j: (b, j, 0)),
            pl.BlockSpec((1, seq_len, kv_w), lambda b, j: (b, 0, ATTN_WIDTH // kv_w)),
            pl.BlockSpec((1, QT, qi_w), lambda b, j: (b, j, (ATTN_WIDTH + kv_w) // qi_w)),
            pl.BlockSpec((1, seq_len, IDX_DIM), lambda b, j: (b, 0, 0)),
            pl.BlockSpec((1, QT, LANES), lambda b, j: (b, j, 0)),
            pl.BlockSpec((N_ATTN_HEADS, 3, TILE, TILE), lambda b, j: (0, 0, 0, 0)),
        ],
        out_specs=pl.BlockSpec((1, QT, ATTN_WIDTH), lambda b, j: (b, j, 0)),
        out_shape=jax.ShapeDtypeStruct((bsz, seq_len, ATTN_WIDTH), BF16),
        scratch_shapes=[
            pltpu.VMEM((seq_len // PAIR, PAIR, QT), F32),
            pltpu.VMEM((seq_len // PAIR, PAIR, QT), BF16),
            pltpu.VMEM((seq_len // PAIR, PAIR, 2 * QT), F32),
            pltpu.VMEM((N_KV_HEADS * PV_ROWS, seq_len), BF16),
            pltpu.VMEM((seq_len // QUAD, N_KV_HEADS, QUAD, rep * QT), F32),
        ],
        compiler_params=pltpu.CompilerParams(
            dimension_semantics=("arbitrary", "arbitrary"), vmem_limit_bytes=VMEM_LIMIT),
        name="sparse_attn",
    )(qkv3, qkv3, qkv3, ki3, dtw3, bias)


SSD_STEP_CHUNKS = 2
SSD_STEP = SSD_STEP_CHUNKS * TILE


def _split_bf16(v, n):
    parts = []
    for _ in range(n):
        p = v.astype(BF16)
        parts.append(p)
        v = v - p.astype(F32)
    return parts


def _dot_01(v, mat01, n):
    out = None
    for p in _split_bf16(v, n):
        t = jnp.dot(p, mat01, preferred_element_type=F32)
        out = t if out is None else out + t
    return out


def _ssd_kernel(z_ref, xbc_ref, dtw_ref, cw_ref, cb_ref, dtb_ref, alog_ref, dsk_ref, nw_ref,
                o_ref, xpad_ref, st_ref):
    step = pl.program_id(1)
    gn = SSD_N_GROUPS * SSD_D_STATE
    hpg = SSD_N_HEADS // SSD_N_GROUPS
    gw = hpg * SSD_HEAD_DIM

    @pl.when(step == 0)
    def _():
        xpad_ref[0:SUBLANES, :] = jnp.zeros((SUBLANES, CONV_DIM), F32)
        st_ref[...] = jnp.zeros_like(st_ref)

    xpad_ref[SUBLANES:, :] = xbc_ref[0]
    xfull = xpad_ref[...]
    conv = cb_ref[...] + cw_ref[CONV_WIDTH - 1:CONV_WIDTH, :] * xfull[SUBLANES:]
    for k in range(CONV_WIDTH - 1):
        shifted = pltpu.roll(xfull, CONV_WIDTH - 1 - k, 0)
        conv = conv + cw_ref[k:k + 1, :] * shifted[SUBLANES:]
    xpad_ref[0:SUBLANES, :] = xfull[SSD_STEP:]
    act = conv * (1.0 / (1.0 + jnp.exp(-conv)))
    xs = act[:, :SSD_D_INNER]
    bm = act[:, SSD_D_INNER:SSD_D_INNER + gn].astype(BF16)
    cm = act[:, SSD_D_INNER + gn:].astype(BF16)

    dt_in = dtw_ref[0] + dtb_ref[...]
    dt = jnp.maximum(dt_in, 0.0) + jnp.log1p(jnp.exp(-jnp.abs(dt_in)))
    adt = dt * (-jnp.exp(alog_ref[...]))
    row = lax.broadcasted_iota(I32, (SSD_STEP, SSD_STEP), 0)
    col = lax.broadcasted_iota(I32, (SSD_STEP, SSD_STEP), 1)
    same_chunk = (row // TILE) == (col // TILE)
    chunk_tril = jnp.where(same_chunk & (row >= col), 1.0, 0.0).astype(BF16)
    acs = None
    for p in _split_bf16(adt, 3):
        t = jnp.dot(chunk_tril, p, preferred_element_type=F32)
        acs = t if acs is None else acs + t
    acs_t = acs.T
    causal = (lax.broadcasted_iota(I32, (TILE, TILE), 0) >= lax.broadcasted_iota(I32, (TILE, TILE), 1))
    a_last = [acs[(c + 1) * TILE - 1:(c + 1) * TILE, :] for c in range(SSD_STEP_CHUNKS)]
    out_decay = jnp.concatenate(
        [jnp.exp(a_last[c] - acs[c * TILE:(c + 1) * TILE]) for c in range(SSD_STEP_CHUNKS)], axis=0)
    chunk_decay = jnp.concatenate(
        [jnp.exp(a) for a in a_last] + [jnp.zeros((SUBLANES - SSD_STEP_CHUNKS, LANES), F32)], axis=0)
    hsel = (lax.broadcasted_iota(I32, (LANES, SSD_D_INNER), 0)
            == lax.broadcasted_iota(I32, (LANES, SSD_D_INNER), 1) // SSD_HEAD_DIM).astype(BF16)
    expanded = _dot_01(jnp.concatenate([dt, jnp.exp(acs), out_decay, chunk_decay], axis=0), hsel, 2)
    dt_x = expanded[:SSD_STEP]
    in_decay_x = expanded[SSD_STEP:2 * SSD_STEP]
    out_decay_x = expanded[2 * SSD_STEP:3 * SSD_STEP]
    chunk_decay_x = expanded[3 * SSD_STEP:]

    x_dt = xs * dt_x
    x_dt_b = x_dt.astype(BF16)
    x_out_b = (x_dt * out_decay_x).astype(BF16)
    tn_dims = (((0,), (0,)), ((), ()))
    nt_dims = (((1,), (1,)), ((), ()))
    states = [st_ref[g] for g in range(SSD_N_GROUPS)]
    y_chunks = []
    for c in range(SSD_STEP_CHUNKS):
        rows = slice(c * TILE, (c + 1) * TILE)
        y_parts = []
        for g in range(SSD_N_GROUPS):
            bg = bm[rows, g * SSD_D_STATE:(g + 1) * SSD_D_STATE]
            cg = cm[rows, g * SSD_D_STATE:(g + 1) * SSD_D_STATE]
            lanes = slice(g * gw, (g + 1) * gw)
            cb = lax.dot_general(cg, bg, nt_dims, preferred_element_type=F32)
            y_diag = []
            for r in range(hpg):
                h = g * hpg + r
                seg = jnp.where(causal, acs[rows, h:h + 1] - acs_t[h:h + 1, rows], NEG_INF)
                w = (cb * jnp.exp(seg)).astype(BF16)
                y_diag.append(jnp.dot(w, x_dt_b[rows, h * SSD_HEAD_DIM:(h + 1) * SSD_HEAD_DIM],
                                      preferred_element_type=F32))
            y_off = (jnp.dot(cg, states[g].astype(BF16), preferred_element_type=F32)
                     * in_decay_x[rows, lanes])
            y_parts.append(jnp.concatenate(y_diag, axis=-1) + y_off)
            st_new = lax.dot_general(bg, x_out_b[rows, lanes], tn_dims, preferred_element_type=F32)
            states[g] = states[g] * chunk_decay_x[c:c + 1, lanes] + st_new
        y_chunks.append(jnp.concatenate(y_parts, axis=-1))
    for g in range(SSD_N_GROUPS):
        st_ref[g] = states[g]

    y = jnp.concatenate(y_chunks, axis=0) + xs * dsk_ref[...]
    zv = z_ref[0]
    y = y * (zv * (1.0 / (1.0 + jnp.exp(-zv))))
    ng = SSD_D_INNER // SSD_N_GROUPS
    outs = []
    for g in range(SSD_N_GROUPS):
        yg = y[:, g * ng:(g + 1) * ng]
        outs.append(yg * lax.rsqrt(jnp.mean(yg * yg, axis=-1, keepdims=True) + EPS))
    o_ref[0] = (jnp.concatenate(outs, axis=-1) * nw_ref[...]).astype(BF16)


def _ssd(z3, xbc3, dtw3, conv_w, conv_b, dt_bias, a_log, d_skip, norm_w):
    bsz, seq_len, _ = z3.shape
    assert seq_len % SSD_STEP == 0
    nc = seq_len // SSD_STEP
    blk = lambda w: pl.BlockSpec((1, SSD_STEP, w), lambda b, c: (b, c, 0))
    par = lambda r, w: pl.BlockSpec((r, w), lambda b, c: (0, 0))
    return pl.pallas_call(
        _ssd_kernel,
        grid=(bsz, nc),
        in_specs=[blk(SSD_D_INNER), blk(CONV_DIM), blk(LANES),
                  par(CONV_WIDTH, CONV_DIM), par(1, CONV_DIM), par(1, LANES), par(1, LANES),
                  par(1, SSD_D_INNER), par(1, SSD_D_INNER)],
        out_specs=blk(SSD_D_INNER),
        out_shape=jax.ShapeDtypeStruct((bsz, seq_len, SSD_D_INNER), BF16),
        scratch_shapes=[
            pltpu.VMEM((SUBLANES + SSD_STEP, CONV_DIM), F32),
            pltpu.VMEM((SSD_N_GROUPS, SSD_D_STATE, SSD_D_INNER // SSD_N_GROUPS), F32),
        ],
        compiler_params=pltpu.CompilerParams(
            dimension_semantics=("arbitrary", "arbitrary"), vmem_limit_bytes=VMEM_LIMIT),
        name="ssd_mixer",
    )(z3, xbc3, dtw3, conv_w, conv_b, dt_bias, a_log, d_skip, norm_w)


FF_CHUNK = 512


def _mlp_kernel(x_ref, attn_ref, ssd_ref, wo_ref, g1_ref, g2_ref, wu_ref, wd_ref, g3_ref, o_ref):
    mix = (jnp.dot(attn_ref[...], wo_ref[:ATTN_WIDTH, :], preferred_element_type=F32)
           + jnp.dot(ssd_ref[...], wo_ref[ATTN_WIDTH:, :], preferred_element_type=F32))
    h1 = x_ref[...] + _rms(mix, g1_ref[...])
    u = _rms(h1, g2_ref[...]).astype(BF16)
    acc = jnp.zeros(h1.shape, F32)
    for c in range(0, D_FF, FF_CHUNK):
        f = jnp.dot(u, wu_ref[:, c:c + FF_CHUNK], preferred_element_type=F32)
        f = jnp.square(jnp.maximum(f, 0.0)).astype(BF16)
        acc = acc + jnp.dot(f, wd_ref[c:c + FF_CHUNK, :], preferred_element_type=F32)
    o_ref[...] = h1 + _rms(acc, g3_ref[...])


def _mlp(x2, attn2, ssd2, wo, g1, g2, wu, wd, g3, tm):
    n = x2.shape[0]
    row = lambda i: (i, 0)
    const = lambda i: (0, 0)
    single = dict(pipeline_mode=pl.Buffered(1))
    return pl.pallas_call(
        _mlp_kernel,
        grid=(n // tm,),
        in_specs=[
            pl.BlockSpec((tm, D_MODEL), row),
            pl.BlockSpec((tm, ATTN_WIDTH), row),
            pl.BlockSpec((tm, SSD_D_INNER), row),
            pl.BlockSpec((ATTN_WIDTH + SSD_D_INNER, D_MODEL), const, **single),
            pl.BlockSpec((1, D_MODEL), const),
            pl.BlockSpec((1, D_MODEL), const),
            pl.BlockSpec((D_MODEL, D_FF), const, **single),
            pl.BlockSpec((D_FF, D_MODEL), const, **single),
            pl.BlockSpec((1, D_MODEL), const),
        ],
        out_specs=pl.BlockSpec((tm, D_MODEL), row),
        out_shape=jax.ShapeDtypeStruct((n, D_MODEL), F32),
        compiler_params=pltpu.CompilerParams(
            dimension_semantics=("arbitrary",), vmem_limit_bytes=VMEM_LIMIT),
        name="out_proj_mlp",
    )(x2, attn2, ssd2, wo, g1, g2, wu, wd, g3)


def _pack_w_in(w):
    w = w.astype(BF16)
    o = 0
    parts = {}
    for name, width in (("qkvqi", QKV_W), ("ki", IDX_DIM), ("wi", N_IDX_HEADS),
                        ("z", SSD_D_INNER), ("xbc", CONV_DIM), ("dt", SSD_N_HEADS)):
        parts[name] = w[:, o:o + width]
        o += width
    zeros = lambda n: jnp.zeros((w.shape[0], n), w.dtype)
    return jnp.concatenate([
        parts["qkvqi"], parts["ki"], zeros(LANES - IDX_DIM), parts["z"], parts["xbc"],
        parts["dt"], parts["wi"], zeros(LANES - SSD_N_HEADS - N_IDX_HEADS)], axis=1)


def _pad_lanes(v, n):
    return jnp.pad(v, (0, n - v.shape[0])).reshape(1, n)


def kernel(x, norm_pre_mix, norm_post_mix, norm_pre_mlp, norm_post_mlp, w_in, k_idx_ln_w, k_idx_ln_b, conv_w, conv_b, dt_bias, a_log, d_skip, ssd_norm_w, w_out, w_mlp_up, w_mlp_down, rel_bias):
    bsz, seq_len, d = x.shape
    n = bsz * seq_len
    assert d == D_MODEL and seq_len % TILE == 0
    tm = 512 if n % 512 == 0 else TILE
    n_sel = min(TOPK_MAX, seq_len // 4)
    bias = _bias_tiles(rel_bias)
    h = x.reshape(n, d)
    for i in range(norm_pre_mix.shape[0]):
        row = lambda v: v[i].reshape(1, -1)
        qkv, ki, z, xbc, dtw = _in_proj(h, row(norm_pre_mix), _pack_w_in(w_in[i]),
                                        row(k_idx_ln_w), row(k_idx_ln_b), tm)
        r3 = lambda a: a.reshape(bsz, seq_len, a.shape[-1])
        dtw3 = r3(dtw)
        attn = _attention(r3(qkv), r3(ki), dtw3, bias, n_sel)
        ssd = _ssd(r3(z), r3(xbc), dtw3, conv_w[i], row(conv_b),
                   _pad_lanes(dt_bias[i], LANES), _pad_lanes(a_log[i], LANES),
                   jnp.repeat(d_skip[i], SSD_HEAD_DIM).reshape(1, -1), row(ssd_norm_w))
        h = _mlp(h, attn.reshape(n, -1), ssd.reshape(n, -1), w_out[i].astype(BF16),
                 row(norm_post_mix), row(norm_pre_mlp), w_mlp_up[i].astype(BF16),
                 w_mlp_down[i].astype(BF16), row(norm_post_mlp), tm)
    return h.reshape(bsz, seq_len, d)
```

```python
import functools
import math

import jax
import jax.numpy as jnp
from jax import lax
from jax.experimental import pallas as pl
from jax.experimental.pallas import tpu as pltpu

F32 = jnp.float32
BF16 = jnp.bfloat16
I32 = jnp.int32

D_MODEL = 1024
N_ATTN_HEADS = 8
N_KV_HEADS = 2
HEAD_DIM = 64
ATTN_WIDTH = N_ATTN_HEADS * HEAD_DIM
N_IDX_HEADS = 4
IDX_DIM = 64
TOPK_MAX = 256
NUM_BUCKETS = 32
MAX_DISTANCE = 128
SSD_D_INNER = 512
SSD_HEAD_DIM = 64
SSD_N_HEADS = 8
SSD_N_GROUPS = 2
SSD_D_STATE = 128
CONV_WIDTH = 4
CONV_DIM = SSD_D_INNER + 2 * SSD_N_GROUPS * SSD_D_STATE
D_FF = 4 * D_MODEL
EPS = 1e-6

LANES = 128
SUBLANES = 8
TILE = 128
VMEM_LIMIT = 56 * 1024 * 1024

QKV_W = ATTN_WIDTH + 2 * N_KV_HEADS * HEAD_DIM + N_IDX_HEADS * IDX_DIM
KV_W = 2 * N_KV_HEADS * HEAD_DIM
QI_W = N_IDX_HEADS * IDX_DIM
COL_QKV = 0
COL_KV = COL_QKV + ATTN_WIDTH
COL_QI = COL_KV + KV_W
COL_KI = COL_QKV + QKV_W
COL_Z = COL_KI + LANES
COL_XBC = COL_Z + SSD_D_INNER
COL_DTW = COL_XBC + CONV_DIM
W_CAT = COL_DTW + LANES
DTW_WI = SSD_N_HEADS

NEG_INF = float("-inf")
LOG2E = 1.4426950408889634
Q_SCALE = LOG2E * HEAD_DIM ** -0.5


def _rms(x, g):
    return x * lax.rsqrt(jnp.mean(x * x, axis=-1, keepdims=True) + EPS) * g


def _in_proj_kernel(x_ref, g_ref, w_ref, lnw_ref, lnb_ref,
                    q_ref, kv_ref, qi_ref, ki_ref, z_ref, xbc_ref, dtw_ref):
    u = _rms(x_ref[...], g_ref[...]).astype(BF16)

    def mm(lo, hi):
        return jnp.dot(u, w_ref[:, lo:hi], preferred_element_type=F32)

    q_ref[...] = (mm(COL_QKV, COL_KV) * Q_SCALE).astype(BF16)
    kv_ref[...] = mm(COL_KV, COL_QI).astype(BF16)
    qi_ref[...] = mm(COL_QI, COL_KI).astype(BF16)
    ki = mm(COL_KI, COL_Z)[:, :IDX_DIM]
    mu = jnp.mean(ki, axis=-1, keepdims=True)
    var = jnp.mean(jnp.square(ki - mu), axis=-1, keepdims=True)
    ki_ref[...] = ((ki - mu) * lax.rsqrt(var + EPS) * lnw_ref[...] + lnb_ref[...]).astype(BF16)
    z_ref[...] = mm(COL_Z, COL_XBC)
    xbc_ref[...] = mm(COL_XBC, COL_DTW)
    dtw_ref[...] = mm(COL_DTW, W_CAT)


def _in_proj(x2, g, w_cat, lnw, lnb, tm):
    n = x2.shape[0]
    row = lambda i: (i, 0)
    const = lambda i: (0, 0)
    return pl.pallas_call(
        _in_proj_kernel,
        grid=(n // tm,),
        in_specs=[
            pl.BlockSpec((tm, D_MODEL), row),
            pl.BlockSpec((1, D_MODEL), const),
            pl.BlockSpec((D_MODEL, W_CAT), const, pipeline_mode=pl.Buffered(1)),
            pl.BlockSpec((1, IDX_DIM), const),
            pl.BlockSpec((1, IDX_DIM), const),
        ],
        out_specs=[
            pl.BlockSpec((tm, ATTN_WIDTH), row),
            pl.BlockSpec((tm, KV_W), row),
            pl.BlockSpec((tm, QI_W), row),
            pl.BlockSpec((tm, IDX_DIM), row),
            pl.BlockSpec((tm, SSD_D_INNER), row),
            pl.BlockSpec((tm, CONV_DIM), row),
            pl.BlockSpec((tm, LANES), row),
        ],
        out_shape=[
            jax.ShapeDtypeStruct((n, ATTN_WIDTH), BF16),
            jax.ShapeDtypeStruct((n, KV_W), BF16),
            jax.ShapeDtypeStruct((n, QI_W), BF16),
            jax.ShapeDtypeStruct((n, IDX_DIM), BF16),
            jax.ShapeDtypeStruct((n, SSD_D_INNER), F32),
            jax.ShapeDtypeStruct((n, CONV_DIM), F32),
            jax.ShapeDtypeStruct((n, LANES), F32),
        ],
        compiler_params=pltpu.CompilerParams(
            dimension_semantics=("arbitrary",), vmem_limit_bytes=VMEM_LIMIT),
        name="in_proj",
    )(x2, g, w_cat, lnw, lnb)


def _bias_kernel(tbl_ref, out_ref):
    sk = lax.broadcasted_iota(I32, (TILE, TILE), 0)
    tq = lax.broadcasted_iota(I32, (TILE, TILE), 1)
    max_exact = NUM_BUCKETS // 2
    for off in range(2):
        dist = jnp.maximum(off * TILE + tq - sk, 0)
        df = jnp.maximum(dist, 1).astype(F32)
        large = max_exact + (jnp.log(df / max_exact) / math.log(MAX_DISTANCE / max_exact)
                             * (NUM_BUCKETS - max_exact)).astype(I32)
        large = jnp.minimum(large, NUM_BUCKETS - 1)
        bucket = jnp.where(dist < max_exact, dist, large)
        for h in range(N_ATTN_HEADS):
            acc = jnp.zeros((TILE, TILE), F32)
            for b in range(NUM_BUCKETS):
                acc = jnp.where(bucket == b, tbl_ref[b, h], acc)
            out_ref[h, off] = (acc - tbl_ref[NUM_BUCKETS - 1, h]) * LOG2E
    for h in range(N_ATTN_HEADS):
        out_ref[h, 2] = jnp.zeros((TILE, TILE), F32)


def _bias_tiles(rel_bias):
    return pl.pallas_call(
        _bias_kernel,
        in_specs=[pl.BlockSpec(memory_space=pltpu.SMEM)],
        out_specs=pl.BlockSpec(memory_space=pltpu.VMEM),
        out_shape=jax.ShapeDtypeStruct((N_ATTN_HEADS, 3, TILE, TILE), F32),
        name="bias_tiles",
    )(rel_bias)


def _key_to_f32(u):
    ks = u ^ I32(-2 ** 31)
    bits = jnp.where(ks >= 0, ks, ks ^ I32(0x7FFFFFFF))
    return lax.bitcast_convert_type(bits, F32)


PAIR = 2 * TILE
QUAD = 4 * TILE
QT = 2 * TILE
HALF_BITS = 16
HALF_MASK = 2 ** HALF_BITS - 1
HALF_BIAS = 2 ** (HALF_BITS - 1)
I16 = jnp.int16
PV_ROWS = HEAD_DIM + 16


def _attn_kernel(q_ref, kv_ref, qi_ref, ki_ref, dtw_ref, bias_ref, o_ref,
                 sc_ref, hi_ref, lo_ref, rk_ref, vt_ref, lg_ref, *, seq_len, n_sel):
    j = pl.program_id(1)
    i_hi = j * (QT // TILE) + QT // TILE - 1
    n_quads = seq_len // QUAD
    nq4 = i_hi // 4 + 1
    npair = i_hi // 2 + 1
    rep = N_ATTN_HEADS // N_KV_HEADS
    gl = rep * QT
    nt_dims = (((1,), (1,)), ((), ()))
    v_lo = N_KV_HEADS * HEAD_DIM

    @pl.when(j == 0)
    def _():
        vt = kv_ref[0, :, v_lo:].astype(F32).T
        ones_row = jnp.where(lax.broadcasted_iota(I32, (PV_ROWS - HEAD_DIM, seq_len), 0) == 0, 1.0, 0.0)
        for g in range(N_KV_HEADS):
            vt_ref[g * PV_ROWS:g * PV_ROWS + HEAD_DIM, :] = vt[g * HEAD_DIM:(g + 1) * HEAD_DIM].astype(BF16)
            vt_ref[g * PV_ROWS + HEAD_DIM:(g + 1) * PV_ROWS, :] = ones_row.astype(BF16)

    w_t = dtw_ref[0].T
    idx_scale = (N_IDX_HEADS ** -0.5) * (IDX_DIM ** -0.5)
    w_rows = [w_t[DTW_WI + h:DTW_WI + h + 1, :] * idx_scale for h in range(N_IDX_HEADS)]
    qi = qi_ref[0]
    qi_all = jnp.concatenate(
        [qi[:, h * IDX_DIM:(h + 1) * IDX_DIM] for h in range(N_IDX_HEADS)], axis=0)
    s_loc = lax.broadcasted_iota(I32, (PAIR, QT), 0)
    s_minus_t = s_loc - lax.broadcasted_iota(I32, (PAIR, QT), 1)

    def score_quads(v):
        for c in range(v):
            kt = ki_ref[0, c * QUAD:(c + 1) * QUAD, :]
            d = lax.dot_general(kt, qi_all, nt_dims, preferred_element_type=F32)
            s = jnp.zeros((QUAD, QT), F32)
            for h in range(N_IDX_HEADS):
                s = s + w_rows[h] * jnp.maximum(d[:, h * QT:(h + 1) * QT], 0.0)
            for u in range(QUAD // PAIR):
                pr = c * (QUAD // PAIR) + u
                su = jnp.where(s_minus_t <= j * QT - pr * PAIR, s[u * PAIR:(u + 1) * PAIR], NEG_INF)
                sc_ref[pr] = su
                bits = lax.bitcast_convert_type(su, I32)
                key = jnp.where(bits >= 0, bits, bits ^ I32(0x7FFFFFFF))
                hi_ref[pr] = lax.shift_right_arithmetic(key, I32(HALF_BITS)).astype(I16)
                lo_ref[pr] = ((key & I32(HALF_MASK)) - I32(HALF_BIAS)).astype(I16)

    for v in range(1, n_quads + 1):
        pl.when(nq4 == v)(functools.partial(score_quads, v))

    t_glob = j * QT + lax.broadcasted_iota(I32, (1, QT), 1)
    k_eff = jnp.minimum(n_sel, t_glob + 1).astype(F32)
    acc_rows = 4 * 2 * SUBLANES
    one16, zero16 = jnp.ones((), I16), jnp.zeros((), I16)

    def fold(cnt):
        parts = [cnt[k * acc_rows:(k + 1) * acc_rows] for k in range(PAIR // acc_rows)]
        while len(parts) > 1:
            parts = [a + b for a, b in zip(parts[::2], parts[1::2])]
        return parts[0]

    def total(acc):
        return acc.astype(I32).astype(F32).sum(axis=0, keepdims=True)

    def count_ge(plane_ref, cand16):
        def body(c, acc):
            return acc + fold(jnp.where(plane_ref[c] >= cand16, one16, zero16))
        return total(lax.fori_loop(0, npair, body, jnp.zeros((acc_rows, QT), I16)))

    def to16(u):
        return (u - I32(HALF_BIAS)).astype(I16)

    def search_step(b, carry, plane_ref, base):
        prefix, cnt_ge = carry
        cand = prefix | lax.shift_left(I32(1), HALF_BITS - 1 - b)
        cnt = count_ge(plane_ref, to16(cand)) + base
        ok = cnt >= k_eff
        return jnp.where(ok, cand, prefix), jnp.where(ok, cnt, cnt_ge)

    carry = (jnp.zeros((1, QT), I32), (t_glob + 1).astype(F32))
    hi_u, cnt_ge = lax.fori_loop(
        0, HALF_BITS, functools.partial(search_step, plane_ref=hi_ref, base=0.0), carry)
    hi16 = to16(hi_u)

    def narrow_body(c, acc):
        h = hi_ref[c]
        lo_ref[c] = jnp.where(h == hi16, lo_ref[c], I16(-HALF_BIAS))
        return acc + fold(jnp.where(h > hi16, one16, zero16))

    above = total(lax.fori_loop(0, npair, narrow_body, jnp.zeros((acc_rows, QT), I16)))
    lo_u, cnt_ge = lax.fori_loop(
        0, HALF_BITS, functools.partial(search_step, plane_ref=lo_ref, base=above),
        (jnp.zeros((1, QT), I32), cnt_ge))
    thr = _key_to_f32(lax.shift_left(hi_u, I32(HALF_BITS)) | lo_u)

    q = q_ref[0]
    qgs = [jnp.concatenate(
        [q[:, (g * rep + r) * HEAD_DIM:(g * rep + r + 1) * HEAD_DIM] for r in range(rep)],
        axis=0) for g in range(N_KV_HEADS)]
    below_diag = (lax.broadcasted_iota(I32, (PAIR, PAIR), 0)
                  > lax.broadcasted_iota(I32, (PAIR, PAIR), 1)).astype(BF16)
    tiles_per_quad = QUAD // TILE
    pairs_per_quad = QUAD // PAIR
    n_qt = QT // TILE
    int_min = I32(-2 ** 31)

    def bias_rows(c, g, u):
        cols = []
        for r in range(rep):
            for w in range(n_qt):
                off = jnp.clip(j * n_qt + w - (c * tiles_per_quad + u), 0, 2)
                cols.append(bias_ref[g * rep + r, off])
        return jnp.concatenate(cols, axis=1)

    def tied(s):
        eq = jnp.where(s == thr, 1.0, 0.0)
        neg0 = eq * jnp.where(lax.bitcast_convert_type(s, I32) == int_min, 1.0, 0.0)
        return eq - neg0, neg0

    def attend(v):
        tots = []
        for pr in range(v * pairs_per_quad):
            e2 = jnp.concatenate(tied(sc_ref[pr]), axis=1).astype(BF16)
            r = jnp.dot(below_diag, e2, preferred_element_type=F32)
            rk_ref[pr] = r
            tots.append(r[PAIR - 1:PAIR] + e2[PAIR - 1:PAIR].astype(F32))
        tot = tots[0]
        for t in tots[1:]:
            tot = tot + t
        need = k_eff - (cnt_ge - tot[:, :QT] - tot[:, QT:])
        off = jnp.concatenate([jnp.zeros((1, QT), F32), tot[:, :QT]], axis=1)

        m8 = [jnp.full((SUBLANES, gl), NEG_INF, F32) for _ in range(N_KV_HEADS)]
        for c in range(v):
            masks = []
            for u in range(pairs_per_quad):
                pr = c * pairs_per_quad + u
                s = sc_ref[pr]
                r = rk_ref[pr] + off
                off = off + tots[pr]
                neg0 = lax.bitcast_convert_type(s, I32) == int_min
                keep_tie = jnp.where(jnp.where(neg0, r[:, QT:], r[:, :QT]) < need, 0.0, NEG_INF)
                masks.append(jnp.where(s > thr, 0.0, jnp.where(s == thr, keep_tie, NEG_INF)))
            mask = jnp.concatenate(masks, axis=0)
            mask = jnp.concatenate([mask] * rep, axis=1)
            keys = slice(c * QUAD, (c + 1) * QUAD)
            for g in range(N_KV_HEADS):
                kt = kv_ref[0, keys, g * HEAD_DIM:(g + 1) * HEAD_DIM]
                lg = lax.dot_general(kt, qgs[g], nt_dims, preferred_element_type=F32) + mask
                if c == v - 1:
                    lg = lg + jnp.concatenate([bias_rows(c, g, u) for u in range(tiles_per_quad)], axis=0)
                elif c == v - 2:
                    u = tiles_per_quad - 1
                    lg = jnp.concatenate([lg[:u * TILE], lg[u * TILE:] + bias_rows(c, g, u)], axis=0)
                lg_ref[c, g] = lg
                m8[g] = jnp.maximum(m8[g], lg.reshape(QUAD // SUBLANES, SUBLANES, gl).max(axis=0))

        outs = []
        for g in range(N_KV_HEADS):
            m_row = jnp.max(m8[g], axis=0, keepdims=True)
            acc = jnp.zeros((PV_ROWS, gl), F32)
            for c in range(v):
                p = jnp.exp2((lg_ref[c, g] - m_row).astype(BF16))
                vt = vt_ref[g * PV_ROWS:(g + 1) * PV_ROWS, c * QUAD:(c + 1) * QUAD]
                acc = acc + jnp.dot(vt, p, preferred_element_type=F32)
            o_g = acc[:HEAD_DIM] / acc[HEAD_DIM:HEAD_DIM + 1]
            outs += [o_g[:, r * QT:(r + 1) * QT] for r in range(rep)]
        o_ref[0] = jnp.concatenate(outs, axis=0).T.astype(BF16)

    for v in range(1, n_quads + 1):
        pl.when(nq4 == v)(functools.partial(attend, v))


def _attention(q3, kv3, qi3, ki3, dtw3, bias, n_sel):
    bsz, seq_len, _ = q3.shape
    assert seq_len % QUAD == 0 and QUAD % QT == 0
    rep = N_ATTN_HEADS // N_KV_HEADS
    kern = functools.partial(_attn_kernel, seq_len=seq_len, n_sel=n_sel)
    return pl.pallas_call(
        kern,
        grid=(bsz, seq_len // QT),
        in_specs=[
            pl.BlockSpec((1, QT, ATTN_WIDTH), lambda b, j: (b, j, 0)),
            pl.BlockSpec((1, seq_len, KV_W), lambda b, j: (b, 0, 0)),
            pl.BlockSpec((1, QT, QI_W), lambda b, j: (b, j, 0)),
            pl.BlockSpec((1, seq_len, IDX_DIM), lambda b, j: (b, 0, 0)),
            pl.BlockSpec((1, QT, LANES), lambda b, j: (b, j, 0)),
            pl.BlockSpec((N_ATTN_HEADS, 3, TILE, TILE), lambda b, j: (0, 0, 0, 0)),
        ],
        out_specs=pl.BlockSpec((1, QT, ATTN_WIDTH), lambda b, j: (b, j, 0)),
        out_shape=jax.ShapeDtypeStruct((bsz, seq_len, ATTN_WIDTH), BF16),
        scratch_shapes=[
            pltpu.VMEM((seq_len // PAIR, PAIR, QT), F32),
            pltpu.VMEM((seq_len // PAIR, PAIR, QT), I16),
            pltpu.VMEM((seq_len // PAIR, PAIR, QT), I16),
            pltpu.VMEM((seq_len // PAIR, PAIR, 2 * QT), F32),
            pltpu.VMEM((N_KV_HEADS * PV_ROWS, seq_len), BF16),
            pltpu.VMEM((seq_len // QUAD, N_KV_HEADS, QUAD, rep * QT), F32),
        ],
        compiler_params=pltpu.CompilerParams(
            dimension_semantics=("arbitrary", "arbitrary"), vmem_limit_bytes=VMEM_LIMIT),
        name="sparse_attn",
    )(q3, kv3, qi3, ki3, dtw3, bias)


SSD_STEP_CHUNKS = 2
SSD_STEP = SSD_STEP_CHUNKS * TILE


def _split_bf16(v, n):
    parts = []
    for _ in range(n):
        p = v.astype(BF16)
        parts.append(p)
        v = v - p.astype(F32)
    return parts


def _dot_01(v, mat01, n):
    out = None
    for p in _split_bf16(v, n):
        t = jnp.dot(p, mat01, preferred_element_type=F32)
        out = t if out is None else out + t
    return out


def _ssd_kernel(z_ref, xbc_ref, dtw_ref, cw_ref, cb_ref, dtb_ref, alog_ref, dsk_ref, nw_ref,
                o_ref, xpad_ref, st_ref):
    step = pl.program_id(1)
    gn = SSD_N_GROUPS * SSD_D_STATE
    hpg = SSD_N_HEADS // SSD_N_GROUPS
    gw = hpg * SSD_HEAD_DIM

    @pl.when(step == 0)
    def _():
        xpad_ref[0:SUBLANES, :] = jnp.zeros((SUBLANES, CONV_DIM), F32)
        st_ref[...] = jnp.zeros_like(st_ref)

    xpad_ref[SUBLANES:, :] = xbc_ref[0]
    xfull = xpad_ref[...]
    conv = cb_ref[...] + cw_ref[CONV_WIDTH - 1:CONV_WIDTH, :] * xfull[SUBLANES:]
    for k in range(CONV_WIDTH - 1):
        shifted = pltpu.roll(xfull, CONV_WIDTH - 1 - k, 0)
        conv = conv + cw_ref[k:k + 1, :] * shifted[SUBLANES:]
    xpad_ref[0:SUBLANES, :] = xfull[SSD_STEP:]
    act = conv * (1.0 / (1.0 + jnp.exp(-conv)))
    xs = act[:, :SSD_D_INNER]
    bm = act[:, SSD_D_INNER:SSD_D_INNER + gn].astype(BF16)
    cm = act[:, SSD_D_INNER + gn:].astype(BF16)

    dt_in = dtw_ref[0] + dtb_ref[...]
    dt = jnp.maximum(dt_in, 0.0) + jnp.log1p(jnp.exp(-jnp.abs(dt_in)))
    adt = dt * (-jnp.exp(alog_ref[...]))
    row = lax.broadcasted_iota(I32, (SSD_STEP, SSD_STEP), 0)
    col = lax.broadcasted_iota(I32, (SSD_STEP, SSD_STEP), 1)
    same_chunk = (row // TILE) == (col // TILE)
    chunk_tril = jnp.where(same_chunk & (row >= col), 1.0, 0.0).astype(BF16)
    acs = None
    for p in _split_bf16(adt, 3):
        t = jnp.dot(chunk_tril, p, preferred_element_type=F32)
        acs = t if acs is None else acs + t
    acs_t = acs.T
    causal = (lax.broadcasted_iota(I32, (TILE, TILE), 0) >= lax.broadcasted_iota(I32, (TILE, TILE), 1))
    a_last = [acs[(c + 1) * TILE - 1:(c + 1) * TILE, :] for c in range(SSD_STEP_CHUNKS)]
    out_decay = jnp.concatenate(
        [jnp.exp(a_last[c] - acs[c * TILE:(c + 1) * TILE]) for c in range(SSD_STEP_CHUNKS)], axis=0)
    chunk_decay = jnp.concatenate(
        [jnp.exp(a) for a in a_last] + [jnp.zeros((SUBLANES - SSD_STEP_CHUNKS, LANES), F32)], axis=0)
    hsel = (lax.broadcasted_iota(I32, (LANES, SSD_D_INNER), 0)
            == lax.broadcasted_iota(I32, (LANES, SSD_D_INNER), 1) // SSD_HEAD_DIM).astype(BF16)
    expanded = _dot_01(jnp.concatenate([dt, jnp.exp(acs), out_decay, chunk_decay], axis=0), hsel, 2)
    dt_x = expanded[:SSD_STEP]
    in_decay_x = expanded[SSD_STEP:2 * SSD_STEP]
    out_decay_x = expanded[2 * SSD_STEP:3 * SSD_STEP]
    chunk_decay_x = expanded[3 * SSD_STEP:]

    x_dt = xs * dt_x
    x_dt_b = x_dt.astype(BF16)
    x_out_b = (x_dt * out_decay_x).astype(BF16)
    tn_dims = (((0,), (0,)), ((), ()))
    nt_dims = (((1,), (1,)), ((), ()))
    states = [st_ref[g] for g in range(SSD_N_GROUPS)]
    y_chunks = []
    for c in range(SSD_STEP_CHUNKS):
        rows = slice(c * TILE, (c + 1) * TILE)
        y_parts = []
        for g in range(SSD_N_GROUPS):
            bg = bm[rows, g * SSD_D_STATE:(g + 1) * SSD_D_STATE]
            cg = cm[rows, g * SSD_D_STATE:(g + 1) * SSD_D_STATE]
            lanes = slice(g * gw, (g + 1) * gw)
            cb = lax.dot_general(cg, bg, nt_dims, preferred_element_type=F32)
            y_diag = []
            for r in range(hpg):
                h = g * hpg + r
                seg = jnp.where(causal, acs[rows, h:h + 1] - acs_t[h:h + 1, rows], NEG_INF)
                w = (cb * jnp.exp(seg)).astype(BF16)
                y_diag.append(jnp.dot(w, x_dt_b[rows, h * SSD_HEAD_DIM:(h + 1) * SSD_HEAD_DIM],
                                      preferred_element_type=F32))
            y_off = (jnp.dot(cg, states[g].astype(BF16), preferred_element_type=F32)
                     * in_decay_x[rows, lanes])
            y_parts.append(jnp.concatenate(y_diag, axis=-1) + y_off)
            st_new = lax.dot_general(bg, x_out_b[rows, lanes], tn_dims, preferred_element_type=F32)
            states[g] = states[g] * chunk_decay_x[c:c + 1, lanes] + st_new
        y_chunks.append(jnp.concatenate(y_parts, axis=-1))
    for g in range(SSD_N_GROUPS):
        st_ref[g] = states[g]

    y = jnp.concatenate(y_chunks, axis=0) + xs * dsk_ref[...]
    zv = z_ref[0]
    y = y * (zv * (1.0 / (1.0 + jnp.exp(-zv))))
    ng = SSD_D_INNER // SSD_N_GROUPS
    outs = []
    for g in range(SSD_N_GROUPS):
        yg = y[:, g * ng:(g + 1) * ng]
        outs.append(yg * lax.rsqrt(jnp.mean(yg * yg, axis=-1, keepdims=True) + EPS))
    o_ref[0] = (jnp.concatenate(outs, axis=-1) * nw_ref[...]).astype(BF16)


def _ssd(z3, xbc3, dtw3, conv_w, conv_b, dt_bias, a_log, d_skip, norm_w):
    bsz, seq_len, _ = z3.shape
    assert seq_len % SSD_STEP == 0
    nc = seq_len // SSD_STEP
    blk = lambda w: pl.BlockSpec((1, SSD_STEP, w), lambda b, c: (b, c, 0))
    par = lambda r, w: pl.BlockSpec((r, w), lambda b, c: (0, 0))
    return pl.pallas_call(
        _ssd_kernel,
        grid=(bsz, nc),
        in_specs=[blk(SSD_D_INNER), blk(CONV_DIM), blk(LANES),
                  par(CONV_WIDTH, CONV_DIM), par(1, CONV_DIM), par(1, LANES), par(1, LANES),
                  par(1, SSD_D_INNER), par(1, SSD_D_INNER)],
        out_specs=blk(SSD_D_INNER),
        out_shape=jax.ShapeDtypeStruct((bsz, seq_len, SSD_D_INNER), BF16),
        scratch_shapes=[
            pltpu.VMEM((SUBLANES + SSD_STEP, CONV_DIM), F32),
            pltpu.VMEM((SSD_N_GROUPS, SSD_D_STATE, SSD_D_INNER // SSD_N_GROUPS), F32),
        ],
        compiler_params=pltpu.CompilerParams(
            dimension_semantics=("arbitrary", "arbitrary"), vmem_limit_bytes=VMEM_LIMIT),
        name="ssd_mixer",
    )(z3, xbc3, dtw3, conv_w, conv_b, dt_bias, a_log, d_skip, norm_w)


FF_CHUNK = 512


def _mlp_kernel(x_ref, attn_ref, ssd_ref, wo_ref, g1_ref, g2_ref, wu_ref, wd_ref, g3_ref, o_ref):
    mix = (jnp.dot(attn_ref[...], wo_ref[:ATTN_WIDTH, :], preferred_element_type=F32)
           + jnp.dot(ssd_ref[...], wo_ref[ATTN_WIDTH:, :], preferred_element_type=F32))
    h1 = x_ref[...] + _rms(mix, g1_ref[...])
    u = _rms(h1, g2_ref[...]).astype(BF16)
    acc = jnp.zeros(h1.shape, F32)
    for c in range(0, D_FF, FF_CHUNK):
        f = jnp.dot(u, wu_ref[:, c:c + FF_CHUNK], preferred_element_type=F32)
        f = jnp.square(jnp.maximum(f, 0.0)).astype(BF16)
        acc = acc + jnp.dot(f, wd_ref[c:c + FF_CHUNK, :], preferred_element_type=F32)
    o_ref[...] = h1 + _rms(acc, g3_ref[...])


def _mlp(x2, attn2, ssd2, wo, g1, g2, wu, wd, g3, tm):
    n = x2.shape[0]
    row = lambda i: (i, 0)
    const = lambda i: (0, 0)
    single = dict(pipeline_mode=pl.Buffered(1))
    return pl.pallas_call(
        _mlp_kernel,
        grid=(n // tm,),
        in_specs=[
            pl.BlockSpec((tm, D_MODEL), row),
            pl.BlockSpec((tm, ATTN_WIDTH), row),
            pl.BlockSpec((tm, SSD_D_INNER), row),
            pl.BlockSpec((ATTN_WIDTH + SSD_D_INNER, D_MODEL), const, **single),
            pl.BlockSpec((1, D_MODEL), const),
            pl.BlockSpec((1, D_MODEL), const),
            pl.BlockSpec((D_MODEL, D_FF), const, **single),
            pl.BlockSpec((D_FF, D_MODEL), const, **single),
            pl.BlockSpec((1, D_MODEL), const),
        ],
        out_specs=pl.BlockSpec((tm, D_MODEL), row),
        out_shape=jax.ShapeDtypeStruct((n, D_MODEL), F32),
        compiler_params=pltpu.CompilerParams(
            dimension_semantics=("arbitrary",), vmem_limit_bytes=VMEM_LIMIT),
        name="out_proj_mlp",
    )(x2, attn2, ssd2, wo, g1, g2, wu, wd, g3)


def _pack_w_in(w):
    w = w.astype(BF16)
    o = 0
    parts = {}
    for name, width in (("qkvqi", QKV_W), ("ki", IDX_DIM), ("wi", N_IDX_HEADS),
                        ("z", SSD_D_INNER), ("xbc", CONV_DIM), ("dt", SSD_N_HEADS)):
        parts[name] = w[:, o:o + width]
        o += width
    zeros = lambda n: jnp.zeros((w.shape[0], n), w.dtype)
    return jnp.concatenate([
        parts["qkvqi"], parts["ki"], zeros(LANES - IDX_DIM), parts["z"], parts["xbc"],
        parts["dt"], parts["wi"], zeros(LANES - SSD_N_HEADS - N_IDX_HEADS)], axis=1)


def _pad_lanes(v, n):
    return jnp.pad(v, (0, n - v.shape[0])).reshape(1, n)


def kernel(x, norm_pre_mix, norm_post_mix, norm_pre_mlp, norm_post_mlp, w_in, k_idx_ln_w, k_idx_ln_b, conv_w, conv_b, dt_bias, a_log, d_skip, ssd_norm_w, w_out, w_mlp_up, w_mlp_down, rel_bias):
    bsz, seq_len, d = x.shape
    n = bsz * seq_len
    assert d == D_MODEL and seq_len % TILE == 0
    tm = next(t for t in (1024, 512, TILE) if n % t == 0)
    n_sel = min(TOPK_MAX, seq_len // 4)
    bias = _bias_tiles(rel_bias)
    h = x.reshape(n, d)
    for i in range(norm_pre_mix.shape[0]):
        row = lambda v: v[i].reshape(1, -1)
        q, kv, qi, ki, z, xbc, dtw = _in_proj(h, row(norm_pre_mix), _pack_w_in(w_in[i]),
                                              row(k_idx_ln_w), row(k_idx_ln_b), tm)
        r3 = lambda a: a.reshape(bsz, seq_len, a.shape[-1])
        dtw3 = r3(dtw)
        attn = _attention(r3(q), r3(kv), r3(qi), r3(ki), dtw3, bias, n_sel)
        ssd = _ssd(r3(z), r3(xbc), dtw3, conv_w[i], row(conv_b),
                   _pad_lanes(dt_bias[i], LANES), _pad_lanes(a_log[i], LANES),
                   jnp.repeat(d_skip[i], SSD_HEAD_DIM).reshape(1, -1), row(ssd_norm_w))
        h = _mlp(h, attn.reshape(n, -1), ssd.reshape(n, -1), w_out[i].astype(BF16),
                 row(norm_post_mix), row(norm_pre_mlp), w_mlp_up[i].astype(BF16),
                 w_mlp_down[i].astype(BF16), row(norm_post_mlp), tm)
    return h.reshape(bsz, seq_len, d)
```

```python
import functools
import math

import jax
import jax.numpy as jnp
from jax import lax
from jax.experimental import pallas as pl
from jax.experimental.pallas import tpu as pltpu

F32 = jnp.float32
BF16 = jnp.bfloat16
I32 = jnp.int32

D_MODEL = 1024
N_ATTN_HEADS = 8
N_KV_HEADS = 2
HEAD_DIM = 64
ATTN_WIDTH = N_ATTN_HEADS * HEAD_DIM
N_IDX_HEADS = 4
IDX_DIM = 64
TOPK_MAX = 256
NUM_BUCKETS = 32
MAX_DISTANCE = 128
SSD_D_INNER = 512
SSD_HEAD_DIM = 64
SSD_N_HEADS = 8
SSD_N_GROUPS = 2
SSD_D_STATE = 128
CONV_WIDTH = 4
CONV_DIM = SSD_D_INNER + 2 * SSD_N_GROUPS * SSD_D_STATE
D_FF = 4 * D_MODEL
EPS = 1e-6

LANES = 128
SUBLANES = 8
TILE = 128
VMEM_LIMIT = 56 * 1024 * 1024

QKV_W = ATTN_WIDTH + 2 * N_KV_HEADS * HEAD_DIM + N_IDX_HEADS * IDX_DIM
KV_W = 2 * N_KV_HEADS * HEAD_DIM
QI_W = N_IDX_HEADS * IDX_DIM
COL_QKV = 0
COL_KV = COL_QKV + ATTN_WIDTH
COL_QI = COL_KV + KV_W
COL_KI = COL_QKV + QKV_W
COL_Z = COL_KI + LANES
COL_XBC = COL_Z + SSD_D_INNER
COL_DTW = COL_XBC + CONV_DIM
W_CAT = COL_DTW + LANES
DTW_WI = SSD_N_HEADS

NEG_INF = float("-inf")
LOG2E = 1.4426950408889634
Q_SCALE = LOG2E * HEAD_DIM ** -0.5


def _rms(x, g):
    return x * lax.rsqrt(jnp.mean(x * x, axis=-1, keepdims=True) + EPS) * g


def _in_proj_kernel(x_ref, g_ref, w_ref, lnw_ref, lnb_ref,
                    q_ref, kv_ref, qi_ref, ki_ref, z_ref, xbc_ref, dtw_ref):
    u = _rms(x_ref[...], g_ref[...]).astype(BF16)

    def mm(lo, hi):
        return jnp.dot(u, w_ref[:, lo:hi], preferred_element_type=F32)

    q_ref[...] = (mm(COL_QKV, COL_KV) * Q_SCALE).astype(BF16)
    kv_ref[...] = mm(COL_KV, COL_QI).astype(BF16)
    qi_ref[...] = mm(COL_QI, COL_KI).astype(BF16)
    ki = mm(COL_KI, COL_Z)[:, :IDX_DIM]
    mu = jnp.mean(ki, axis=-1, keepdims=True)
    var = jnp.mean(jnp.square(ki - mu), axis=-1, keepdims=True)
    ki_ref[...] = ((ki - mu) * lax.rsqrt(var + EPS) * lnw_ref[...] + lnb_ref[...]).astype(BF16)
    z_ref[...] = mm(COL_Z, COL_XBC)
    xbc_ref[...] = mm(COL_XBC, COL_DTW)
    dtw_ref[...] = mm(COL_DTW, W_CAT)


def _in_proj(x2, g, w_cat, lnw, lnb, tm):
    n = x2.shape[0]
    row = lambda i: (i, 0)
    const = lambda i: (0, 0)
    return pl.pallas_call(
        _in_proj_kernel,
        grid=(n // tm,),
        in_specs=[
            pl.BlockSpec((tm, D_MODEL), row),
            pl.BlockSpec((1, D_MODEL), const),
            pl.BlockSpec((D_MODEL, W_CAT), const, pipeline_mode=pl.Buffered(1)),
            pl.BlockSpec((1, IDX_DIM), const),
            pl.BlockSpec((1, IDX_DIM), const),
        ],
        out_specs=[
            pl.BlockSpec((tm, ATTN_WIDTH), row),
            pl.BlockSpec((tm, KV_W), row),
            pl.BlockSpec((tm, QI_W), row),
            pl.BlockSpec((tm, IDX_DIM), row),
            pl.BlockSpec((tm, SSD_D_INNER), row),
            pl.BlockSpec((tm, CONV_DIM), row),
            pl.BlockSpec((tm, LANES), row),
        ],
        out_shape=[
            jax.ShapeDtypeStruct((n, ATTN_WIDTH), BF16),
            jax.ShapeDtypeStruct((n, KV_W), BF16),
            jax.ShapeDtypeStruct((n, QI_W), BF16),
            jax.ShapeDtypeStruct((n, IDX_DIM), BF16),
            jax.ShapeDtypeStruct((n, SSD_D_INNER), F32),
            jax.ShapeDtypeStruct((n, CONV_DIM), F32),
            jax.ShapeDtypeStruct((n, LANES), F32),
        ],
        compiler_params=pltpu.CompilerParams(
            dimension_semantics=("arbitrary",), vmem_limit_bytes=VMEM_LIMIT),
        name="in_proj",
    )(x2, g, w_cat, lnw, lnb)


def _bias_kernel(tbl_ref, out_ref):
    sk = lax.broadcasted_iota(I32, (TILE, TILE), 0)
    tq = lax.broadcasted_iota(I32, (TILE, TILE), 1)
    max_exact = NUM_BUCKETS // 2
    for off in range(2):
        dist = jnp.maximum(off * TILE + tq - sk, 0)
        df = jnp.maximum(dist, 1).astype(F32)
        large = max_exact + (jnp.log(df / max_exact) / math.log(MAX_DISTANCE / max_exact)
                             * (NUM_BUCKETS - max_exact)).astype(I32)
        large = jnp.minimum(large, NUM_BUCKETS - 1)
        bucket = jnp.where(dist < max_exact, dist, large)
        for h in range(N_ATTN_HEADS):
            acc = jnp.zeros((TILE, TILE), F32)
            for b in range(NUM_BUCKETS):
                acc = jnp.where(bucket == b, tbl_ref[b, h], acc)
            out_ref[h, off] = (acc - tbl_ref[NUM_BUCKETS - 1, h]) * LOG2E
    for h in range(N_ATTN_HEADS):
        out_ref[h, 2] = jnp.zeros((TILE, TILE), F32)


def _bias_tiles(rel_bias):
    return pl.pallas_call(
        _bias_kernel,
        in_specs=[pl.BlockSpec(memory_space=pltpu.SMEM)],
        out_specs=pl.BlockSpec(memory_space=pltpu.VMEM),
        out_shape=jax.ShapeDtypeStruct((N_ATTN_HEADS, 3, TILE, TILE), F32),
        name="bias_tiles",
    )(rel_bias)


def _key_to_f32(u):
    ks = u ^ I32(-2 ** 31)
    bits = jnp.where(ks >= 0, ks, ks ^ I32(0x7FFFFFFF))
    return lax.bitcast_convert_type(bits, F32)


PAIR = 2 * TILE
QUAD = 4 * TILE
QT = 2 * TILE
HALF_BITS = 16
HALF_MASK = 2 ** HALF_BITS - 1
HALF_BIAS = 2 ** (HALF_BITS - 1)
I16 = jnp.int16
PV_ROWS = HEAD_DIM + 16


def _attn_kernel(q_ref, kv_ref, qi_ref, ki_ref, dtw_ref, bias_ref, o_ref,
                 sc_ref, hi_ref, lo_ref, rk_ref, vt_ref, lg_ref, *, seq_len, n_sel):
    j = pl.program_id(1)
    i_hi = j * (QT // TILE) + QT // TILE - 1
    npair = i_hi // 2 + 1

    def quad_keys(v):
        return [QUAD] * (v // 2) + [PAIR] * (v % 2)

    rep = N_ATTN_HEADS // N_KV_HEADS
    gl = rep * QT
    nt_dims = (((1,), (1,)), ((), ()))
    v_lo = N_KV_HEADS * HEAD_DIM

    @pl.when(j == 0)
    def _():
        vt = kv_ref[0, :, v_lo:].astype(F32).T
        ones_row = jnp.where(lax.broadcasted_iota(I32, (PV_ROWS - HEAD_DIM, seq_len), 0) == 0, 1.0, 0.0)
        for g in range(N_KV_HEADS):
            vt_ref[g * PV_ROWS:g * PV_ROWS + HEAD_DIM, :] = vt[g * HEAD_DIM:(g + 1) * HEAD_DIM].astype(BF16)
            vt_ref[g * PV_ROWS + HEAD_DIM:(g + 1) * PV_ROWS, :] = ones_row.astype(BF16)

    w_t = dtw_ref[0].T
    idx_scale = (N_IDX_HEADS ** -0.5) * (IDX_DIM ** -0.5)
    w_rows = [w_t[DTW_WI + h:DTW_WI + h + 1, :] * idx_scale for h in range(N_IDX_HEADS)]
    qi = qi_ref[0]
    qi_all = jnp.concatenate(
        [qi[:, h * IDX_DIM:(h + 1) * IDX_DIM] for h in range(N_IDX_HEADS)], axis=0)
    s_loc = lax.broadcasted_iota(I32, (PAIR, QT), 0)
    s_minus_t = s_loc - lax.broadcasted_iota(I32, (PAIR, QT), 1)

    def score_quads(v):
        for c, nk in enumerate(quad_keys(v)):
            kt = ki_ref[0, c * QUAD:c * QUAD + nk, :]
            d = lax.dot_general(kt, qi_all, nt_dims, preferred_element_type=F32)
            s = jnp.zeros((nk, QT), F32)
            for h in range(N_IDX_HEADS):
                s = s + w_rows[h] * jnp.maximum(d[:, h * QT:(h + 1) * QT], 0.0)
            for u in range(nk // PAIR):
                pr = c * (QUAD // PAIR) + u
                su = jnp.where(s_minus_t <= j * QT - pr * PAIR, s[u * PAIR:(u + 1) * PAIR], NEG_INF)
                sc_ref[pr] = su
                bits = lax.bitcast_convert_type(su, I32)
                key = bits ^ (lax.shift_right_arithmetic(bits, I32(31)) & I32(0x7FFFFFFF))
                hi_ref[pr] = lax.shift_right_arithmetic(key, I32(HALF_BITS)).astype(I16)
                lo_ref[pr] = (key ^ I32(HALF_BIAS)).astype(I16)

    for v in range(1, seq_len // PAIR + 1):
        pl.when(npair == v)(functools.partial(score_quads, v))

    t_glob = j * QT + lax.broadcasted_iota(I32, (1, QT), 1)
    k_eff = jnp.minimum(n_sel, t_glob + 1)
    acc_rows = 4 * 2 * SUBLANES
    one16, zero16 = jnp.ones((), I16), jnp.zeros((), I16)

    def fold(cnt, rows=acc_rows):
        parts = [cnt[k * rows:(k + 1) * rows] for k in range(cnt.shape[0] // rows)]
        while len(parts) > 1:
            parts = [a + b for a, b in zip(parts[::2], parts[1::2])]
        return parts[0]

    def total(acc):
        packed = fold(acc, 2 * SUBLANES).astype(I32)
        return (packed[:SUBLANES] + packed[SUBLANES:]).sum(axis=0, keepdims=True)

    def count_ge(plane_ref, cand16):
        def body(c, acc):
            return acc + fold(jnp.where(plane_ref[c] >= cand16, one16, zero16))
        return total(lax.fori_loop(0, npair, body, jnp.zeros((acc_rows, QT), I16)))

    def to16(u):
        return (u - I32(HALF_BIAS)).astype(I16)

    def search_step(b, carry, plane_ref, base):
        prefix, cnt_ge = carry
        cand = prefix | lax.shift_left(I32(1), HALF_BITS - 1 - b)
        cnt = count_ge(plane_ref, to16(cand)) + base
        ok = cnt >= k_eff
        return jnp.where(ok, cand, prefix), jnp.where(ok, cnt, cnt_ge)

    carry = (jnp.zeros((1, QT), I32), t_glob + 1)
    hi_u, cnt_ge = lax.fori_loop(
        0, HALF_BITS, functools.partial(search_step, plane_ref=hi_ref, base=0), carry)
    hi16 = to16(hi_u)

    def narrow_body(c, acc):
        h = hi_ref[c]
        lo_ref[c] = jnp.where(h == hi16, lo_ref[c], I16(-HALF_BIAS))
        return acc + fold(jnp.where(h > hi16, one16, zero16))

    above = total(lax.fori_loop(0, npair, narrow_body, jnp.zeros((acc_rows, QT), I16)))
    lo_u, cnt_ge = lax.fori_loop(
        0, HALF_BITS, functools.partial(search_step, plane_ref=lo_ref, base=above),
        (jnp.zeros((1, QT), I32), cnt_ge))
    thr = _key_to_f32(lax.shift_left(hi_u, I32(HALF_BITS)) | lo_u)

    q = q_ref[0]
    qgs = [jnp.concatenate(
        [q[:, (g * rep + r) * HEAD_DIM:(g * rep + r + 1) * HEAD_DIM] for r in range(rep)],
        axis=0) for g in range(N_KV_HEADS)]
    below_diag = (lax.broadcasted_iota(I32, (PAIR, PAIR), 0)
                  > lax.broadcasted_iota(I32, (PAIR, PAIR), 1)).astype(BF16)
    tiles_per_quad = QUAD // TILE
    pairs_per_quad = QUAD // PAIR
    n_qt = QT // TILE
    int_min = I32(-2 ** 31)

    def bias_rows(c, g, u):
        cols = []
        for r in range(rep):
            for w in range(n_qt):
                off = jnp.clip(j * n_qt + w - (c * tiles_per_quad + u), 0, 2)
                cols.append(bias_ref[g * rep + r, off])
        return jnp.concatenate(cols, axis=1)

    def tied(s):
        eq = jnp.where(s == thr, 1.0, 0.0)
        neg0 = eq * jnp.where(lax.bitcast_convert_type(s, I32) == int_min, 1.0, 0.0)
        return eq - neg0, neg0

    def attend(v):
        tots = []
        for pr in range(v):
            e2 = jnp.concatenate(tied(sc_ref[pr]), axis=1).astype(BF16)
            r = jnp.dot(below_diag, e2, preferred_element_type=F32)
            rk_ref[pr] = r
            tots.append(r[PAIR - 1:PAIR] + e2[PAIR - 1:PAIR].astype(F32))
        tot = tots[0]
        for t in tots[1:]:
            tot = tot + t
        need = (k_eff - cnt_ge).astype(F32) + tot[:, :QT] + tot[:, QT:]
        off = jnp.concatenate([jnp.zeros((1, QT), F32), tot[:, :QT]], axis=1)

        m8 = [jnp.full((SUBLANES, gl), NEG_INF, F32) for _ in range(N_KV_HEADS)]
        nks = quad_keys(v)
        for c, nk in enumerate(nks):
            masks = []
            for u in range(nk // PAIR):
                pr = c * pairs_per_quad + u
                s = sc_ref[pr]
                r = rk_ref[pr] + off
                off = off + tots[pr]
                neg0 = lax.bitcast_convert_type(s, I32) == int_min
                keep_tie = jnp.where(jnp.where(neg0, r[:, QT:], r[:, :QT]) < need, 0.0, NEG_INF)
                masks.append(jnp.where(s > thr, 0.0, jnp.where(s == thr, keep_tie, NEG_INF)))
            mask = jnp.concatenate(masks, axis=0)
            mask = jnp.concatenate([mask] * rep, axis=1)
            keys = slice(c * QUAD, c * QUAD + nk)
            for g in range(N_KV_HEADS):
                kt = kv_ref[0, keys, g * HEAD_DIM:(g + 1) * HEAD_DIM]
                lg = lax.dot_general(kt, qgs[g], nt_dims, preferred_element_type=F32) + mask
                if c == len(nks) - 1:
                    lg = lg + jnp.concatenate([bias_rows(c, g, u) for u in range(nk // TILE)], axis=0)
                elif c == len(nks) - 2 and v % 2 == 1:
                    u = tiles_per_quad - 1
                    lg = jnp.concatenate([lg[:u * TILE], lg[u * TILE:] + bias_rows(c, g, u)], axis=0)
                lg_ref[c, g, :nk] = lg
                m8[g] = jnp.maximum(m8[g], lg.reshape(nk // SUBLANES, SUBLANES, gl).max(axis=0))

        outs = []
        for g in range(N_KV_HEADS):
            m_row = jnp.max(m8[g], axis=0, keepdims=True)
            acc = jnp.zeros((PV_ROWS, gl), F32)
            for c, nk in enumerate(nks):
                p = jnp.exp2((lg_ref[c, g, :nk] - m_row).astype(BF16))
                vt = vt_ref[g * PV_ROWS:(g + 1) * PV_ROWS, c * QUAD:c * QUAD + nk]
                acc = acc + jnp.dot(vt, p, preferred_element_type=F32)
            o_g = acc[:HEAD_DIM] / acc[HEAD_DIM:HEAD_DIM + 1]
            outs += [o_g[:, r * QT:(r + 1) * QT] for r in range(rep)]
        o_ref[0] = jnp.concatenate(outs, axis=0).T.astype(BF16)

    for v in range(1, seq_len // PAIR + 1):
        pl.when(npair == v)(functools.partial(attend, v))


def _attention(q3, kv3, qi3, ki3, dtw3, bias, n_sel):
    bsz, seq_len, _ = q3.shape
    assert seq_len % QUAD == 0 and QUAD % QT == 0
    rep = N_ATTN_HEADS // N_KV_HEADS
    kern = functools.partial(_attn_kernel, seq_len=seq_len, n_sel=n_sel)
    return pl.pallas_call(
        kern,
        grid=(bsz, seq_len // QT),
        in_specs=[
            pl.BlockSpec((1, QT, ATTN_WIDTH), lambda b, j: (b, j, 0)),
            pl.BlockSpec((1, seq_len, KV_W), lambda b, j: (b, 0, 0)),
            pl.BlockSpec((1, QT, QI_W), lambda b, j: (b, j, 0)),
            pl.BlockSpec((1, seq_len, IDX_DIM), lambda b, j: (b, 0, 0)),
            pl.BlockSpec((1, QT, LANES), lambda b, j: (b, j, 0)),
            pl.BlockSpec((N_ATTN_HEADS, 3, TILE, TILE), lambda b, j: (0, 0, 0, 0)),
        ],
        out_specs=pl.BlockSpec((1, QT, ATTN_WIDTH), lambda b, j: (b, j, 0)),
        out_shape=jax.ShapeDtypeStruct((bsz, seq_len, ATTN_WIDTH), BF16),
        scratch_shapes=[
            pltpu.VMEM((seq_len // PAIR, PAIR, QT), F32),
            pltpu.VMEM((seq_len // PAIR, PAIR, QT), I16),
            pltpu.VMEM((seq_len // PAIR, PAIR, QT), I16),
            pltpu.VMEM((seq_len // PAIR, PAIR, 2 * QT), F32),
            pltpu.VMEM((N_KV_HEADS * PV_ROWS, seq_len), BF16),
            pltpu.VMEM((seq_len // QUAD, N_KV_HEADS, QUAD, rep * QT), F32),
        ],
        compiler_params=pltpu.CompilerParams(
            dimension_semantics=("arbitrary", "arbitrary"), vmem_limit_bytes=VMEM_LIMIT),
        name="sparse_attn",
    )(q3, kv3, qi3, ki3, dtw3, bias)


SSD_STEP_CHUNKS = 2
SSD_STEP = SSD_STEP_CHUNKS * TILE


def _split_bf16(v, n):
    parts = []
    for _ in range(n):
        p = v.astype(BF16)
        parts.append(p)
        v = v - p.astype(F32)
    return parts


def _dot_01(v, mat01, n):
    out = None
    for p in _split_bf16(v, n):
        t = jnp.dot(p, mat01, preferred_element_type=F32)
        out = t if out is None else out + t
    return out


def _ssd_kernel(z_ref, xbc_ref, dtw_ref, cw_ref, cb_ref, dtb_ref, alog_ref, dsk_ref, nw_ref,
                o_ref, xpad_ref, st_ref):
    step = pl.program_id(1)
    gn = SSD_N_GROUPS * SSD_D_STATE
    hpg = SSD_N_HEADS // SSD_N_GROUPS
    gw = hpg * SSD_HEAD_DIM

    @pl.when(step == 0)
    def _():
        xpad_ref[0:SUBLANES, :] = jnp.zeros((SUBLANES, CONV_DIM), F32)
        st_ref[...] = jnp.zeros_like(st_ref)

    xpad_ref[SUBLANES:, :] = xbc_ref[0]
    xfull = xpad_ref[...]
    conv = cb_ref[...] + cw_ref[CONV_WIDTH - 1:CONV_WIDTH, :] * xfull[SUBLANES:]
    for k in range(CONV_WIDTH - 1):
        shifted = pltpu.roll(xfull, CONV_WIDTH - 1 - k, 0)
        conv = conv + cw_ref[k:k + 1, :] * shifted[SUBLANES:]
    xpad_ref[0:SUBLANES, :] = xfull[SSD_STEP:]
    act = conv * (1.0 / (1.0 + jnp.exp(-conv)))
    xs = act[:, :SSD_D_INNER]
    bm = act[:, SSD_D_INNER:SSD_D_INNER + gn].astype(BF16)
    cm = act[:, SSD_D_INNER + gn:].astype(BF16)

    dt_in = dtw_ref[0] + dtb_ref[...]
    dt = jnp.maximum(dt_in, 0.0) + jnp.log1p(jnp.exp(-jnp.abs(dt_in)))
    adt = dt * (-jnp.exp(alog_ref[...]))
    row = lax.broadcasted_iota(I32, (SSD_STEP, SSD_STEP), 0)
    col = lax.broadcasted_iota(I32, (SSD_STEP, SSD_STEP), 1)
    same_chunk = (row // TILE) == (col // TILE)
    chunk_tril = jnp.where(same_chunk & (row >= col), 1.0, 0.0).astype(BF16)
    acs = None
    for p in _split_bf16(adt, 3):
        t = jnp.dot(chunk_tril, p, preferred_element_type=F32)
        acs = t if acs is None else acs + t
    acs_t = acs.T
    causal = (lax.broadcasted_iota(I32, (TILE, TILE), 0) >= lax.broadcasted_iota(I32, (TILE, TILE), 1))
    a_last = [acs[(c + 1) * TILE - 1:(c + 1) * TILE, :] for c in range(SSD_STEP_CHUNKS)]
    out_decay = jnp.concatenate(
        [jnp.exp(a_last[c] - acs[c * TILE:(c + 1) * TILE]) for c in range(SSD_STEP_CHUNKS)], axis=0)
    chunk_decay = jnp.concatenate(
        [jnp.exp(a) for a in a_last] + [jnp.zeros((SUBLANES - SSD_STEP_CHUNKS, LANES), F32)], axis=0)
    hsel = (lax.broadcasted_iota(I32, (LANES, SSD_D_INNER), 0)
            == lax.broadcasted_iota(I32, (LANES, SSD_D_INNER), 1) // SSD_HEAD_DIM).astype(BF16)
    expanded = _dot_01(jnp.concatenate([dt, jnp.exp(acs), out_decay, chunk_decay], axis=0), hsel, 2)
    dt_x = expanded[:SSD_STEP]
    in_decay_x = expanded[SSD_STEP:2 * SSD_STEP]
    out_decay_x = expanded[2 * SSD_STEP:3 * SSD_STEP]
    chunk_decay_x = expanded[3 * SSD_STEP:]

    x_dt = xs * dt_x
    x_dt_b = x_dt.astype(BF16)
    x_out_b = (x_dt * out_decay_x).astype(BF16)
    tn_dims = (((0,), (0,)), ((), ()))
    nt_dims = (((1,), (1,)), ((), ()))
    states = [st_ref[g] for g in range(SSD_N_GROUPS)]
    y_chunks = []
    for c in range(SSD_STEP_CHUNKS):
        rows = slice(c * TILE, (c + 1) * TILE)
        y_parts = []
        for g in range(SSD_N_GROUPS):
            bg = bm[rows, g * SSD_D_STATE:(g + 1) * SSD_D_STATE]
            cg = cm[rows, g * SSD_D_STATE:(g + 1) * SSD_D_STATE]
            lanes = slice(g * gw, (g + 1) * gw)
            cb = lax.dot_general(cg, bg, nt_dims, preferred_element_type=F32)
            y_diag = []
            for r in range(hpg):
                h = g * hpg + r
                seg = jnp.where(causal, acs[rows, h:h + 1] - acs_t[h:h + 1, rows], NEG_INF)
                w = (cb * jnp.exp(seg)).astype(BF16)
                y_diag.append(jnp.dot(w, x_dt_b[rows, h * SSD_HEAD_DIM:(h + 1) * SSD_HEAD_DIM],
                                      preferred_element_type=F32))
            y_off = (jnp.dot(cg, states[g].astype(BF16), preferred_element_type=F32)
                     * in_decay_x[rows, lanes])
            y_parts.append(jnp.concatenate(y_diag, axis=-1) + y_off)
            st_new = lax.dot_general(bg, x_out_b[rows, lanes], tn_dims, preferred_element_type=F32)
            states[g] = states[g] * chunk_decay_x[c:c + 1, lanes] + st_new
        y_chunks.append(jnp.concatenate(y_parts, axis=-1))
    for g in range(SSD_N_GROUPS):
        st_ref[g] = states[g]

    y = jnp.concatenate(y_chunks, axis=0) + xs * dsk_ref[...]
    zv = z_ref[0]
    y = y * (zv * (1.0 / (1.0 + jnp.exp(-zv))))
    ng = SSD_D_INNER // SSD_N_GROUPS
    outs = []
    for g in range(SSD_N_GROUPS):
        yg = y[:, g * ng:(g + 1) * ng]
        outs.append(yg * lax.rsqrt(jnp.mean(yg * yg, axis=-1, keepdims=True) + EPS))
    o_ref[0] = (jnp.concatenate(outs, axis=-1) * nw_ref[...]).astype(BF16)


def _ssd(z3, xbc3, dtw3, conv_w, conv_b, dt_bias, a_log, d_skip, norm_w):
    bsz, seq_len, _ = z3.shape
    assert seq_len % SSD_STEP == 0
    nc = seq_len // SSD_STEP
    blk = lambda w: pl.BlockSpec((1, SSD_STEP, w), lambda b, c: (b, c, 0))
    par = lambda r, w: pl.BlockSpec((r, w), lambda b, c: (0, 0))
    return pl.pallas_call(
        _ssd_kernel,
        grid=(bsz, nc),
        in_specs=[blk(SSD_D_INNER), blk(CONV_DIM), blk(LANES),
                  par(CONV_WIDTH, CONV_DIM), par(1, CONV_DIM), par(1, LANES), par(1, LANES),
                  par(1, SSD_D_INNER), par(1, SSD_D_INNER)],
        out_specs=blk(SSD_D_INNER),
        out_shape=jax.ShapeDtypeStruct((bsz, seq_len, SSD_D_INNER), BF16),
        scratch_shapes=[
            pltpu.VMEM((SUBLANES + SSD_STEP, CONV_DIM), F32),
            pltpu.VMEM((SSD_N_GROUPS, SSD_D_STATE, SSD_D_INNER // SSD_N_GROUPS), F32),
        ],
        compiler_params=pltpu.CompilerParams(
            dimension_semantics=("arbitrary", "arbitrary"), vmem_limit_bytes=VMEM_LIMIT),
        name="ssd_mixer",
    )(z3, xbc3, dtw3, conv_w, conv_b, dt_bias, a_log, d_skip, norm_w)


FF_CHUNK = 512


def _mlp_kernel(x_ref, attn_ref, ssd_ref, wo_ref, g1_ref, g2_ref, wu_ref, wd_ref, g3_ref, o_ref):
    mix = (jnp.dot(attn_ref[...], wo_ref[:ATTN_WIDTH, :], preferred_element_type=F32)
           + jnp.dot(ssd_ref[...], wo_ref[ATTN_WIDTH:, :], preferred_element_type=F32))
    h1 = x_ref[...] + _rms(mix, g1_ref[...])
    u = _rms(h1, g2_ref[...]).astype(BF16)
    acc = jnp.zeros(h1.shape, F32)
    for c in range(0, D_FF, FF_CHUNK):
        f = jnp.dot(u, wu_ref[:, c:c + FF_CHUNK], preferred_element_type=F32)
        f = jnp.square(jnp.maximum(f, 0.0)).astype(BF16)
        acc = acc + jnp.dot(f, wd_ref[c:c + FF_CHUNK, :], preferred_element_type=F32)
    o_ref[...] = h1 + _rms(acc, g3_ref[...])


def _mlp(x2, attn2, ssd2, wo, g1, g2, wu, wd, g3, tm):
    n = x2.shape[0]
    row = lambda i: (i, 0)
    const = lambda i: (0, 0)
    single = dict(pipeline_mode=pl.Buffered(1))
    return pl.pallas_call(
        _mlp_kernel,
        grid=(n // tm,),
        in_specs=[
            pl.BlockSpec((tm, D_MODEL), row),
            pl.BlockSpec((tm, ATTN_WIDTH), row),
            pl.BlockSpec((tm, SSD_D_INNER), row),
            pl.BlockSpec((ATTN_WIDTH + SSD_D_INNER, D_MODEL), const, **single),
            pl.BlockSpec((1, D_MODEL), const),
            pl.BlockSpec((1, D_MODEL), const),
            pl.BlockSpec((D_MODEL, D_FF), const, **single),
            pl.BlockSpec((D_FF, D_MODEL), const, **single),
            pl.BlockSpec((1, D_MODEL), const),
        ],
        out_specs=pl.BlockSpec((tm, D_MODEL), row),
        out_shape=jax.ShapeDtypeStruct((n, D_MODEL), F32),
        compiler_params=pltpu.CompilerParams(
            dimension_semantics=("arbitrary",), vmem_limit_bytes=VMEM_LIMIT),
        name="out_proj_mlp",
    )(x2, attn2, ssd2, wo, g1, g2, wu, wd, g3)


def _pack_w_in(w):
    w = w.astype(BF16)
    o = 0
    parts = {}
    for name, width in (("qkvqi", QKV_W), ("ki", IDX_DIM), ("wi", N_IDX_HEADS),
                        ("z", SSD_D_INNER), ("xbc", CONV_DIM), ("dt", SSD_N_HEADS)):
        parts[name] = w[:, o:o + width]
        o += width
    zeros = lambda n: jnp.zeros((w.shape[0], n), w.dtype)
    return jnp.concatenate([
        parts["qkvqi"], parts["ki"], zeros(LANES - IDX_DIM), parts["z"], parts["xbc"],
        parts["dt"], parts["wi"], zeros(LANES - SSD_N_HEADS - N_IDX_HEADS)], axis=1)


def _pad_lanes(v, n):
    return jnp.pad(v, (0, n - v.shape[0])).reshape(1, n)


def kernel(x, norm_pre_mix, norm_post_mix, norm_pre_mlp, norm_post_mlp, w_in, k_idx_ln_w, k_idx_ln_b, conv_w, conv_b, dt_bias, a_log, d_skip, ssd_norm_w, w_out, w_mlp_up, w_mlp_down, rel_bias):
    bsz, seq_len, d = x.shape
    n = bsz * seq_len
    assert d == D_MODEL and seq_len % TILE == 0
    tm = next(t for t in (1024, 512, TILE) if n % t == 0)
    n_sel = min(TOPK_MAX, seq_len // 4)
    bias = _bias_tiles(rel_bias)
    h = x.reshape(n, d)
    for i in range(norm_pre_mix.shape[0]):
        row = lambda v: v[i].reshape(1, -1)
        q, kv, qi, ki, z, xbc, dtw = _in_proj(h, row(norm_pre_mix), _pack_w_in(w_in[i]),
                                              row(k_idx_ln_w), row(k_idx_ln_b), tm)
        r3 = lambda a: a.reshape(bsz, seq_len, a.shape[-1])
        dtw3 = r3(dtw)
        attn = _attention(r3(q), r3(kv), r3(qi), r3(ki), dtw3, bias, n_sel)
        ssd = _ssd(r3(z), r3(xbc), dtw3, conv_w[i], row(conv_b),
                   _pad_lanes(dt_bias[i], LANES), _pad_lanes(a_log[i], LANES),
                   jnp.repeat(d_skip[i], SSD_HEAD_DIM).reshape(1, -1), row(ssd_norm_w))
        h = _mlp(h, attn.reshape(n, -1), ssd.reshape(n, -1), w_out[i].astype(BF16),
                 row(norm_post_mix), row(norm_pre_mlp), w_mlp_up[i].astype(BF16),
                 w_mlp_down[i].astype(BF16), row(norm_post_mlp), tm)
    return h.reshape(bsz, seq_len, d)
```

```python
import functools
import math

import jax
import jax.numpy as jnp
from jax import lax
from jax.experimental import pallas as pl
from jax.experimental.pallas import tpu as pltpu

F32 = jnp.float32
BF16 = jnp.bfloat16
I32 = jnp.int32

D_MODEL = 1024
N_ATTN_HEADS = 8
N_KV_HEADS = 2
HEAD_DIM = 64
ATTN_WIDTH = N_ATTN_HEADS * HEAD_DIM
N_IDX_HEADS = 4
IDX_DIM = 64
TOPK_MAX = 256
NUM_BUCKETS = 32
MAX_DISTANCE = 128
SSD_D_INNER = 512
SSD_HEAD_DIM = 64
SSD_N_HEADS = 8
SSD_N_GROUPS = 2
SSD_D_STATE = 128
CONV_WIDTH = 4
CONV_DIM = SSD_D_INNER + 2 * SSD_N_GROUPS * SSD_D_STATE
D_FF = 4 * D_MODEL
EPS = 1e-6

LANES = 128
SUBLANES = 8
TILE = 128
VMEM_LIMIT = 56 * 1024 * 1024

QKV_W = ATTN_WIDTH + 2 * N_KV_HEADS * HEAD_DIM + N_IDX_HEADS * IDX_DIM
KV_W = 2 * N_KV_HEADS * HEAD_DIM
QI_W = N_IDX_HEADS * IDX_DIM
COL_QKV = 0
COL_KV = COL_QKV + ATTN_WIDTH
COL_QI = COL_KV + KV_W
COL_KI = COL_QKV + QKV_W
COL_Z = COL_KI + LANES
COL_XBC = COL_Z + SSD_D_INNER
COL_DTW = COL_XBC + CONV_DIM
W_CAT = COL_DTW + LANES
DTW_WI = SSD_N_HEADS
SRC_QKV = 0
SRC_KI = SRC_QKV + QKV_W
SRC_WI = SRC_KI + IDX_DIM
SRC_Z = SRC_WI + N_IDX_HEADS
SRC_XBC = SRC_Z + SSD_D_INNER
SRC_DT = SRC_XBC + CONV_DIM

NEG_INF = float("-inf")
LOG2E = 1.4426950408889634
Q_SCALE = LOG2E * HEAD_DIM ** -0.5


def _rms(x, g):
    return x * lax.rsqrt(jnp.mean(x * x, axis=-1, keepdims=True) + EPS) * g


W_PACK_ROWS = 128


def _in_proj_kernel(x_ref, g_ref, win_ref, lnw_ref, lnb_ref,
                    q_ref, kv_ref, qi_ref, ki_ref, z_ref, xbc_ref, dtw_ref, w_ref):
    @pl.when(pl.program_id(0) == 0)
    def _():
        def pack_rows(r, carry):
            rows = pl.ds(pl.multiple_of(r * W_PACK_ROWS, W_PACK_ROWS), W_PACK_ROWS)
            src = lambda lo, width: win_ref[rows, lo:lo + width]
            zeros = lambda width: jnp.zeros((W_PACK_ROWS, width), F32)
            w_ref[rows, COL_QKV:COL_KI] = src(SRC_QKV, QKV_W).astype(BF16)
            w_ref[rows, COL_KI:COL_Z] = jnp.concatenate(
                [src(SRC_KI, IDX_DIM), zeros(LANES - IDX_DIM)], axis=1).astype(BF16)
            w_ref[rows, COL_Z:COL_XBC] = src(SRC_Z, SSD_D_INNER).astype(BF16)
            w_ref[rows, COL_XBC:COL_DTW] = src(SRC_XBC, CONV_DIM).astype(BF16)
            w_ref[rows, COL_DTW:W_CAT] = jnp.concatenate(
                [src(SRC_DT, SSD_N_HEADS), src(SRC_WI, N_IDX_HEADS),
                 zeros(LANES - SSD_N_HEADS - N_IDX_HEADS)], axis=1).astype(BF16)
            return carry

        lax.fori_loop(0, D_MODEL // W_PACK_ROWS, pack_rows, 0)

    u = _rms(x_ref[...], g_ref[...]).astype(BF16)

    def mm(lo, hi):
        return jnp.dot(u, w_ref[:, lo:hi], preferred_element_type=F32)

    q_ref[...] = (mm(COL_QKV, COL_KV) * Q_SCALE).astype(BF16)
    kv_ref[...] = mm(COL_KV, COL_QI).astype(BF16)
    qi_ref[...] = mm(COL_QI, COL_KI).astype(BF16)
    ki = mm(COL_KI, COL_Z)[:, :IDX_DIM]
    mu = jnp.mean(ki, axis=-1, keepdims=True)
    var = jnp.mean(jnp.square(ki - mu), axis=-1, keepdims=True)
    ki_ref[...] = ((ki - mu) * lax.rsqrt(var + EPS) * lnw_ref[...] + lnb_ref[...]).astype(BF16)
    z_ref[...] = mm(COL_Z, COL_XBC)
    xbc_ref[...] = mm(COL_XBC, COL_DTW)
    dtw_ref[...] = mm(COL_DTW, W_CAT)


def _in_proj(x2, g, w_in, lnw, lnb, tm):
    n = x2.shape[0]
    assert w_in.shape == (D_MODEL, SRC_DT + SSD_N_HEADS)
    row = lambda i: (i, 0)
    const = lambda i: (0, 0)
    return pl.pallas_call(
        _in_proj_kernel,
        grid=(n // tm,),
        in_specs=[
            pl.BlockSpec((tm, D_MODEL), row),
            pl.BlockSpec((1, D_MODEL), const),
            pl.BlockSpec(w_in.shape, const, pipeline_mode=pl.Buffered(1)),
            pl.BlockSpec((1, IDX_DIM), const),
            pl.BlockSpec((1, IDX_DIM), const),
        ],
        out_specs=[
            pl.BlockSpec((tm, ATTN_WIDTH), row),
            pl.BlockSpec((tm, KV_W), row),
            pl.BlockSpec((tm, QI_W), row),
            pl.BlockSpec((tm, IDX_DIM), row),
            pl.BlockSpec((tm, SSD_D_INNER), row),
            pl.BlockSpec((tm, CONV_DIM), row),
            pl.BlockSpec((tm, LANES), row),
        ],
        out_shape=[
            jax.ShapeDtypeStruct((n, ATTN_WIDTH), BF16),
            jax.ShapeDtypeStruct((n, KV_W), BF16),
            jax.ShapeDtypeStruct((n, QI_W), BF16),
            jax.ShapeDtypeStruct((n, IDX_DIM), BF16),
            jax.ShapeDtypeStruct((n, SSD_D_INNER), F32),
            jax.ShapeDtypeStruct((n, CONV_DIM), F32),
            jax.ShapeDtypeStruct((n, LANES), F32),
        ],
        scratch_shapes=[pltpu.VMEM((D_MODEL, W_CAT), BF16)],
        compiler_params=pltpu.CompilerParams(
            dimension_semantics=("arbitrary",), vmem_limit_bytes=VMEM_LIMIT),
        name="in_proj",
    )(x2, g, w_in, lnw, lnb)


def _bias_kernel(tbl_ref, out_ref):
    sk = lax.broadcasted_iota(I32, (TILE, TILE), 0)
    tq = lax.broadcasted_iota(I32, (TILE, TILE), 1)
    max_exact = NUM_BUCKETS // 2
    for off in range(2):
        dist = jnp.maximum(off * TILE + tq - sk, 0)
        df = jnp.maximum(dist, 1).astype(F32)
        large = max_exact + (jnp.log(df / max_exact) / math.log(MAX_DISTANCE / max_exact)
                             * (NUM_BUCKETS - max_exact)).astype(I32)
        large = jnp.minimum(large, NUM_BUCKETS - 1)
        bucket = jnp.where(dist < max_exact, dist, large)
        for h in range(N_ATTN_HEADS):
            acc = jnp.zeros((TILE, TILE), F32)
            for b in range(NUM_BUCKETS):
                acc = jnp.where(bucket == b, tbl_ref[b, h], acc)
            out_ref[h, off] = (acc - tbl_ref[NUM_BUCKETS - 1, h]) * LOG2E
    for h in range(N_ATTN_HEADS):
        out_ref[h, 2] = jnp.zeros((TILE, TILE), F32)


def _bias_tiles(rel_bias):
    return pl.pallas_call(
        _bias_kernel,
        in_specs=[pl.BlockSpec(memory_space=pltpu.SMEM)],
        out_specs=pl.BlockSpec(memory_space=pltpu.VMEM),
        out_shape=jax.ShapeDtypeStruct((N_ATTN_HEADS, 3, TILE, TILE), F32),
        name="bias_tiles",
    )(rel_bias)


def _key_to_f32(u):
    ks = u ^ I32(-2 ** 31)
    bits = jnp.where(ks >= 0, ks, ks ^ I32(0x7FFFFFFF))
    return lax.bitcast_convert_type(bits, F32)


PAIR = 2 * TILE
QUAD = 4 * TILE
QT = 2 * TILE
HALF_BITS = 16
HALF_MASK = 2 ** HALF_BITS - 1
HALF_BIAS = 2 ** (HALF_BITS - 1)
I16 = jnp.int16
PV_ROWS = HEAD_DIM + 16


def _attn_kernel(q_ref, kv_ref, qi_ref, ki_ref, dtw_ref, bias_ref, o_ref,
                 sc_ref, hi_ref, lo_ref, rk_ref, vt_ref, lg_ref, *, seq_len, n_sel):
    j = pl.program_id(1)
    i_hi = j * (QT // TILE) + QT // TILE - 1
    npair = i_hi // 2 + 1

    def quad_keys(v):
        return [QUAD] * (v // 2) + [PAIR] * (v % 2)

    rep = N_ATTN_HEADS // N_KV_HEADS
    gl = rep * QT
    nt_dims = (((1,), (1,)), ((), ()))
    v_lo = N_KV_HEADS * HEAD_DIM

    @pl.when(j == 0)
    def _():
        vt = kv_ref[0, :, v_lo:].astype(F32).T
        ones_row = jnp.where(lax.broadcasted_iota(I32, (PV_ROWS - HEAD_DIM, seq_len), 0) == 0, 1.0, 0.0)
        for g in range(N_KV_HEADS):
            vt_ref[g * PV_ROWS:g * PV_ROWS + HEAD_DIM, :] = vt[g * HEAD_DIM:(g + 1) * HEAD_DIM].astype(BF16)
            vt_ref[g * PV_ROWS + HEAD_DIM:(g + 1) * PV_ROWS, :] = ones_row.astype(BF16)

    w_t = dtw_ref[0].T
    idx_scale = (N_IDX_HEADS ** -0.5) * (IDX_DIM ** -0.5)
    w_rows = [w_t[DTW_WI + h:DTW_WI + h + 1, :] * idx_scale for h in range(N_IDX_HEADS)]
    qi = qi_ref[0]
    qi_all = jnp.concatenate(
        [qi[:, h * IDX_DIM:(h + 1) * IDX_DIM] for h in range(N_IDX_HEADS)], axis=0)
    s_loc = lax.broadcasted_iota(I32, (PAIR, QT), 0)
    s_minus_t = s_loc - lax.broadcasted_iota(I32, (PAIR, QT), 1)

    def score_quads(v):
        for c, nk in enumerate(quad_keys(v)):
            kt = ki_ref[0, c * QUAD:c * QUAD + nk, :]
            d = lax.dot_general(kt, qi_all, nt_dims, preferred_element_type=F32)
            s = jnp.zeros((nk, QT), F32)
            for h in range(N_IDX_HEADS):
                s = s + w_rows[h] * jnp.maximum(d[:, h * QT:(h + 1) * QT], 0.0)
            for u in range(nk // PAIR):
                pr = c * (QUAD // PAIR) + u
                su = jnp.where(s_minus_t <= j * QT - pr * PAIR, s[u * PAIR:(u + 1) * PAIR], NEG_INF)
                sc_ref[pr] = su
                bits = lax.bitcast_convert_type(su, I32)
                key = bits ^ (lax.shift_right_arithmetic(bits, I32(31)) & I32(0x7FFFFFFF))
                hi_ref[pr] = lax.shift_right_arithmetic(key, I32(HALF_BITS)).astype(I16)
                lo_ref[pr] = (key ^ I32(HALF_BIAS)).astype(I16)

    for v in range(1, seq_len // PAIR + 1):
        pl.when(npair == v)(functools.partial(score_quads, v))

    t_glob = j * QT + lax.broadcasted_iota(I32, (1, QT), 1)
    k_eff = jnp.minimum(n_sel, t_glob + 1)
    acc_rows = 4 * 2 * SUBLANES
    one16, zero16 = jnp.ones((), I16), jnp.zeros((), I16)

    def fold(cnt, rows=acc_rows):
        parts = [cnt[k * rows:(k + 1) * rows] for k in range(cnt.shape[0] // rows)]
        while len(parts) > 1:
            parts = [a + b for a, b in zip(parts[::2], parts[1::2])]
        return parts[0]

    def total(acc):
        packed = fold(acc, 2 * SUBLANES).astype(I32)
        return (packed[:SUBLANES] + packed[SUBLANES:]).sum(axis=0, keepdims=True)

    def count_ge(plane_ref, cand16):
        def body(c, acc):
            return acc + fold(jnp.where(plane_ref[c] >= cand16, one16, zero16))
        return total(lax.fori_loop(0, npair, body, jnp.zeros((acc_rows, QT), I16)))

    def to16(u):
        return (u - I32(HALF_BIAS)).astype(I16)

    def search_step(b, carry, plane_ref, base):
        prefix, cnt_ge = carry
        cand = prefix | lax.shift_left(I32(1), HALF_BITS - 1 - b)
        cnt = count_ge(plane_ref, to16(cand)) + base
        ok = cnt >= k_eff
        return jnp.where(ok, cand, prefix), jnp.where(ok, cnt, cnt_ge)

    carry = (jnp.zeros((1, QT), I32), t_glob + 1)
    hi_u, cnt_ge = lax.fori_loop(
        0, HALF_BITS, functools.partial(search_step, plane_ref=hi_ref, base=0), carry)
    hi16 = to16(hi_u)

    def narrow_body(c, acc):
        h = hi_ref[c]
        lo_ref[c] = jnp.where(h == hi16, lo_ref[c], I16(-HALF_BIAS))
        return acc + fold(jnp.where(h > hi16, one16, zero16))

    above = total(lax.fori_loop(0, npair, narrow_body, jnp.zeros((acc_rows, QT), I16)))
    lo_u, cnt_ge = lax.fori_loop(
        0, HALF_BITS, functools.partial(search_step, plane_ref=lo_ref, base=above),
        (jnp.zeros((1, QT), I32), cnt_ge))
    thr = _key_to_f32(lax.shift_left(hi_u, I32(HALF_BITS)) | lo_u)

    q = q_ref[0]
    qgs = [jnp.concatenate(
        [q[:, (g * rep + r) * HEAD_DIM:(g * rep + r + 1) * HEAD_DIM] for r in range(rep)],
        axis=0) for g in range(N_KV_HEADS)]
    below_diag = (lax.broadcasted_iota(I32, (PAIR, PAIR), 0)
                  > lax.broadcasted_iota(I32, (PAIR, PAIR), 1)).astype(BF16)
    tiles_per_quad = QUAD // TILE
    pairs_per_quad = QUAD // PAIR
    n_qt = QT // TILE
    int_min = I32(-2 ** 31)

    def bias_rows(c, g, u):
        cols = []
        for r in range(rep):
            for w in range(n_qt):
                off = jnp.clip(j * n_qt + w - (c * tiles_per_quad + u), 0, 2)
                cols.append(bias_ref[g * rep + r, off])
        return jnp.concatenate(cols, axis=1)

    def tied(s):
        eq = jnp.where(s == thr, 1.0, 0.0)
        neg0 = eq * jnp.where(lax.bitcast_convert_type(s, I32) == int_min, 1.0, 0.0)
        return eq - neg0, neg0

    def attend(v):
        tots = []
        for pr in range(v):
            e2 = jnp.concatenate(tied(sc_ref[pr]), axis=1).astype(BF16)
            r = jnp.dot(below_diag, e2, preferred_element_type=F32)
            rk_ref[pr] = r
            tots.append(r[PAIR - 1:PAIR] + e2[PAIR - 1:PAIR].astype(F32))
        tot = tots[0]
        for t in tots[1:]:
            tot = tot + t
        need = (k_eff - cnt_ge).astype(F32) + tot[:, :QT] + tot[:, QT:]
        off = jnp.concatenate([jnp.zeros((1, QT), F32), tot[:, :QT]], axis=1)

        m8 = [jnp.full((SUBLANES, gl), NEG_INF, F32) for _ in range(N_KV_HEADS)]
        nks = quad_keys(v)
        for c, nk in enumerate(nks):
            masks = []
            for u in range(nk // PAIR):
                pr = c * pairs_per_quad + u
                s = sc_ref[pr]
                r = rk_ref[pr] + off
                off = off + tots[pr]
                neg0 = lax.bitcast_convert_type(s, I32) == int_min
                keep_tie = jnp.where(jnp.where(neg0, r[:, QT:], r[:, :QT]) < need, 0.0, NEG_INF)
                masks.append(jnp.where(s > thr, 0.0, jnp.where(s == thr, keep_tie, NEG_INF)))
            mask = jnp.concatenate(masks, axis=0)
            mask = jnp.concatenate([mask] * rep, axis=1)
            keys = slice(c * QUAD, c * QUAD + nk)
            for g in range(N_KV_HEADS):
                kt = kv_ref[0, keys, g * HEAD_DIM:(g + 1) * HEAD_DIM]
                lg = lax.dot_general(kt, qgs[g], nt_dims, preferred_element_type=F32) + mask
                if c == len(nks) - 1:
                    lg = lg + jnp.concatenate([bias_rows(c, g, u) for u in range(nk // TILE)], axis=0)
                elif c == len(nks) - 2 and v % 2 == 1:
                    u = tiles_per_quad - 1
                    lg = jnp.concatenate([lg[:u * TILE], lg[u * TILE:] + bias_rows(c, g, u)], axis=0)
                lg_ref[c, g, :nk] = lg
                m8[g] = jnp.maximum(m8[g], lg.reshape(nk // SUBLANES, SUBLANES, gl).max(axis=0))

        outs = []
        for g in range(N_KV_HEADS):
            m_row = jnp.max(m8[g], axis=0, keepdims=True)
            acc = jnp.zeros((PV_ROWS, gl), F32)
            for c, nk in enumerate(nks):
                p = jnp.exp2((lg_ref[c, g, :nk] - m_row).astype(BF16))
                vt = vt_ref[g * PV_ROWS:(g + 1) * PV_ROWS, c * QUAD:c * QUAD + nk]
                acc = acc + jnp.dot(vt, p, preferred_element_type=F32)
            o_g = acc[:HEAD_DIM] / acc[HEAD_DIM:HEAD_DIM + 1]
            outs += [o_g[:, r * QT:(r + 1) * QT] for r in range(rep)]
        o_ref[0] = jnp.concatenate(outs, axis=0).T.astype(BF16)

    for v in range(1, seq_len // PAIR + 1):
        pl.when(npair == v)(functools.partial(attend, v))


def _attention(q3, kv3, qi3, ki3, dtw3, bias, n_sel):
    bsz, seq_len, _ = q3.shape
    assert seq_len % QUAD == 0 and QUAD % QT == 0
    rep = N_ATTN_HEADS // N_KV_HEADS
    kern = functools.partial(_attn_kernel, seq_len=seq_len, n_sel=n_sel)
    return pl.pallas_call(
        kern,
        grid=(bsz, seq_len // QT),
        in_specs=[
            pl.BlockSpec((1, QT, ATTN_WIDTH), lambda b, j: (b, j, 0)),
            pl.BlockSpec((1, seq_len, KV_W), lambda b, j: (b, 0, 0)),
            pl.BlockSpec((1, QT, QI_W), lambda b, j: (b, j, 0)),
            pl.BlockSpec((1, seq_len, IDX_DIM), lambda b, j: (b, 0, 0)),
            pl.BlockSpec((1, QT, LANES), lambda b, j: (b, j, 0)),
            pl.BlockSpec((N_ATTN_HEADS, 3, TILE, TILE), lambda b, j: (0, 0, 0, 0)),
        ],
        out_specs=pl.BlockSpec((1, QT, ATTN_WIDTH), lambda b, j: (b, j, 0)),
        out_shape=jax.ShapeDtypeStruct((bsz, seq_len, ATTN_WIDTH), BF16),
        scratch_shapes=[
            pltpu.VMEM((seq_len // PAIR, PAIR, QT), F32),
            pltpu.VMEM((seq_len // PAIR, PAIR, QT), I16),
            pltpu.VMEM((seq_len // PAIR, PAIR, QT), I16),
            pltpu.VMEM((seq_len // PAIR, PAIR, 2 * QT), F32),
            pltpu.VMEM((N_KV_HEADS * PV_ROWS, seq_len), BF16),
            pltpu.VMEM((seq_len // QUAD, N_KV_HEADS, QUAD, rep * QT), F32),
        ],
        compiler_params=pltpu.CompilerParams(
            dimension_semantics=("arbitrary", "arbitrary"), vmem_limit_bytes=VMEM_LIMIT),
        name="sparse_attn",
    )(q3, kv3, qi3, ki3, dtw3, bias)


SSD_STEP_CHUNKS = 2
SSD_STEP = SSD_STEP_CHUNKS * TILE


def _split_bf16(v, n):
    parts = []
    for _ in range(n):
        p = v.astype(BF16)
        parts.append(p)
        v = v - p.astype(F32)
    return parts


def _dot_01(v, mat01, n):
    out = None
    for p in _split_bf16(v, n):
        t = jnp.dot(p, mat01, preferred_element_type=F32)
        out = t if out is None else out + t
    return out


def _ssd_kernel(z_ref, xbc_ref, dtw_ref, cw_ref, cb_ref, dtb_ref, alog_ref, dsk_ref, nw_ref,
                o_ref, xpad_ref, st_ref):
    step = pl.program_id(1)
    gn = SSD_N_GROUPS * SSD_D_STATE
    hpg = SSD_N_HEADS // SSD_N_GROUPS
    gw = hpg * SSD_HEAD_DIM

    @pl.when(step == 0)
    def _():
        xpad_ref[0:SUBLANES, :] = jnp.zeros((SUBLANES, CONV_DIM), F32)
        st_ref[...] = jnp.zeros_like(st_ref)

    xpad_ref[SUBLANES:, :] = xbc_ref[0]
    xfull = xpad_ref[...]
    conv = cb_ref[...] + cw_ref[CONV_WIDTH - 1:CONV_WIDTH, :] * xfull[SUBLANES:]
    for k in range(CONV_WIDTH - 1):
        shifted = pltpu.roll(xfull, CONV_WIDTH - 1 - k, 0)
        conv = conv + cw_ref[k:k + 1, :] * shifted[SUBLANES:]
    xpad_ref[0:SUBLANES, :] = xfull[SSD_STEP:]
    act = conv * jax.nn.sigmoid(conv)
    xs = act[:, :SSD_D_INNER]
    bm = act[:, SSD_D_INNER:SSD_D_INNER + gn].astype(BF16)
    cm = act[:, SSD_D_INNER + gn:].astype(BF16)

    dt_in = dtw_ref[0] + dtb_ref[...]
    dt = jnp.maximum(dt_in, 0.0) + jnp.log1p(jnp.exp(-jnp.abs(dt_in)))
    adt = dt * (-jnp.exp(alog_ref[...]))
    row = lax.broadcasted_iota(I32, (SSD_STEP, SSD_STEP), 0)
    col = lax.broadcasted_iota(I32, (SSD_STEP, SSD_STEP), 1)
    same_chunk = (row // TILE) == (col // TILE)
    chunk_tril = jnp.where(same_chunk & (row >= col), 1.0, 0.0).astype(BF16)
    acs = None
    for p in _split_bf16(adt, 3):
        t = jnp.dot(chunk_tril, p, preferred_element_type=F32)
        acs = t if acs is None else acs + t
    acs_t = acs.T
    causal = (lax.broadcasted_iota(I32, (TILE, TILE), 0) >= lax.broadcasted_iota(I32, (TILE, TILE), 1))
    a_last = [acs[(c + 1) * TILE - 1:(c + 1) * TILE, :] for c in range(SSD_STEP_CHUNKS)]
    out_decay = jnp.concatenate(
        [jnp.exp(a_last[c] - acs[c * TILE:(c + 1) * TILE]) for c in range(SSD_STEP_CHUNKS)], axis=0)
    chunk_decay = jnp.concatenate(
        [jnp.exp(a) for a in a_last] + [jnp.zeros((SUBLANES - SSD_STEP_CHUNKS, LANES), F32)], axis=0)
    hsel = (lax.broadcasted_iota(I32, (LANES, SSD_D_INNER), 0)
            == lax.broadcasted_iota(I32, (LANES, SSD_D_INNER), 1) // SSD_HEAD_DIM).astype(BF16)
    expanded = _dot_01(jnp.concatenate([dt, jnp.exp(acs), out_decay, chunk_decay], axis=0), hsel, 2)
    dt_x = expanded[:SSD_STEP]
    in_decay_x = expanded[SSD_STEP:2 * SSD_STEP]
    out_decay_x = expanded[2 * SSD_STEP:3 * SSD_STEP]
    chunk_decay_x = expanded[3 * SSD_STEP:]

    x_dt = xs * dt_x
    x_dt_b = x_dt.astype(BF16)
    x_out_b = (x_dt * out_decay_x).astype(BF16)
    tn_dims = (((0,), (0,)), ((), ()))
    nt_dims = (((1,), (1,)), ((), ()))
    states = [st_ref[g] for g in range(SSD_N_GROUPS)]
    y_chunks = []
    for c in range(SSD_STEP_CHUNKS):
        rows = slice(c * TILE, (c + 1) * TILE)
        y_parts = []
        for g in range(SSD_N_GROUPS):
            bg = bm[rows, g * SSD_D_STATE:(g + 1) * SSD_D_STATE]
            cg = cm[rows, g * SSD_D_STATE:(g + 1) * SSD_D_STATE]
            lanes = slice(g * gw, (g + 1) * gw)
            cb = lax.dot_general(cg, bg, nt_dims, preferred_element_type=F32)
            y_diag = []
            for r in range(hpg):
                h = g * hpg + r
                seg = jnp.where(causal, acs[rows, h:h + 1] - acs_t[h:h + 1, rows], NEG_INF)
                w = (cb * jnp.exp(seg)).astype(BF16)
                y_diag.append(jnp.dot(w, x_dt_b[rows, h * SSD_HEAD_DIM:(h + 1) * SSD_HEAD_DIM],
                                      preferred_element_type=F32))
            y_off = (jnp.dot(cg, states[g].astype(BF16), preferred_element_type=F32)
                     * in_decay_x[rows, lanes])
            y_parts.append(jnp.concatenate(y_diag, axis=-1) + y_off)
            st_new = lax.dot_general(bg, x_out_b[rows, lanes], tn_dims, preferred_element_type=F32)
            states[g] = states[g] * chunk_decay_x[c:c + 1, lanes] + st_new
        y_chunks.append(jnp.concatenate(y_parts, axis=-1))
    for g in range(SSD_N_GROUPS):
        st_ref[g] = states[g]

    y = jnp.concatenate(y_chunks, axis=0) + xs * dsk_ref[...]
    zv = z_ref[0]
    y = y * (zv * jax.nn.sigmoid(zv))
    ng = SSD_D_INNER // SSD_N_GROUPS
    outs = []
    for g in range(SSD_N_GROUPS):
        yg = y[:, g * ng:(g + 1) * ng]
        outs.append(yg * lax.rsqrt(jnp.mean(yg * yg, axis=-1, keepdims=True) + EPS))
    o_ref[0] = (jnp.concatenate(outs, axis=-1) * nw_ref[...]).astype(BF16)


def _ssd(z3, xbc3, dtw3, conv_w, conv_b, dt_bias, a_log, d_skip, norm_w):
    bsz, seq_len, _ = z3.shape
    assert seq_len % SSD_STEP == 0
    nc = seq_len // SSD_STEP
    blk = lambda w: pl.BlockSpec((1, SSD_STEP, w), lambda b, c: (b, c, 0))
    par = lambda r, w: pl.BlockSpec((r, w), lambda b, c: (0, 0))
    return pl.pallas_call(
        _ssd_kernel,
        grid=(bsz, nc),
        in_specs=[blk(SSD_D_INNER), blk(CONV_DIM), blk(LANES),
                  par(CONV_WIDTH, CONV_DIM), par(1, CONV_DIM), par(1, LANES), par(1, LANES),
                  par(1, SSD_D_INNER), par(1, SSD_D_INNER)],
        out_specs=blk(SSD_D_INNER),
        out_shape=jax.ShapeDtypeStruct((bsz, seq_len, SSD_D_INNER), BF16),
        scratch_shapes=[
            pltpu.VMEM((SUBLANES + SSD_STEP, CONV_DIM), F32),
            pltpu.VMEM((SSD_N_GROUPS, SSD_D_STATE, SSD_D_INNER // SSD_N_GROUPS), F32),
        ],
        compiler_params=pltpu.CompilerParams(
            dimension_semantics=("arbitrary", "arbitrary"), vmem_limit_bytes=VMEM_LIMIT),
        name="ssd_mixer",
    )(z3, xbc3, dtw3, conv_w, conv_b, dt_bias, a_log, d_skip, norm_w)


FF_CHUNK = 512


def _mlp_kernel(x_ref, attn_ref, ssd_ref, wo_ref, g1_ref, g2_ref, wu_ref, wd_ref, g3_ref, o_ref):
    mix = (jnp.dot(attn_ref[...], wo_ref[:ATTN_WIDTH, :], preferred_element_type=F32)
           + jnp.dot(ssd_ref[...], wo_ref[ATTN_WIDTH:, :], preferred_element_type=F32))
    h1 = x_ref[...] + _rms(mix, g1_ref[...])
    u = _rms(h1, g2_ref[...]).astype(BF16)
    acc = jnp.zeros(h1.shape, F32)
    for c in range(0, D_FF, FF_CHUNK):
        f = jnp.dot(u, wu_ref[:, c:c + FF_CHUNK], preferred_element_type=F32)
        f = jnp.square(jnp.maximum(f, 0.0)).astype(BF16)
        acc = acc + jnp.dot(f, wd_ref[c:c + FF_CHUNK, :], preferred_element_type=F32)
    o_ref[...] = h1 + _rms(acc, g3_ref[...])


def _mlp(x2, attn2, ssd2, wo, g1, g2, wu, wd, g3, tm):
    n = x2.shape[0]
    row = lambda i: (i, 0)
    const = lambda i: (0, 0)
    single = dict(pipeline_mode=pl.Buffered(1))
    return pl.pallas_call(
        _mlp_kernel,
        grid=(n // tm,),
        in_specs=[
            pl.BlockSpec((tm, D_MODEL), row),
            pl.BlockSpec((tm, ATTN_WIDTH), row),
            pl.BlockSpec((tm, SSD_D_INNER), row),
            pl.BlockSpec((ATTN_WIDTH + SSD_D_INNER, D_MODEL), const, **single),
            pl.BlockSpec((1, D_MODEL), const),
            pl.BlockSpec((1, D_MODEL), const),
            pl.BlockSpec((D_MODEL, D_FF), const, **single),
            pl.BlockSpec((D_FF, D_MODEL), const, **single),
            pl.BlockSpec((1, D_MODEL), const),
        ],
        out_specs=pl.BlockSpec((tm, D_MODEL), row),
        out_shape=jax.ShapeDtypeStruct((n, D_MODEL), F32),
        compiler_params=pltpu.CompilerParams(
            dimension_semantics=("arbitrary",), vmem_limit_bytes=VMEM_LIMIT),
        name="out_proj_mlp",
    )(x2, attn2, ssd2, wo, g1, g2, wu, wd, g3)


def _pad_lanes(v, n):
    return jnp.pad(v, (0, n - v.shape[0])).reshape(1, n)


def kernel(x, norm_pre_mix, norm_post_mix, norm_pre_mlp, norm_post_mlp, w_in, k_idx_ln_w, k_idx_ln_b, conv_w, conv_b, dt_bias, a_log, d_skip, ssd_norm_w, w_out, w_mlp_up, w_mlp_down, rel_bias):
    bsz, seq_len, d = x.shape
    n = bsz * seq_len
    assert d == D_MODEL and seq_len % TILE == 0
    tm = next(t for t in (1024, 512, TILE) if n % t == 0)
    n_sel = min(TOPK_MAX, seq_len // 4)
    bias = _bias_tiles(rel_bias)
    h = x.reshape(n, d)
    for i in range(norm_pre_mix.shape[0]):
        row = lambda v: v[i].reshape(1, -1)
        q, kv, qi, ki, z, xbc, dtw = _in_proj(h, row(norm_pre_mix), w_in[i],
                                              row(k_idx_ln_w), row(k_idx_ln_b), tm)
        r3 = lambda a: a.reshape(bsz, seq_len, a.shape[-1])
        dtw3 = r3(dtw)
        attn = _attention(r3(q), r3(kv), r3(qi), r3(ki), dtw3, bias, n_sel)
        ssd = _ssd(r3(z), r3(xbc), dtw3, conv_w[i], row(conv_b),
                   _pad_lanes(dt_bias[i], LANES), _pad_lanes(a_log[i], LANES),
                   jnp.repeat(d_skip[i], SSD_HEAD_DIM).reshape(1, -1), row(ssd_norm_w))
        h = _mlp(h, attn.reshape(n, -1), ssd.reshape(n, -1), w_out[i].astype(BF16),
                 row(norm_post_mix), row(norm_pre_mlp), w_mlp_up[i].astype(BF16),
                 w_mlp_down[i].astype(BF16), row(norm_post_mlp), tm)
    return h.reshape(bsz, seq_len, d)
```

```python
import functools
import math

import jax
import jax.numpy as jnp
from jax import lax
from jax.experimental import pallas as pl
from jax.experimental.pallas import tpu as pltpu

F32 = jnp.float32
BF16 = jnp.bfloat16
I32 = jnp.int32

D_MODEL = 1024
N_ATTN_HEADS = 8
N_KV_HEADS = 2
HEAD_DIM = 64
ATTN_WIDTH = N_ATTN_HEADS * HEAD_DIM
N_IDX_HEADS = 4
IDX_DIM = 64
TOPK_MAX = 256
NUM_BUCKETS = 32
MAX_DISTANCE = 128
SSD_D_INNER = 512
SSD_HEAD_DIM = 64
SSD_N_HEADS = 8
SSD_N_GROUPS = 2
SSD_D_STATE = 128
CONV_WIDTH = 4
CONV_DIM = SSD_D_INNER + 2 * SSD_N_GROUPS * SSD_D_STATE
D_FF = 4 * D_MODEL
EPS = 1e-6

LANES = 128
SUBLANES = 8
TILE = 128
VMEM_LIMIT = 56 * 1024 * 1024

QKV_W = ATTN_WIDTH + 2 * N_KV_HEADS * HEAD_DIM + N_IDX_HEADS * IDX_DIM
KV_W = 2 * N_KV_HEADS * HEAD_DIM
QI_W = N_IDX_HEADS * IDX_DIM
COL_QKV = 0
COL_KV = COL_QKV + ATTN_WIDTH
COL_QI = COL_KV + KV_W
COL_KI = COL_QKV + QKV_W
COL_Z = COL_KI + LANES
COL_XBC = COL_Z + SSD_D_INNER
COL_DTW = COL_XBC + CONV_DIM
W_CAT = COL_DTW + LANES
DTW_WI = SSD_N_HEADS
SRC_QKV = 0
SRC_KI = SRC_QKV + QKV_W
SRC_WI = SRC_KI + IDX_DIM
SRC_Z = SRC_WI + N_IDX_HEADS
SRC_XBC = SRC_Z + SSD_D_INNER
SRC_DT = SRC_XBC + CONV_DIM

NEG_INF = float("-inf")
LOG2E = 1.4426950408889634
Q_SCALE = LOG2E * HEAD_DIM ** -0.5


def _rms(x, g):
    return x * lax.rsqrt(jnp.mean(x * x, axis=-1, keepdims=True) + EPS) * g


W_PACK_ROWS = 128


def _in_proj_kernel(x_ref, g_ref, win_ref, lnw_ref, lnb_ref,
                    q_ref, kv_ref, qi_ref, ki_ref, z_ref, xbc_ref, dtw_ref, w_ref):
    @pl.when(pl.program_id(0) == 0)
    def _():
        def pack_rows(r, carry):
            rows = pl.ds(pl.multiple_of(r * W_PACK_ROWS, W_PACK_ROWS), W_PACK_ROWS)
            src = lambda lo, width: win_ref[rows, lo:lo + width]
            zeros = lambda width: jnp.zeros((W_PACK_ROWS, width), F32)
            w_ref[rows, COL_QKV:COL_KI] = src(SRC_QKV, QKV_W).astype(BF16)
            w_ref[rows, COL_KI:COL_Z] = jnp.concatenate(
                [src(SRC_KI, IDX_DIM), zeros(LANES - IDX_DIM)], axis=1).astype(BF16)
            w_ref[rows, COL_Z:COL_XBC] = src(SRC_Z, SSD_D_INNER).astype(BF16)
            w_ref[rows, COL_XBC:COL_DTW] = src(SRC_XBC, CONV_DIM).astype(BF16)
            w_ref[rows, COL_DTW:W_CAT] = jnp.concatenate(
                [src(SRC_DT, SSD_N_HEADS), src(SRC_WI, N_IDX_HEADS),
                 zeros(LANES - SSD_N_HEADS - N_IDX_HEADS)], axis=1).astype(BF16)
            return carry

        lax.fori_loop(0, D_MODEL // W_PACK_ROWS, pack_rows, 0)

    u = _rms(x_ref[...], g_ref[...]).astype(BF16)

    def mm(lo, hi):
        return jnp.dot(u, w_ref[:, lo:hi], preferred_element_type=F32)

    q_ref[...] = (mm(COL_QKV, COL_KV) * Q_SCALE).astype(BF16)
    kv_ref[...] = mm(COL_KV, COL_QI).astype(BF16)
    qi_ref[...] = mm(COL_QI, COL_KI).astype(BF16)
    ki = mm(COL_KI, COL_Z)[:, :IDX_DIM]
    mu = jnp.mean(ki, axis=-1, keepdims=True)
    var = jnp.mean(jnp.square(ki - mu), axis=-1, keepdims=True)
    ki_ref[...] = ((ki - mu) * lax.rsqrt(var + EPS) * lnw_ref[...] + lnb_ref[...]).astype(BF16)
    z_ref[...] = mm(COL_Z, COL_XBC)
    xbc_ref[...] = mm(COL_XBC, COL_DTW)
    dtw_ref[...] = mm(COL_DTW, W_CAT)


def _in_proj(x2, g, w_in, lnw, lnb, tm):
    n = x2.shape[0]
    assert w_in.shape == (D_MODEL, SRC_DT + SSD_N_HEADS)
    row = lambda i: (i, 0)
    const = lambda i: (0, 0)
    return pl.pallas_call(
        _in_proj_kernel,
        grid=(n // tm,),
        in_specs=[
            pl.BlockSpec((tm, D_MODEL), row),
            pl.BlockSpec((1, D_MODEL), const),
            pl.BlockSpec(w_in.shape, const, pipeline_mode=pl.Buffered(1)),
            pl.BlockSpec((1, IDX_DIM), const),
            pl.BlockSpec((1, IDX_DIM), const),
        ],
        out_specs=[
            pl.BlockSpec((tm, ATTN_WIDTH), row),
            pl.BlockSpec((tm, KV_W), row),
            pl.BlockSpec((tm, QI_W), row),
            pl.BlockSpec((tm, IDX_DIM), row),
            pl.BlockSpec((tm, SSD_D_INNER), row),
            pl.BlockSpec((tm, CONV_DIM), row),
            pl.BlockSpec((tm, LANES), row),
        ],
        out_shape=[
            jax.ShapeDtypeStruct((n, ATTN_WIDTH), BF16),
            jax.ShapeDtypeStruct((n, KV_W), BF16),
            jax.ShapeDtypeStruct((n, QI_W), BF16),
            jax.ShapeDtypeStruct((n, IDX_DIM), BF16),
            jax.ShapeDtypeStruct((n, SSD_D_INNER), F32),
            jax.ShapeDtypeStruct((n, CONV_DIM), F32),
            jax.ShapeDtypeStruct((n, LANES), F32),
        ],
        scratch_shapes=[pltpu.VMEM((D_MODEL, W_CAT), BF16)],
        compiler_params=pltpu.CompilerParams(
            dimension_semantics=("arbitrary",), vmem_limit_bytes=VMEM_LIMIT),
        name="in_proj",
    )(x2, g, w_in, lnw, lnb)


def _bias_kernel(tbl_ref, out_ref):
    sk = lax.broadcasted_iota(I32, (TILE, TILE), 0)
    tq = lax.broadcasted_iota(I32, (TILE, TILE), 1)
    max_exact = NUM_BUCKETS // 2
    for off in range(2):
        dist = jnp.maximum(off * TILE + tq - sk, 0)
        df = jnp.maximum(dist, 1).astype(F32)
        large = max_exact + (jnp.log(df / max_exact) / math.log(MAX_DISTANCE / max_exact)
                             * (NUM_BUCKETS - max_exact)).astype(I32)
        large = jnp.minimum(large, NUM_BUCKETS - 1)
        bucket = jnp.where(dist < max_exact, dist, large)
        for h in range(N_ATTN_HEADS):
            acc = jnp.zeros((TILE, TILE), F32)
            for b in range(NUM_BUCKETS):
                acc = jnp.where(bucket == b, tbl_ref[b, h], acc)
            out_ref[h, off] = (acc - tbl_ref[NUM_BUCKETS - 1, h]) * LOG2E
    for h in range(N_ATTN_HEADS):
        out_ref[h, 2] = jnp.zeros((TILE, TILE), F32)


def _bias_tiles(rel_bias):
    return pl.pallas_call(
        _bias_kernel,
        in_specs=[pl.BlockSpec(memory_space=pltpu.SMEM)],
        out_specs=pl.BlockSpec(memory_space=pltpu.VMEM),
        out_shape=jax.ShapeDtypeStruct((N_ATTN_HEADS, 3, TILE, TILE), F32),
        name="bias_tiles",
    )(rel_bias)


def _key_to_f32(u):
    ks = u ^ I32(-2 ** 31)
    bits = jnp.where(ks >= 0, ks, ks ^ I32(0x7FFFFFFF))
    return lax.bitcast_convert_type(bits, F32)


PAIR = 2 * TILE
QUAD = 4 * TILE
QT = 2 * TILE
HALF_BITS = 16
HALF_MASK = 2 ** HALF_BITS - 1
HALF_BIAS = 2 ** (HALF_BITS - 1)
I16 = jnp.int16
PV_ROWS = HEAD_DIM + 16


def _attn_kernel(q_ref, kv_ref, qi_ref, ki_ref, dtw_ref, qi_next_ref, dtw_next_ref, bias_ref, o_ref,
                 sc_ref, hi_ref, lo_ref, rk_ref, vt_ref, lg_ref, *, seq_len, n_sel):
    j = pl.program_id(1)
    i_hi = j * (QT // TILE) + QT // TILE - 1
    npair = i_hi // 2 + 1
    slot = lax.rem(j, 2)

    def quad_keys(v):
        return [QUAD] * (v // 2) + [PAIR] * (v % 2)

    rep = N_ATTN_HEADS // N_KV_HEADS
    gl = rep * QT
    nt_dims = (((1,), (1,)), ((), ()))
    v_lo = N_KV_HEADS * HEAD_DIM

    @pl.when(j == 0)
    def _():
        vt = kv_ref[0, :, v_lo:].astype(F32).T
        ones_row = jnp.where(lax.broadcasted_iota(I32, (PV_ROWS - HEAD_DIM, seq_len), 0) == 0, 1.0, 0.0)
        for g in range(N_KV_HEADS):
            vt_ref[g * PV_ROWS:g * PV_ROWS + HEAD_DIM, :] = vt[g * HEAD_DIM:(g + 1) * HEAD_DIM].astype(BF16)
            vt_ref[g * PV_ROWS + HEAD_DIM:(g + 1) * PV_ROWS, :] = ones_row.astype(BF16)

    idx_scale = (N_IDX_HEADS ** -0.5) * (IDX_DIM ** -0.5)
    s_loc = lax.broadcasted_iota(I32, (PAIR, QT), 0)
    s_minus_t = s_loc - lax.broadcasted_iota(I32, (PAIR, QT), 1)

    def score_quads(v, to_slot, step, qi_src, dtw_src):
        w_t = dtw_src[0].T
        w_rows = [w_t[DTW_WI + h:DTW_WI + h + 1, :] * idx_scale for h in range(N_IDX_HEADS)]
        qi = qi_src[0]
        qi_all = jnp.concatenate(
            [qi[:, h * IDX_DIM:(h + 1) * IDX_DIM] for h in range(N_IDX_HEADS)], axis=0)
        def piece(c, nk):
            kt = ki_ref[0, c * QUAD:c * QUAD + nk, :]
            d = lax.dot_general(kt, qi_all, nt_dims, preferred_element_type=F32)
            s = jnp.zeros((nk, QT), F32)
            for h in range(N_IDX_HEADS):
                s = s + w_rows[h] * jnp.maximum(d[:, h * QT:(h + 1) * QT], 0.0)
            for u in range(nk // PAIR):
                pr = c * (QUAD // PAIR) + u
                su = jnp.where(s_minus_t <= step * QT - pr * PAIR, s[u * PAIR:(u + 1) * PAIR], NEG_INF)
                sc_ref[to_slot, pr] = su
                bits = lax.bitcast_convert_type(su, I32)
                key = bits ^ (lax.shift_right_arithmetic(bits, I32(31)) & I32(0x7FFFFFFF))
                hi_ref[to_slot, pr] = lax.shift_right_arithmetic(key, I32(HALF_BITS)).astype(I16)
                lo_ref[to_slot, pr] = (key ^ I32(HALF_BIAS)).astype(I16)

        return [functools.partial(piece, c, nk) for c, nk in enumerate(quad_keys(v))]

    @pl.when(j == 0)
    def _():
        for piece in score_quads(1, 0, 0, qi_ref, dtw_ref):
            piece()

    t_glob = j * QT + lax.broadcasted_iota(I32, (1, QT), 1)
    k_eff = jnp.minimum(n_sel, t_glob + 1)
    acc_rows = 4 * 2 * SUBLANES
    one16, zero16 = jnp.ones((), I16), jnp.zeros((), I16)

    def fold(cnt, rows=acc_rows):
        parts = [cnt[k * rows:(k + 1) * rows] for k in range(cnt.shape[0] // rows)]
        while len(parts) > 1:
            parts = [a + b for a, b in zip(parts[::2], parts[1::2])]
        return parts[0]

    def total(acc):
        packed = fold(acc, 2 * SUBLANES).astype(I32)
        return (packed[:SUBLANES] + packed[SUBLANES:]).sum(axis=0, keepdims=True)

    def count_ge(plane_ref, cand16):
        def body(c, acc):
            return acc + fold(jnp.where(plane_ref[slot, c] >= cand16, one16, zero16))
        return total(lax.fori_loop(0, npair, body, jnp.zeros((acc_rows, QT), I16)))

    def to16(u):
        return (u - I32(HALF_BIAS)).astype(I16)

    def search_step(b, carry, plane_ref, base):
        prefix, cnt_ge = carry
        cand = prefix | lax.shift_left(I32(1), HALF_BITS - 1 - b)
        cnt = count_ge(plane_ref, to16(cand)) + base
        ok = cnt >= k_eff
        return jnp.where(ok, cand, prefix), jnp.where(ok, cnt, cnt_ge)

    carry = (jnp.zeros((1, QT), I32), t_glob + 1)
    hi_u, cnt_ge = lax.fori_loop(
        0, HALF_BITS, functools.partial(search_step, plane_ref=hi_ref, base=0), carry)
    hi16 = to16(hi_u)

    def narrow_body(c, acc):
        h = hi_ref[slot, c]
        lo_ref[slot, c] = jnp.where(h == hi16, lo_ref[slot, c], I16(-HALF_BIAS))
        return acc + fold(jnp.where(h > hi16, one16, zero16))

    above = total(lax.fori_loop(0, npair, narrow_body, jnp.zeros((acc_rows, QT), I16)))
    lo_u, cnt_ge = lax.fori_loop(
        0, HALF_BITS, functools.partial(search_step, plane_ref=lo_ref, base=above),
        (jnp.zeros((1, QT), I32), cnt_ge))
    thr = _key_to_f32(lax.shift_left(hi_u, I32(HALF_BITS)) | lo_u)

    q = q_ref[0]
    qgs = [jnp.concatenate(
        [q[:, (g * rep + r) * HEAD_DIM:(g * rep + r + 1) * HEAD_DIM] for r in range(rep)],
        axis=0) for g in range(N_KV_HEADS)]
    below_diag = (lax.broadcasted_iota(I32, (PAIR, PAIR), 0)
                  > lax.broadcasted_iota(I32, (PAIR, PAIR), 1)).astype(BF16)
    tiles_per_quad = QUAD // TILE
    pairs_per_quad = QUAD // PAIR
    n_qt = QT // TILE
    int_min = I32(-2 ** 31)

    def bias_rows(c, g, u):
        cols = []
        for r in range(rep):
            for w in range(n_qt):
                off = jnp.clip(j * n_qt + w - (c * tiles_per_quad + u), 0, 2)
                cols.append(bias_ref[g * rep + r, off])
        return jnp.concatenate(cols, axis=1)

    def tied(s):
        eq = jnp.where(s == thr, 1.0, 0.0)
        neg0 = eq * jnp.where(lax.bitcast_convert_type(s, I32) == int_min, 1.0, 0.0)
        return eq - neg0, neg0

    def attend(v):
        tots = []
        for pr in range(v):
            e2 = jnp.concatenate(tied(sc_ref[slot, pr]), axis=1).astype(BF16)
            r = jnp.dot(below_diag, e2, preferred_element_type=F32)
            rk_ref[pr] = r
            tots.append(r[PAIR - 1:PAIR] + e2[PAIR - 1:PAIR].astype(F32))
        tot = tots[0]
        for t in tots[1:]:
            tot = tot + t
        need = (k_eff - cnt_ge).astype(F32) + tot[:, :QT] + tot[:, QT:]
        off = jnp.concatenate([jnp.zeros((1, QT), F32), tot[:, :QT]], axis=1)

        m8 = [jnp.full((SUBLANES, gl), NEG_INF, F32) for _ in range(N_KV_HEADS)]
        nks = quad_keys(v)
        for c, nk in enumerate(nks):
            masks = []
            for u in range(nk // PAIR):
                pr = c * pairs_per_quad + u
                s = sc_ref[slot, pr]
                r = rk_ref[pr] + off
                off = off + tots[pr]
                neg0 = lax.bitcast_convert_type(s, I32) == int_min
                keep_tie = jnp.where(jnp.where(neg0, r[:, QT:], r[:, :QT]) < need, 0.0, NEG_INF)
                masks.append(jnp.where(s > thr, 0.0, jnp.where(s == thr, keep_tie, NEG_INF)))
            mask = jnp.concatenate(masks, axis=0)
            mask = jnp.concatenate([mask] * rep, axis=1)
            keys = slice(c * QUAD, c * QUAD + nk)
            for g in range(N_KV_HEADS):
                kt = kv_ref[0, keys, g * HEAD_DIM:(g + 1) * HEAD_DIM]
                lg = lax.dot_general(kt, qgs[g], nt_dims, preferred_element_type=F32) + mask
                if c == len(nks) - 1:
                    lg = lg + jnp.concatenate([bias_rows(c, g, u) for u in range(nk // TILE)], axis=0)
                elif c == len(nks) - 2 and v % 2 == 1:
                    u = tiles_per_quad - 1
                    lg = jnp.concatenate([lg[:u * TILE], lg[u * TILE:] + bias_rows(c, g, u)], axis=0)
                lg_ref[c, g, :nk] = lg
                m8[g] = jnp.maximum(m8[g], lg.reshape(nk // SUBLANES, SUBLANES, gl).max(axis=0))

        pieces = (score_quads(v + 1, 1 - slot, j + 1, qi_next_ref, dtw_next_ref)
                  if v < seq_len // PAIR else [])

        outs = []
        for g in range(N_KV_HEADS):
            m_row = jnp.max(m8[g], axis=0, keepdims=True)
            acc = jnp.zeros((PV_ROWS, gl), F32)
            for c, nk in enumerate(nks):
                p = jnp.exp2((lg_ref[c, g, :nk] - m_row).astype(BF16))
                vt = vt_ref[g * PV_ROWS:(g + 1) * PV_ROWS, c * QUAD:c * QUAD + nk]
                acc = acc + jnp.dot(vt, p, preferred_element_type=F32)
                if pieces:
                    pieces.pop(0)()
            if g == N_KV_HEADS - 1:
                for piece in pieces:
                    piece()
            o_g = acc[:HEAD_DIM] / acc[HEAD_DIM:HEAD_DIM + 1]
            outs += [o_g[:, r * QT:(r + 1) * QT] for r in range(rep)]
        o_ref[0] = jnp.concatenate(outs, axis=0).T.astype(BF16)

    for v in range(1, seq_len // PAIR + 1):
        pl.when(npair == v)(functools.partial(attend, v))


def _attention(q3, kv3, qi3, ki3, dtw3, bias, n_sel):
    bsz, seq_len, _ = q3.shape
    assert seq_len % QUAD == 0 and QUAD % QT == 0
    rep = N_ATTN_HEADS // N_KV_HEADS
    kern = functools.partial(_attn_kernel, seq_len=seq_len, n_sel=n_sel)
    steps = seq_len // QT
    this_step = lambda b, j: (b, j, 0)
    next_step = lambda b, j: (b, jnp.minimum(j + 1, steps - 1), 0)
    return pl.pallas_call(
        kern,
        grid=(bsz, steps),
        in_specs=[
            pl.BlockSpec((1, QT, ATTN_WIDTH), this_step),
            pl.BlockSpec((1, seq_len, KV_W), lambda b, j: (b, 0, 0)),
            pl.BlockSpec((1, QT, QI_W), this_step),
            pl.BlockSpec((1, seq_len, IDX_DIM), lambda b, j: (b, 0, 0)),
            pl.BlockSpec((1, QT, LANES), this_step),
            pl.BlockSpec((1, QT, QI_W), next_step),
            pl.BlockSpec((1, QT, LANES), next_step),
            pl.BlockSpec((N_ATTN_HEADS, 3, TILE, TILE), lambda b, j: (0, 0, 0, 0)),
        ],
        out_specs=pl.BlockSpec((1, QT, ATTN_WIDTH), this_step),
        out_shape=jax.ShapeDtypeStruct((bsz, seq_len, ATTN_WIDTH), BF16),
        scratch_shapes=[
            pltpu.VMEM((2, seq_len // PAIR, PAIR, QT), F32),
            pltpu.VMEM((2, seq_len // PAIR, PAIR, QT), I16),
            pltpu.VMEM((2, seq_len // PAIR, PAIR, QT), I16),
            pltpu.VMEM((seq_len // PAIR, PAIR, 2 * QT), F32),
            pltpu.VMEM((N_KV_HEADS * PV_ROWS, seq_len), BF16),
            pltpu.VMEM((seq_len // QUAD, N_KV_HEADS, QUAD, rep * QT), F32),
        ],
        compiler_params=pltpu.CompilerParams(
            dimension_semantics=("arbitrary", "arbitrary"), vmem_limit_bytes=VMEM_LIMIT),
        name="sparse_attn",
    )(q3, kv3, qi3, ki3, dtw3, qi3, dtw3, bias)


SSD_STEP_CHUNKS = 2
SSD_STEP = SSD_STEP_CHUNKS * TILE


def _split_bf16(v, n):
    parts = []
    for _ in range(n):
        p = v.astype(BF16)
        parts.append(p)
        v = v - p.astype(F32)
    return parts


def _dot_01(v, mat01, n):
    out = None
    for p in _split_bf16(v, n):
        t = jnp.dot(p, mat01, preferred_element_type=F32)
        out = t if out is None else out + t
    return out


def _ssd_kernel(z_ref, xbc_ref, dtw_ref, cw_ref, cb_ref, dtb_ref, alog_ref, dsk_ref, nw_ref,
                o_ref, xpad_ref, st_ref):
    step = pl.program_id(1)
    gn = SSD_N_GROUPS * SSD_D_STATE
    hpg = SSD_N_HEADS // SSD_N_GROUPS
    gw = hpg * SSD_HEAD_DIM

    @pl.when(step == 0)
    def _():
        xpad_ref[0:SUBLANES, :] = jnp.zeros((SUBLANES, CONV_DIM), F32)
        st_ref[...] = jnp.zeros_like(st_ref)

    xpad_ref[SUBLANES:, :] = xbc_ref[0]
    xfull = xpad_ref[...]
    conv = cb_ref[...] + cw_ref[CONV_WIDTH - 1:CONV_WIDTH, :] * xfull[SUBLANES:]
    for k in range(CONV_WIDTH - 1):
        shifted = pltpu.roll(xfull, CONV_WIDTH - 1 - k, 0)
        conv = conv + cw_ref[k:k + 1, :] * shifted[SUBLANES:]
    xpad_ref[0:SUBLANES, :] = xfull[SSD_STEP:]
    act = conv * jax.nn.sigmoid(conv)
    xs = act[:, :SSD_D_INNER]
    bm = act[:, SSD_D_INNER:SSD_D_INNER + gn].astype(BF16)
    cm = act[:, SSD_D_INNER + gn:].astype(BF16)

    dt_in = dtw_ref[0] + dtb_ref[...]
    dt = jnp.maximum(dt_in, 0.0) + jnp.log1p(jnp.exp(-jnp.abs(dt_in)))
    adt = dt * (-jnp.exp(alog_ref[...]))
    row = lax.broadcasted_iota(I32, (SSD_STEP, SSD_STEP), 0)
    col = lax.broadcasted_iota(I32, (SSD_STEP, SSD_STEP), 1)
    same_chunk = (row // TILE) == (col // TILE)
    chunk_tril = jnp.where(same_chunk & (row >= col), 1.0, 0.0).astype(BF16)
    acs = None
    for p in _split_bf16(adt, 3):
        t = jnp.dot(chunk_tril, p, preferred_element_type=F32)
        acs = t if acs is None else acs + t
    acs_t = acs.T
    causal = (lax.broadcasted_iota(I32, (TILE, TILE), 0) >= lax.broadcasted_iota(I32, (TILE, TILE), 1))
    a_last = [acs[(c + 1) * TILE - 1:(c + 1) * TILE, :] for c in range(SSD_STEP_CHUNKS)]
    out_decay = jnp.concatenate(
        [jnp.exp(a_last[c] - acs[c * TILE:(c + 1) * TILE]) for c in range(SSD_STEP_CHUNKS)], axis=0)
    chunk_decay = jnp.concatenate(
        [jnp.exp(a) for a in a_last] + [jnp.zeros((SUBLANES - SSD_STEP_CHUNKS, LANES), F32)], axis=0)
    hsel = (lax.broadcasted_iota(I32, (LANES, SSD_D_INNER), 0)
            == lax.broadcasted_iota(I32, (LANES, SSD_D_INNER), 1) // SSD_HEAD_DIM).astype(BF16)
    expanded = _dot_01(jnp.concatenate([dt, jnp.exp(acs), out_decay, chunk_decay], axis=0), hsel, 2)
    dt_x = expanded[:SSD_STEP]
    in_decay_x = expanded[SSD_STEP:2 * SSD_STEP]
    out_decay_x = expanded[2 * SSD_STEP:3 * SSD_STEP]
    chunk_decay_x = expanded[3 * SSD_STEP:]

    x_dt = xs * dt_x
    x_dt_b = x_dt.astype(BF16)
    x_out_b = (x_dt * out_decay_x).astype(BF16)
    tn_dims = (((0,), (0,)), ((), ()))
    nt_dims = (((1,), (1,)), ((), ()))
    states = [st_ref[g] for g in range(SSD_N_GROUPS)]
    y_chunks = []
    for c in range(SSD_STEP_CHUNKS):
        rows = slice(c * TILE, (c + 1) * TILE)
        y_parts = []
        for g in range(SSD_N_GROUPS):
            bg = bm[rows, g * SSD_D_STATE:(g + 1) * SSD_D_STATE]
            cg = cm[rows, g * SSD_D_STATE:(g + 1) * SSD_D_STATE]
            lanes = slice(g * gw, (g + 1) * gw)
            cb = lax.dot_general(cg, bg, nt_dims, preferred_element_type=F32)
            y_diag = []
            for r in range(hpg):
                h = g * hpg + r
                seg = jnp.where(causal, acs[rows, h:h + 1] - acs_t[h:h + 1, rows], NEG_INF)
                w = (cb * jnp.exp(seg)).astype(BF16)
                y_diag.append(jnp.dot(w, x_dt_b[rows, h * SSD_HEAD_DIM:(h + 1) * SSD_HEAD_DIM],
                                      preferred_element_type=F32))
            y_off = (jnp.dot(cg, states[g].astype(BF16), preferred_element_type=F32)
                     * in_decay_x[rows, lanes])
            y_parts.append(jnp.concatenate(y_diag, axis=-1) + y_off)
            st_new = lax.dot_general(bg, x_out_b[rows, lanes], tn_dims, preferred_element_type=F32)
            states[g] = states[g] * chunk_decay_x[c:c + 1, lanes] + st_new
        y_chunks.append(jnp.concatenate(y_parts, axis=-1))
    for g in range(SSD_N_GROUPS):
        st_ref[g] = states[g]

    y = jnp.concatenate(y_chunks, axis=0) + xs * dsk_ref[...]
    zv = z_ref[0]
    y = y * (zv * jax.nn.sigmoid(zv))
    ng = SSD_D_INNER // SSD_N_GROUPS
    outs = []
    for g in range(SSD_N_GROUPS):
        yg = y[:, g * ng:(g + 1) * ng]
        outs.append(yg * lax.rsqrt(jnp.mean(yg * yg, axis=-1, keepdims=True) + EPS))
    o_ref[0] = (jnp.concatenate(outs, axis=-1) * nw_ref[...]).astype(BF16)


def _ssd(z3, xbc3, dtw3, conv_w, conv_b, dt_bias, a_log, d_skip, norm_w):
    bsz, seq_len, _ = z3.shape
    assert seq_len % SSD_STEP == 0
    nc = seq_len // SSD_STEP
    blk = lambda w: pl.BlockSpec((1, SSD_STEP, w), lambda b, c: (b, c, 0))
    par = lambda r, w: pl.BlockSpec((r, w), lambda b, c: (0, 0))
    return pl.pallas_call(
        _ssd_kernel,
        grid=(bsz, nc),
        in_specs=[blk(SSD_D_INNER), blk(CONV_DIM), blk(LANES),
                  par(CONV_WIDTH, CONV_DIM), par(1, CONV_DIM), par(1, LANES), par(1, LANES),
                  par(1, SSD_D_INNER), par(1, SSD_D_INNER)],
        out_specs=blk(SSD_D_INNER),
        out_shape=jax.ShapeDtypeStruct((bsz, seq_len, SSD_D_INNER), BF16),
        scratch_shapes=[
            pltpu.VMEM((SUBLANES + SSD_STEP, CONV_DIM), F32),
            pltpu.VMEM((SSD_N_GROUPS, SSD_D_STATE, SSD_D_INNER // SSD_N_GROUPS), F32),
        ],
        compiler_params=pltpu.CompilerParams(
            dimension_semantics=("arbitrary", "arbitrary"), vmem_limit_bytes=VMEM_LIMIT),
        name="ssd_mixer",
    )(z3, xbc3, dtw3, conv_w, conv_b, dt_bias, a_log, d_skip, norm_w)


FF_CHUNK = 512


def _mlp_kernel(x_ref, attn_ref, ssd_ref, wo_ref, g1_ref, g2_ref, wu_ref, wd_ref, g3_ref, o_ref):
    mix = (jnp.dot(attn_ref[...], wo_ref[:ATTN_WIDTH, :], preferred_element_type=F32)
           + jnp.dot(ssd_ref[...], wo_ref[ATTN_WIDTH:, :], preferred_element_type=F32))
    h1 = x_ref[...] + _rms(mix, g1_ref[...])
    u = _rms(h1, g2_ref[...]).astype(BF16)
    acc = jnp.zeros(h1.shape, F32)
    for c in range(0, D_FF, FF_CHUNK):
        f = jnp.dot(u, wu_ref[:, c:c + FF_CHUNK], preferred_element_type=F32)
        f = jnp.square(jnp.maximum(f, 0.0)).astype(BF16)
        acc = acc + jnp.dot(f, wd_ref[c:c + FF_CHUNK, :], preferred_element_type=F32)
    o_ref[...] = h1 + _rms(acc, g3_ref[...])


def _mlp(x2, attn2, ssd2, wo, g1, g2, wu, wd, g3, tm):
    n = x2.shape[0]
    row = lambda i: (i, 0)
    const = lambda i: (0, 0)
    single = dict(pipeline_mode=pl.Buffered(1))
    return pl.pallas_call(
        _mlp_kernel,
        grid=(n // tm,),
        in_specs=[
            pl.BlockSpec((tm, D_MODEL), row),
            pl.BlockSpec((tm, ATTN_WIDTH), row),
            pl.BlockSpec((tm, SSD_D_INNER), row),
            pl.BlockSpec((ATTN_WIDTH + SSD_D_INNER, D_MODEL), const, **single),
            pl.BlockSpec((1, D_MODEL), const),
            pl.BlockSpec((1, D_MODEL), const),
            pl.BlockSpec((D_MODEL, D_FF), const, **single),
            pl.BlockSpec((D_FF, D_MODEL), const, **single),
            pl.BlockSpec((1, D_MODEL), const),
        ],
        out_specs=pl.BlockSpec((tm, D_MODEL), row),
        out_shape=jax.ShapeDtypeStruct((n, D_MODEL), F32),
        compiler_params=pltpu.CompilerParams(
            dimension_semantics=("arbitrary",), vmem_limit_bytes=VMEM_LIMIT),
        name="out_proj_mlp",
    )(x2, attn2, ssd2, wo, g1, g2, wu, wd, g3)


def _pad_lanes(v, n):
    return jnp.pad(v, (0, n - v.shape[0])).reshape(1, n)


def kernel(x, norm_pre_mix, norm_post_mix, norm_pre_mlp, norm_post_mlp, w_in, k_idx_ln_w, k_idx_ln_b, conv_w, conv_b, dt_bias, a_log, d_skip, ssd_norm_w, w_out, w_mlp_up, w_mlp_down, rel_bias):
    bsz, seq_len, d = x.shape
    n = bsz * seq_len
    assert d == D_MODEL and seq_len % TILE == 0
    tm = next(t for t in (1024, 512, TILE) if n % t == 0)
    n_sel = min(TOPK_MAX, seq_len // 4)
    bias = _bias_tiles(rel_bias)
    h = x.reshape(n, d)
    for i in range(norm_pre_mix.shape[0]):
        row = lambda v: v[i].reshape(1, -1)
        q, kv, qi, ki, z, xbc, dtw = _in_proj(h, row(norm_pre_mix), w_in[i],
                                              row(k_idx_ln_w), row(k_idx_ln_b), tm)
        r3 = lambda a: a.reshape(bsz, seq_len, a.shape[-1])
        dtw3 = r3(dtw)
        attn = _attention(r3(q), r3(kv), r3(qi), r3(ki), dtw3, bias, n_sel)
        ssd = _ssd(r3(z), r3(xbc), dtw3, conv_w[i], row(conv_b),
                   _pad_lanes(dt_bias[i], LANES), _pad_lanes(a_log[i], LANES),
                   jnp.repeat(d_skip[i], SSD_HEAD_DIM).reshape(1, -1), row(ssd_norm_w))
        h = _mlp(h, attn.reshape(n, -1), ssd.reshape(n, -1), w_out[i].astype(BF16),
                 row(norm_post_mix), row(norm_pre_mlp), w_mlp_up[i].astype(BF16),
                 w_mlp_down[i].astype(BF16), row(norm_post_mlp), tm)
    return h.reshape(bsz, seq_len, d)
```

```python
import functools
import math

import jax
import jax.numpy as jnp
from jax import lax
from jax.experimental import pallas as pl
from jax.experimental.pallas import tpu as pltpu

F32 = jnp.float32
BF16 = jnp.bfloat16
I32 = jnp.int32

D_MODEL = 1024
N_ATTN_HEADS = 8
N_KV_HEADS = 2
HEAD_DIM = 64
ATTN_WIDTH = N_ATTN_HEADS * HEAD_DIM
N_IDX_HEADS = 4
IDX_DIM = 64
TOPK_MAX = 256
NUM_BUCKETS = 32
MAX_DISTANCE = 128
SSD_D_INNER = 512
SSD_HEAD_DIM = 64
SSD_N_HEADS = 8
SSD_N_GROUPS = 2
SSD_D_STATE = 128
CONV_WIDTH = 4
CONV_DIM = SSD_D_INNER + 2 * SSD_N_GROUPS * SSD_D_STATE
D_FF = 4 * D_MODEL
EPS = 1e-6

LANES = 128
SUBLANES = 8
TILE = 128
VMEM_LIMIT = 56 * 1024 * 1024

QKV_W = ATTN_WIDTH + 2 * N_KV_HEADS * HEAD_DIM + N_IDX_HEADS * IDX_DIM
KV_W = 2 * N_KV_HEADS * HEAD_DIM
QI_W = N_IDX_HEADS * IDX_DIM
COL_QKV = 0
COL_KV = COL_QKV + ATTN_WIDTH
COL_QI = COL_KV + KV_W
COL_KI = COL_QKV + QKV_W
COL_Z = COL_KI + LANES
COL_XBC = COL_Z + SSD_D_INNER
COL_DTW = COL_XBC + CONV_DIM
W_CAT = COL_DTW + LANES
DTW_WI = SSD_N_HEADS
SRC_QKV = 0
SRC_KI = SRC_QKV + QKV_W
SRC_WI = SRC_KI + IDX_DIM
SRC_Z = SRC_WI + N_IDX_HEADS
SRC_XBC = SRC_Z + SSD_D_INNER
SRC_DT = SRC_XBC + CONV_DIM

NEG_INF = float("-inf")
LOG2E = 1.4426950408889634
Q_SCALE = LOG2E * HEAD_DIM ** -0.5


def _rms(x, g):
    return x * lax.rsqrt(jnp.mean(x * x, axis=-1, keepdims=True) + EPS) * g


W_PACK_ROWS = 128


def _in_proj_kernel(x_ref, g_ref, win_ref, lnw_ref, lnb_ref,
                    q_ref, kv_ref, qi_ref, ki_ref, z_ref, xbc_ref, dtw_ref, w_ref):
    @pl.when(pl.program_id(0) == 0)
    def _():
        def pack_rows(r, carry):
            rows = pl.ds(pl.multiple_of(r * W_PACK_ROWS, W_PACK_ROWS), W_PACK_ROWS)
            src = lambda lo, width: win_ref[rows, lo:lo + width]
            zeros = lambda width: jnp.zeros((W_PACK_ROWS, width), F32)
            w_ref[rows, COL_QKV:COL_KI] = src(SRC_QKV, QKV_W).astype(BF16)
            w_ref[rows, COL_KI:COL_Z] = jnp.concatenate(
                [src(SRC_KI, IDX_DIM), zeros(LANES - IDX_DIM)], axis=1).astype(BF16)
            w_ref[rows, COL_Z:COL_XBC] = src(SRC_Z, SSD_D_INNER).astype(BF16)
            w_ref[rows, COL_XBC:COL_DTW] = src(SRC_XBC, CONV_DIM).astype(BF16)
            w_ref[rows, COL_DTW:W_CAT] = jnp.concatenate(
                [src(SRC_DT, SSD_N_HEADS), src(SRC_WI, N_IDX_HEADS),
                 zeros(LANES - SSD_N_HEADS - N_IDX_HEADS)], axis=1).astype(BF16)
            return carry

        lax.fori_loop(0, D_MODEL // W_PACK_ROWS, pack_rows, 0)

    u = _rms(x_ref[...], g_ref[...]).astype(BF16)

    def mm(lo, hi):
        return jnp.dot(u, w_ref[:, lo:hi], preferred_element_type=F32)

    q_ref[...] = (mm(COL_QKV, COL_KV) * Q_SCALE).astype(BF16)
    kv_ref[...] = mm(COL_KV, COL_QI).astype(BF16)
    qi_ref[...] = mm(COL_QI, COL_KI).astype(BF16)
    ki = mm(COL_KI, COL_Z)[:, :IDX_DIM]
    mu = jnp.mean(ki, axis=-1, keepdims=True)
    var = jnp.mean(jnp.square(ki - mu), axis=-1, keepdims=True)
    ki_ref[...] = ((ki - mu) * lax.rsqrt(var + EPS) * lnw_ref[...] + lnb_ref[...]).astype(BF16)
    z_ref[...] = mm(COL_Z, COL_XBC)
    xbc_ref[...] = mm(COL_XBC, COL_DTW)
    dtw_ref[...] = mm(COL_DTW, W_CAT)


def _in_proj(x2, g, w_in, lnw, lnb, tm):
    n = x2.shape[0]
    assert w_in.shape == (D_MODEL, SRC_DT + SSD_N_HEADS)
    row = lambda i: (i, 0)
    const = lambda i: (0, 0)
    return pl.pallas_call(
        _in_proj_kernel,
        grid=(n // tm,),
        in_specs=[
            pl.BlockSpec((tm, D_MODEL), row),
            pl.BlockSpec((1, D_MODEL), const),
            pl.BlockSpec(w_in.shape, const, pipeline_mode=pl.Buffered(1)),
            pl.BlockSpec((1, IDX_DIM), const),
            pl.BlockSpec((1, IDX_DIM), const),
        ],
        out_specs=[
            pl.BlockSpec((tm, ATTN_WIDTH), row),
            pl.BlockSpec((tm, KV_W), row),
            pl.BlockSpec((tm, QI_W), row),
            pl.BlockSpec((tm, IDX_DIM), row),
            pl.BlockSpec((tm, SSD_D_INNER), row),
            pl.BlockSpec((tm, CONV_DIM), row),
            pl.BlockSpec((tm, LANES), row),
        ],
        out_shape=[
            jax.ShapeDtypeStruct((n, ATTN_WIDTH), BF16),
            jax.ShapeDtypeStruct((n, KV_W), BF16),
            jax.ShapeDtypeStruct((n, QI_W), BF16),
            jax.ShapeDtypeStruct((n, IDX_DIM), BF16),
            jax.ShapeDtypeStruct((n, SSD_D_INNER), F32),
            jax.ShapeDtypeStruct((n, CONV_DIM), F32),
            jax.ShapeDtypeStruct((n, LANES), F32),
        ],
        scratch_shapes=[pltpu.VMEM((D_MODEL, W_CAT), BF16)],
        compiler_params=pltpu.CompilerParams(
            dimension_semantics=("arbitrary",), vmem_limit_bytes=VMEM_LIMIT),
        name="in_proj",
    )(x2, g, w_in, lnw, lnb)


def _bias_kernel(tbl_ref, out_ref):
    sk = lax.broadcasted_iota(I32, (TILE, TILE), 0)
    tq = lax.broadcasted_iota(I32, (TILE, TILE), 1)
    max_exact = NUM_BUCKETS // 2
    for off in range(2):
        dist = jnp.maximum(off * TILE + tq - sk, 0)
        df = jnp.maximum(dist, 1).astype(F32)
        large = max_exact + (jnp.log(df / max_exact) / math.log(MAX_DISTANCE / max_exact)
                             * (NUM_BUCKETS - max_exact)).astype(I32)
        large = jnp.minimum(large, NUM_BUCKETS - 1)
        bucket = jnp.where(dist < max_exact, dist, large)
        for h in range(N_ATTN_HEADS):
            acc = jnp.zeros((TILE, TILE), F32)
            for b in range(NUM_BUCKETS):
                acc = jnp.where(bucket == b, tbl_ref[b, h], acc)
            out_ref[h, off] = (acc - tbl_ref[NUM_BUCKETS - 1, h]) * LOG2E
    for h in range(N_ATTN_HEADS):
        out_ref[h, 2] = jnp.zeros((TILE, TILE), F32)


def _bias_tiles(rel_bias):
    return pl.pallas_call(
        _bias_kernel,
        in_specs=[pl.BlockSpec(memory_space=pltpu.SMEM)],
        out_specs=pl.BlockSpec(memory_space=pltpu.VMEM),
        out_shape=jax.ShapeDtypeStruct((N_ATTN_HEADS, 3, TILE, TILE), F32),
        name="bias_tiles",
    )(rel_bias)


def _key_to_f32(u):
    ks = u ^ I32(-2 ** 31)
    bits = jnp.where(ks >= 0, ks, ks ^ I32(0x7FFFFFFF))
    return lax.bitcast_convert_type(bits, F32)


PAIR = 2 * TILE
QUAD = 4 * TILE
QT = 2 * TILE
HALF_BITS = 16
HALF_MASK = 2 ** HALF_BITS - 1
HALF_BIAS = 2 ** (HALF_BITS - 1)
LO_BITS_FIRST = 10
I16 = jnp.int16
PV_ROWS = HEAD_DIM + 16


def _attn_kernel(q_ref, kv_ref, qi_ref, ki_ref, dtw_ref, qi_next_ref, dtw_next_ref, bias_ref, o_ref,
                 sc_ref, hi_ref, lo_ref, rk_ref, vt_ref, lg_ref, *, seq_len, n_sel):
    j = pl.program_id(1)
    i_hi = j * (QT // TILE) + QT // TILE - 1
    npair = i_hi // 2 + 1
    slot = lax.rem(j, 2)

    def quad_keys(v):
        return [QUAD] * (v // 2) + [PAIR] * (v % 2)

    rep = N_ATTN_HEADS // N_KV_HEADS
    gl = rep * QT
    nt_dims = (((1,), (1,)), ((), ()))
    v_lo = N_KV_HEADS * HEAD_DIM

    @pl.when(j == 0)
    def _():
        vt = kv_ref[0, :, v_lo:].astype(F32).T
        ones_row = jnp.where(lax.broadcasted_iota(I32, (PV_ROWS - HEAD_DIM, seq_len), 0) == 0, 1.0, 0.0)
        for g in range(N_KV_HEADS):
            vt_ref[g * PV_ROWS:g * PV_ROWS + HEAD_DIM, :] = vt[g * HEAD_DIM:(g + 1) * HEAD_DIM].astype(BF16)
            vt_ref[g * PV_ROWS + HEAD_DIM:(g + 1) * PV_ROWS, :] = ones_row.astype(BF16)

    idx_scale = (N_IDX_HEADS ** -0.5) * (IDX_DIM ** -0.5)
    s_loc = lax.broadcasted_iota(I32, (PAIR, QT), 0)
    s_minus_t = s_loc - lax.broadcasted_iota(I32, (PAIR, QT), 1)

    def score_quads(v, to_slot, step, qi_src, dtw_src):
        w_t = dtw_src[0].T
        w_rows = [w_t[DTW_WI + h:DTW_WI + h + 1, :] * idx_scale for h in range(N_IDX_HEADS)]
        qi = qi_src[0]
        qi_all = jnp.concatenate(
            [qi[:, h * IDX_DIM:(h + 1) * IDX_DIM] for h in range(N_IDX_HEADS)], axis=0)
        def piece(c, nk):
            kt = ki_ref[0, c * QUAD:c * QUAD + nk, :]
            d = lax.dot_general(kt, qi_all, nt_dims, preferred_element_type=F32)
            s = jnp.zeros((nk, QT), F32)
            for h in range(N_IDX_HEADS):
                s = s + w_rows[h] * jnp.maximum(d[:, h * QT:(h + 1) * QT], 0.0)
            for u in range(nk // PAIR):
                pr = c * (QUAD // PAIR) + u
                su = jnp.where(s_minus_t <= step * QT - pr * PAIR, s[u * PAIR:(u + 1) * PAIR], NEG_INF)
                sc_ref[to_slot, pr] = su
                bits = lax.bitcast_convert_type(su, I32)
                key = bits ^ (lax.shift_right_arithmetic(bits, I32(31)) & I32(0x7FFFFFFF))
                hi_ref[to_slot, pr] = lax.shift_right_arithmetic(key, I32(HALF_BITS)).astype(I16)
                lo_ref[to_slot, pr] = (key ^ I32(HALF_BIAS)).astype(I16)

        return [functools.partial(piece, c, nk) for c, nk in enumerate(quad_keys(v))]

    @pl.when(j == 0)
    def _():
        for piece in score_quads(1, 0, 0, qi_ref, dtw_ref):
            piece()

    t_glob = j * QT + lax.broadcasted_iota(I32, (1, QT), 1)
    k_eff = jnp.minimum(n_sel, t_glob + 1)
    acc_rows = 4 * 2 * SUBLANES
    one16, zero16 = jnp.ones((), I16), jnp.zeros((), I16)

    def fold(cnt, rows=acc_rows):
        parts = [cnt[k * rows:(k + 1) * rows] for k in range(cnt.shape[0] // rows)]
        while len(parts) > 1:
            parts = [a + b for a, b in zip(parts[::2], parts[1::2])]
        return parts[0]

    def total(acc):
        packed = fold(acc, 2 * SUBLANES).astype(I32)
        return (packed[:SUBLANES] + packed[SUBLANES:]).sum(axis=0, keepdims=True)

    def count_ge(plane_ref, cand16):
        def body(c, acc):
            return acc + fold(jnp.where(plane_ref[slot, c] >= cand16, one16, zero16))
        return total(lax.fori_loop(0, npair, body, jnp.zeros((acc_rows, QT), I16)))

    def to16(u):
        return (u - I32(HALF_BIAS)).astype(I16)

    def search_step(b, carry, plane_ref, base):
        prefix, cnt_ge, settled = carry
        cand = prefix | lax.shift_left(I32(1), HALF_BITS - 1 - b)
        cnt = count_ge(plane_ref, to16(cand)) + base
        ok = cnt >= k_eff
        settled = jnp.where(ok, jnp.where(cnt == k_eff, 1, settled), settled)
        return jnp.where(ok, cand, prefix), jnp.where(ok, cnt, cnt_ge), settled

    carry = (jnp.zeros((1, QT), I32), t_glob + 1, jnp.zeros((1, QT), I32))
    hi_u, cnt_ge, settled = lax.fori_loop(
        0, HALF_BITS, functools.partial(search_step, plane_ref=hi_ref, base=0), carry)
    hi16 = to16(hi_u)

    def narrow_body(c, acc):
        h = hi_ref[slot, c]
        lo_ref[slot, c] = jnp.where(h == hi16, lo_ref[slot, c], I16(-HALF_BIAS))
        return acc + fold(jnp.where(h > hi16, one16, zero16))

    above = total(lax.fori_loop(0, npair, narrow_body, jnp.zeros((acc_rows, QT), I16)))
    lo_step = functools.partial(search_step, plane_ref=lo_ref, base=above)
    floor_tie = count_ge(lo_ref, to16(jnp.ones((1, QT), I32))) + above < k_eff
    carry = (jnp.zeros((1, QT), I32), cnt_ge, jnp.where(floor_tie, 1, settled))
    carry = lax.fori_loop(0, LO_BITS_FIRST, lo_step, carry)
    lo_u, cnt_ge, _ = lax.cond(
        jnp.min(carry[2]) > 0, lambda c: c,
        lambda c: lax.fori_loop(LO_BITS_FIRST, HALF_BITS, lo_step, c), carry)
    thr = _key_to_f32(lax.shift_left(hi_u, I32(HALF_BITS)) | lo_u)

    q = q_ref[0]
    qgs = [jnp.concatenate(
        [q[:, (g * rep + r) * HEAD_DIM:(g * rep + r + 1) * HEAD_DIM] for r in range(rep)],
        axis=0) for g in range(N_KV_HEADS)]
    below_diag = (lax.broadcasted_iota(I32, (PAIR, PAIR), 0)
                  > lax.broadcasted_iota(I32, (PAIR, PAIR), 1)).astype(BF16)
    tiles_per_quad = QUAD // TILE
    pairs_per_quad = QUAD // PAIR
    n_qt = QT // TILE
    int_min = I32(-2 ** 31)

    def bias_rows(c, g, u):
        cols = []
        for r in range(rep):
            for w in range(n_qt):
                off = jnp.clip(j * n_qt + w - (c * tiles_per_quad + u), 0, 2)
                cols.append(bias_ref[g * rep + r, off])
        return jnp.concatenate(cols, axis=1)

    def tied(s):
        eq = jnp.where(s == thr, 1.0, 0.0)
        neg0 = eq * jnp.where(lax.bitcast_convert_type(s, I32) == int_min, 1.0, 0.0)
        return eq - neg0, neg0

    def attend(v):
        tots = []
        for pr in range(v):
            e2 = jnp.concatenate(tied(sc_ref[slot, pr]), axis=1).astype(BF16)
            r = jnp.dot(below_diag, e2, preferred_element_type=F32)
            rk_ref[pr] = r
            tots.append(r[PAIR - 1:PAIR] + e2[PAIR - 1:PAIR].astype(F32))
        tot = tots[0]
        for t in tots[1:]:
            tot = tot + t
        need = (k_eff - cnt_ge).astype(F32) + tot[:, :QT] + tot[:, QT:]
        off = jnp.concatenate([jnp.zeros((1, QT), F32), tot[:, :QT]], axis=1)

        m8 = [jnp.full((SUBLANES, gl), NEG_INF, F32) for _ in range(N_KV_HEADS)]
        nks = quad_keys(v)
        for c, nk in enumerate(nks):
            masks = []
            for u in range(nk // PAIR):
                pr = c * pairs_per_quad + u
                s = sc_ref[slot, pr]
                r = rk_ref[pr] + off
                off = off + tots[pr]
                neg0 = lax.bitcast_convert_type(s, I32) == int_min
                keep_tie = jnp.where(jnp.where(neg0, r[:, QT:], r[:, :QT]) < need, 0.0, NEG_INF)
                masks.append(jnp.where(s > thr, 0.0, jnp.where(s == thr, keep_tie, NEG_INF)))
            mask = jnp.concatenate(masks, axis=0)
            mask = jnp.concatenate([mask] * rep, axis=1)
            keys = slice(c * QUAD, c * QUAD + nk)
            for g in range(N_KV_HEADS):
                kt = kv_ref[0, keys, g * HEAD_DIM:(g + 1) * HEAD_DIM]
                lg = lax.dot_general(kt, qgs[g], nt_dims, preferred_element_type=F32) + mask
                if c == len(nks) - 1:
                    lg = lg + jnp.concatenate([bias_rows(c, g, u) for u in range(nk // TILE)], axis=0)
                elif c == len(nks) - 2 and v % 2 == 1:
                    u = tiles_per_quad - 1
                    lg = jnp.concatenate([lg[:u * TILE], lg[u * TILE:] + bias_rows(c, g, u)], axis=0)
                lg_ref[c, g, :nk] = lg
                m8[g] = jnp.maximum(m8[g], lg.reshape(nk // SUBLANES, SUBLANES, gl).max(axis=0))

        pieces = (score_quads(v + 1, 1 - slot, j + 1, qi_next_ref, dtw_next_ref)
                  if v < seq_len // PAIR else [])

        outs = []
        for g in range(N_KV_HEADS):
            m_row = jnp.max(m8[g], axis=0, keepdims=True)
            acc = jnp.zeros((PV_ROWS, gl), F32)
            for c, nk in enumerate(nks):
                p = jnp.exp2((lg_ref[c, g, :nk] - m_row).astype(BF16))
                vt = vt_ref[g * PV_ROWS:(g + 1) * PV_ROWS, c * QUAD:c * QUAD + nk]
                acc = acc + jnp.dot(vt, p, preferred_element_type=F32)
                if pieces:
                    pieces.pop(0)()
            if g == N_KV_HEADS - 1:
                for piece in pieces:
                    piece()
            o_g = acc[:HEAD_DIM] / acc[HEAD_DIM:HEAD_DIM + 1]
            outs += [o_g[:, r * QT:(r + 1) * QT] for r in range(rep)]
        o_ref[0] = jnp.concatenate(outs, axis=0).T.astype(BF16)

    for v in range(1, seq_len // PAIR + 1):
        pl.when(npair == v)(functools.partial(attend, v))


def _attention(q3, kv3, qi3, ki3, dtw3, bias, n_sel):
    bsz, seq_len, _ = q3.shape
    assert seq_len % QUAD == 0 and QUAD % QT == 0
    rep = N_ATTN_HEADS // N_KV_HEADS
    kern = functools.partial(_attn_kernel, seq_len=seq_len, n_sel=n_sel)
    steps = seq_len // QT
    this_step = lambda b, j: (b, j, 0)
    next_step = lambda b, j: (b, jnp.minimum(j + 1, steps - 1), 0)
    return pl.pallas_call(
        kern,
        grid=(bsz, steps),
        in_specs=[
            pl.BlockSpec((1, QT, ATTN_WIDTH), this_step),
            pl.BlockSpec((1, seq_len, KV_W), lambda b, j: (b, 0, 0)),
            pl.BlockSpec((1, QT, QI_W), this_step),
            pl.BlockSpec((1, seq_len, IDX_DIM), lambda b, j: (b, 0, 0)),
            pl.BlockSpec((1, QT, LANES), this_step),
            pl.BlockSpec((1, QT, QI_W), next_step),
            pl.BlockSpec((1, QT, LANES), next_step),
            pl.BlockSpec((N_ATTN_HEADS, 3, TILE, TILE), lambda b, j: (0, 0, 0, 0)),
        ],
        out_specs=pl.BlockSpec((1, QT, ATTN_WIDTH), this_step),
        out_shape=jax.ShapeDtypeStruct((bsz, seq_len, ATTN_WIDTH), BF16),
        scratch_shapes=[
            pltpu.VMEM((2, seq_len // PAIR, PAIR, QT), F32),
            pltpu.VMEM((2, seq_len // PAIR, PAIR, QT), I16),
            pltpu.VMEM((2, seq_len // PAIR, PAIR, QT), I16),
            pltpu.VMEM((seq_len // PAIR, PAIR, 2 * QT), F32),
            pltpu.VMEM((N_KV_HEADS * PV_ROWS, seq_len), BF16),
            pltpu.VMEM((seq_len // QUAD, N_KV_HEADS, QUAD, rep * QT), F32),
        ],
        compiler_params=pltpu.CompilerParams(
            dimension_semantics=("arbitrary", "arbitrary"), vmem_limit_bytes=VMEM_LIMIT),
        name="sparse_attn",
    )(q3, kv3, qi3, ki3, dtw3, qi3, dtw3, bias)


SSD_STEP_CHUNKS = 2
SSD_STEP = SSD_STEP_CHUNKS * TILE


def _split_bf16(v, n):
    parts = []
    for _ in range(n):
        p = v.astype(BF16)
        parts.append(p)
        v = v - p.astype(F32)
    return parts


def _dot_01(v, mat01, n):
    out = None
    for p in _split_bf16(v, n):
        t = jnp.dot(p, mat01, preferred_element_type=F32)
        out = t if out is None else out + t
    return out


def _ssd_kernel(z_ref, xbc_ref, dtw_ref, cw_ref, cb_ref, dtb_ref, alog_ref, dsk_ref, nw_ref,
                o_ref, xpad_ref, st_ref):
    step = pl.program_id(1)
    gn = SSD_N_GROUPS * SSD_D_STATE
    hpg = SSD_N_HEADS // SSD_N_GROUPS
    gw = hpg * SSD_HEAD_DIM

    @pl.when(step == 0)
    def _():
        xpad_ref[0:SUBLANES, :] = jnp.zeros((SUBLANES, CONV_DIM), F32)
        st_ref[...] = jnp.zeros_like(st_ref)

    xpad_ref[SUBLANES:, :] = xbc_ref[0]
    xfull = xpad_ref[...]
    conv = cb_ref[...] + cw_ref[CONV_WIDTH - 1:CONV_WIDTH, :] * xfull[SUBLANES:]
    for k in range(CONV_WIDTH - 1):
        shifted = pltpu.roll(xfull, CONV_WIDTH - 1 - k, 0)
        conv = conv + cw_ref[k:k + 1, :] * shifted[SUBLANES:]
    xpad_ref[0:SUBLANES, :] = xfull[SSD_STEP:]
    act = conv * jax.nn.sigmoid(conv)
    xs = act[:, :SSD_D_INNER]
    bm = act[:, SSD_D_INNER:SSD_D_INNER + gn].astype(BF16)
    cm = act[:, SSD_D_INNER + gn:].astype(BF16)

    dt_in = dtw_ref[0] + dtb_ref[...]
    dt = jnp.maximum(dt_in, 0.0) + jnp.log1p(jnp.exp(-jnp.abs(dt_in)))
    adt = dt * (-jnp.exp(alog_ref[...]))
    row = lax.broadcasted_iota(I32, (SSD_STEP, SSD_STEP), 0)
    col = lax.broadcasted_iota(I32, (SSD_STEP, SSD_STEP), 1)
    same_chunk = (row // TILE) == (col // TILE)
    chunk_tril = jnp.where(same_chunk & (row >= col), 1.0, 0.0).astype(BF16)
    acs = None
    for p in _split_bf16(adt, 3):
        t = jnp.dot(chunk_tril, p, preferred_element_type=F32)
        acs = t if acs is None else acs + t
    acs_t = acs.T
    causal = (lax.broadcasted_iota(I32, (TILE, TILE), 0) >= lax.broadcasted_iota(I32, (TILE, TILE), 1))
    a_last = [acs[(c + 1) * TILE - 1:(c + 1) * TILE, :] for c in range(SSD_STEP_CHUNKS)]
    out_decay = jnp.concatenate(
        [jnp.exp(a_last[c] - acs[c * TILE:(c + 1) * TILE]) for c in range(SSD_STEP_CHUNKS)], axis=0)
    chunk_decay = jnp.concatenate(
        [jnp.exp(a) for a in a_last] + [jnp.zeros((SUBLANES - SSD_STEP_CHUNKS, LANES), F32)], axis=0)
    hsel = (lax.broadcasted_iota(I32, (LANES, SSD_D_INNER), 0)
            == lax.broadcasted_iota(I32, (LANES, SSD_D_INNER), 1) // SSD_HEAD_DIM).astype(BF16)
    expanded = _dot_01(jnp.concatenate([dt, jnp.exp(acs), out_decay, chunk_decay], axis=0), hsel, 2)
    dt_x = expanded[:SSD_STEP]
    in_decay_x = expanded[SSD_STEP:2 * SSD_STEP]
    out_decay_x = expanded[2 * SSD_STEP:3 * SSD_STEP]
    chunk_decay_x = expanded[3 * SSD_STEP:]

    x_dt = xs * dt_x
    x_dt_b = x_dt.astype(BF16)
    x_out_b = (x_dt * out_decay_x).astype(BF16)
    tn_dims = (((0,), (0,)), ((), ()))
    nt_dims = (((1,), (1,)), ((), ()))
    states = [st_ref[g] for g in range(SSD_N_GROUPS)]
    y_chunks = []
    for c in range(SSD_STEP_CHUNKS):
        rows = slice(c * TILE, (c + 1) * TILE)
        y_parts = []
        for g in range(SSD_N_GROUPS):
            bg = bm[rows, g * SSD_D_STATE:(g + 1) * SSD_D_STATE]
            cg = cm[rows, g * SSD_D_STATE:(g + 1) * SSD_D_STATE]
            lanes = slice(g * gw, (g + 1) * gw)
            cb = lax.dot_general(cg, bg, nt_dims, preferred_element_type=F32)
            y_diag = []
            for r in range(hpg):
                h = g * hpg + r
                seg = jnp.where(causal, acs[rows, h:h + 1] - acs_t[h:h + 1, rows], NEG_INF)
                w = (cb * jnp.exp(seg)).astype(BF16)
                y_diag.append(jnp.dot(w, x_dt_b[rows, h * SSD_HEAD_DIM:(h + 1) * SSD_HEAD_DIM],
                                      preferred_element_type=F32))
            y_off = (jnp.dot(cg, states[g].astype(BF16), preferred_element_type=F32)
                     * in_decay_x[rows, lanes])
            y_parts.append(jnp.concatenate(y_diag, axis=-1) + y_off)
            st_new = lax.dot_general(bg, x_out_b[rows, lanes], tn_dims, preferred_element_type=F32)
            states[g] = states[g] * chunk_decay_x[c:c + 1, lanes] + st_new
        y_chunks.append(jnp.concatenate(y_parts, axis=-1))
    for g in range(SSD_N_GROUPS):
        st_ref[g] = states[g]

    y = jnp.concatenate(y_chunks, axis=0) + xs * dsk_ref[...]
    zv = z_ref[0]
    y = y * (zv * jax.nn.sigmoid(zv))
    ng = SSD_D_INNER // SSD_N_GROUPS
    outs = []
    for g in range(SSD_N_GROUPS):
        yg = y[:, g * ng:(g + 1) * ng]
        outs.append(yg * lax.rsqrt(jnp.mean(yg * yg, axis=-1, keepdims=True) + EPS))
    o_ref[0] = (jnp.concatenate(outs, axis=-1) * nw_ref[...]).astype(BF16)


def _ssd(z3, xbc3, dtw3, conv_w, conv_b, dt_bias, a_log, d_skip, norm_w):
    bsz, seq_len, _ = z3.shape
    assert seq_len % SSD_STEP == 0
    nc = seq_len // SSD_STEP
    blk = lambda w: pl.BlockSpec((1, SSD_STEP, w), lambda b, c: (b, c, 0))
    par = lambda r, w: pl.BlockSpec((r, w), lambda b, c: (0, 0))
    return pl.pallas_call(
        _ssd_kernel,
        grid=(bsz, nc),
        in_specs=[blk(SSD_D_INNER), blk(CONV_DIM), blk(LANES),
                  par(CONV_WIDTH, CONV_DIM), par(1, CONV_DIM), par(1, LANES), par(1, LANES),
                  par(1, SSD_D_INNER), par(1, SSD_D_INNER)],
        out_specs=blk(SSD_D_INNER),
        out_shape=jax.ShapeDtypeStruct((bsz, seq_len, SSD_D_INNER), BF16),
        scratch_shapes=[
            pltpu.VMEM((SUBLANES + SSD_STEP, CONV_DIM), F32),
            pltpu.VMEM((SSD_N_GROUPS, SSD_D_STATE, SSD_D_INNER // SSD_N_GROUPS), F32),
        ],
        compiler_params=pltpu.CompilerParams(
            dimension_semantics=("arbitrary", "arbitrary"), vmem_limit_bytes=VMEM_LIMIT),
        name="ssd_mixer",
    )(z3, xbc3, dtw3, conv_w, conv_b, dt_bias, a_log, d_skip, norm_w)


FF_CHUNK = 512


def _mlp_kernel(x_ref, attn_ref, ssd_ref, wo_ref, g1_ref, g2_ref, wu_ref, wd_ref, g3_ref, o_ref):
    mix = (jnp.dot(attn_ref[...], wo_ref[:ATTN_WIDTH, :], preferred_element_type=F32)
           + jnp.dot(ssd_ref[...], wo_ref[ATTN_WIDTH:, :], preferred_element_type=F32))
    h1 = x_ref[...] + _rms(mix, g1_ref[...])
    u = _rms(h1, g2_ref[...]).astype(BF16)
    acc = jnp.zeros(h1.shape, F32)
    for c in range(0, D_FF, FF_CHUNK):
        f = jnp.dot(u, wu_ref[:, c:c + FF_CHUNK], preferred_element_type=F32)
        f = jnp.square(jnp.maximum(f, 0.0)).astype(BF16)
        acc = acc + jnp.dot(f, wd_ref[c:c + FF_CHUNK, :], preferred_element_type=F32)
    o_ref[...] = h1 + _rms(acc, g3_ref[...])


def _mlp(x2, attn2, ssd2, wo, g1, g2, wu, wd, g3, tm):
    n = x2.shape[0]
    row = lambda i: (i, 0)
    const = lambda i: (0, 0)
    single = dict(pipeline_mode=pl.Buffered(1))
    return pl.pallas_call(
        _mlp_kernel,
        grid=(n // tm,),
        in_specs=[
            pl.BlockSpec((tm, D_MODEL), row),
            pl.BlockSpec((tm, ATTN_WIDTH), row),
            pl.BlockSpec((tm, SSD_D_INNER), row),
            pl.BlockSpec((ATTN_WIDTH + SSD_D_INNER, D_MODEL), const, **single),
            pl.BlockSpec((1, D_MODEL), const),
            pl.BlockSpec((1, D_MODEL), const),
            pl.BlockSpec((D_MODEL, D_FF), const, **single),
            pl.BlockSpec((D_FF, D_MODEL), const, **single),
            pl.BlockSpec((1, D_MODEL), const),
        ],
        out_specs=pl.BlockSpec((tm, D_MODEL), row),
        out_shape=jax.ShapeDtypeStruct((n, D_MODEL), F32),
        compiler_params=pltpu.CompilerParams(
            dimension_semantics=("arbitrary",), vmem_limit_bytes=VMEM_LIMIT),
        name="out_proj_mlp",
    )(x2, attn2, ssd2, wo, g1, g2, wu, wd, g3)


def _pad_lanes(v, n):
    return jnp.pad(v, (0, n - v.shape[0])).reshape(1, n)


def kernel(x, norm_pre_mix, norm_post_mix, norm_pre_mlp, norm_post_mlp, w_in, k_idx_ln_w, k_idx_ln_b, conv_w, conv_b, dt_bias, a_log, d_skip, ssd_norm_w, w_out, w_mlp_up, w_mlp_down, rel_bias):
    bsz, seq_len, d = x.shape
    n = bsz * seq_len
    assert d == D_MODEL and seq_len % TILE == 0
    tm = next(t for t in (1024, 512, TILE) if n % t == 0)
    n_sel = min(TOPK_MAX, seq_len // 4)
    bias = _bias_tiles(rel_bias)
    h = x.reshape(n, d)
    for i in range(norm_pre_mix.shape[0]):
        row = lambda v: v[i].reshape(1, -1)
        q, kv, qi, ki, z, xbc, dtw = _in_proj(h, row(norm_pre_mix), w_in[i],
                                              row(k_idx_ln_w), row(k_idx_ln_b), tm)
        r3 = lambda a: a.reshape(bsz, seq_len, a.shape[-1])
        dtw3 = r3(dtw)
        attn = _attention(r3(q), r3(kv), r3(qi), r3(ki), dtw3, bias, n_sel)
        ssd = _ssd(r3(z), r3(xbc), dtw3, conv_w[i], row(conv_b),
                   _pad_lanes(dt_bias[i], LANES), _pad_lanes(a_log[i], LANES),
                   jnp.repeat(d_skip[i], SSD_HEAD_DIM).reshape(1, -1), row(ssd_norm_w))
        h = _mlp(h, attn.reshape(n, -1), ssd.reshape(n, -1), w_out[i].astype(BF16),
                 row(norm_post_mix), row(norm_pre_mlp), w_mlp_up[i].astype(BF16),
                 w_mlp_down[i].astype(BF16), row(norm_post_mlp), tm)
    return h.reshape(bsz, seq_len, d)
```

```python
import functools
import math

import jax
import jax.numpy as jnp
from jax import lax
from jax.experimental import pallas as pl
from jax.experimental.pallas import tpu as pltpu

F32 = jnp.float32
BF16 = jnp.bfloat16
I32 = jnp.int32

D_MODEL = 1024
N_ATTN_HEADS = 8
N_KV_HEADS = 2
HEAD_DIM = 64
ATTN_WIDTH = N_ATTN_HEADS * HEAD_DIM
N_IDX_HEADS = 4
IDX_DIM = 64
TOPK_MAX = 256
NUM_BUCKETS = 32
MAX_DISTANCE = 128
SSD_D_INNER = 512
SSD_HEAD_DIM = 64
SSD_N_HEADS = 8
SSD_N_GROUPS = 2
SSD_D_STATE = 128
CONV_WIDTH = 4
CONV_DIM = SSD_D_INNER + 2 * SSD_N_GROUPS * SSD_D_STATE
D_FF = 4 * D_MODEL
EPS = 1e-6

LANES = 128
SUBLANES = 8
TILE = 128
VMEM_LIMIT = 56 * 1024 * 1024

QKV_W = ATTN_WIDTH + 2 * N_KV_HEADS * HEAD_DIM + N_IDX_HEADS * IDX_DIM
KV_W = 2 * N_KV_HEADS * HEAD_DIM
QI_W = N_IDX_HEADS * IDX_DIM
COL_QKV = 0
COL_KV = COL_QKV + ATTN_WIDTH
COL_QI = COL_KV + KV_W
COL_KI = COL_QKV + QKV_W
COL_Z = COL_KI + LANES
COL_XBC = COL_Z + SSD_D_INNER
COL_DTW = COL_XBC + CONV_DIM
W_CAT = COL_DTW + LANES
DTW_WI = SSD_N_HEADS
SRC_QKV = 0
SRC_KI = SRC_QKV + QKV_W
SRC_WI = SRC_KI + IDX_DIM
SRC_Z = SRC_WI + N_IDX_HEADS
SRC_XBC = SRC_Z + SSD_D_INNER
SRC_DT = SRC_XBC + CONV_DIM

NEG_INF = float("-inf")
LOG2E = 1.4426950408889634
Q_SCALE = LOG2E * HEAD_DIM ** -0.5


def _rms(x, g):
    return x * lax.rsqrt(jnp.mean(x * x, axis=-1, keepdims=True) + EPS) * g


W_PACK_ROWS = 128


def _in_proj_kernel(x_ref, g_ref, win_ref, lnw_ref, lnb_ref,
                    q_ref, kv_ref, qi_ref, ki_ref, z_ref, xbc_ref, dtw_ref, w_ref):
    @pl.when(pl.program_id(0) == 0)
    def _():
        def pack_rows(r, carry):
            rows = pl.ds(pl.multiple_of(r * W_PACK_ROWS, W_PACK_ROWS), W_PACK_ROWS)
            src = lambda lo, width: win_ref[rows, lo:lo + width]
            zeros = lambda width: jnp.zeros((W_PACK_ROWS, width), F32)
            w_ref[rows, COL_QKV:COL_KI] = src(SRC_QKV, QKV_W).astype(BF16)
            w_ref[rows, COL_KI:COL_Z] = jnp.concatenate(
                [src(SRC_KI, IDX_DIM), zeros(LANES - IDX_DIM)], axis=1).astype(BF16)
            w_ref[rows, COL_Z:COL_XBC] = src(SRC_Z, SSD_D_INNER).astype(BF16)
            w_ref[rows, COL_XBC:COL_DTW] = src(SRC_XBC, CONV_DIM).astype(BF16)
            w_ref[rows, COL_DTW:W_CAT] = jnp.concatenate(
                [src(SRC_DT, SSD_N_HEADS), src(SRC_WI, N_IDX_HEADS),
                 zeros(LANES - SSD_N_HEADS - N_IDX_HEADS)], axis=1).astype(BF16)
            return carry

        lax.fori_loop(0, D_MODEL // W_PACK_ROWS, pack_rows, 0)

    u = _rms(x_ref[...], g_ref[...]).astype(BF16)

    def mm(lo, hi):
        return jnp.dot(u, w_ref[:, lo:hi], preferred_element_type=F32)

    q_ref[...] = (mm(COL_QKV, COL_KV) * Q_SCALE).astype(BF16)
    kv_ref[...] = mm(COL_KV, COL_QI).astype(BF16)
    qi_ref[...] = mm(COL_QI, COL_KI).astype(BF16)
    ki = mm(COL_KI, COL_Z)[:, :IDX_DIM]
    mu = jnp.mean(ki, axis=-1, keepdims=True)
    var = jnp.mean(jnp.square(ki - mu), axis=-1, keepdims=True)
    ki_ref[...] = ((ki - mu) * lax.rsqrt(var + EPS) * lnw_ref[...] + lnb_ref[...]).astype(BF16)
    z_ref[...] = mm(COL_Z, COL_XBC)
    xbc_ref[...] = mm(COL_XBC, COL_DTW)
    dtw_ref[...] = mm(COL_DTW, W_CAT)


def _in_proj(x2, g, w_in, lnw, lnb, tm):
    n = x2.shape[0]
    assert w_in.shape == (D_MODEL, SRC_DT + SSD_N_HEADS)
    row = lambda i: (i, 0)
    const = lambda i: (0, 0)
    return pl.pallas_call(
        _in_proj_kernel,
        grid=(n // tm,),
        in_specs=[
            pl.BlockSpec((tm, D_MODEL), row),
            pl.BlockSpec((1, D_MODEL), const),
            pl.BlockSpec(w_in.shape, const, pipeline_mode=pl.Buffered(1)),
            pl.BlockSpec((1, IDX_DIM), const),
            pl.BlockSpec((1, IDX_DIM), const),
        ],
        out_specs=[
            pl.BlockSpec((tm, ATTN_WIDTH), row),
            pl.BlockSpec((tm, KV_W), row),
            pl.BlockSpec((tm, QI_W), row),
            pl.BlockSpec((tm, IDX_DIM), row),
            pl.BlockSpec((tm, SSD_D_INNER), row),
            pl.BlockSpec((tm, CONV_DIM), row),
            pl.BlockSpec((tm, LANES), row),
        ],
        out_shape=[
            jax.ShapeDtypeStruct((n, ATTN_WIDTH), BF16),
            jax.ShapeDtypeStruct((n, KV_W), BF16),
            jax.ShapeDtypeStruct((n, QI_W), BF16),
            jax.ShapeDtypeStruct((n, IDX_DIM), BF16),
            jax.ShapeDtypeStruct((n, SSD_D_INNER), F32),
            jax.ShapeDtypeStruct((n, CONV_DIM), F32),
            jax.ShapeDtypeStruct((n, LANES), F32),
        ],
        scratch_shapes=[pltpu.VMEM((D_MODEL, W_CAT), BF16)],
        compiler_params=pltpu.CompilerParams(
            dimension_semantics=("arbitrary",), vmem_limit_bytes=VMEM_LIMIT),
        name="in_proj",
    )(x2, g, w_in, lnw, lnb)


def _bias_kernel(tbl_ref, out_ref):
    sk = lax.broadcasted_iota(I32, (TILE, TILE), 0)
    tq = lax.broadcasted_iota(I32, (TILE, TILE), 1)
    max_exact = NUM_BUCKETS // 2
    for off in range(2):
        dist = jnp.maximum(off * TILE + tq - sk, 0)
        df = jnp.maximum(dist, 1).astype(F32)
        large = max_exact + (jnp.log(df / max_exact) / math.log(MAX_DISTANCE / max_exact)
                             * (NUM_BUCKETS - max_exact)).astype(I32)
        large = jnp.minimum(large, NUM_BUCKETS - 1)
        bucket = jnp.where(dist < max_exact, dist, large)
        for h in range(N_ATTN_HEADS):
            acc = jnp.zeros((TILE, TILE), F32)
            for b in range(NUM_BUCKETS):
                acc = jnp.where(bucket == b, tbl_ref[b, h], acc)
            out_ref[h, off] = (acc - tbl_ref[NUM_BUCKETS - 1, h]) * LOG2E
    for h in range(N_ATTN_HEADS):
        out_ref[h, 2] = jnp.zeros((TILE, TILE), F32)


def _bias_tiles(rel_bias):
    return pl.pallas_call(
        _bias_kernel,
        in_specs=[pl.BlockSpec(memory_space=pltpu.SMEM)],
        out_specs=pl.BlockSpec(memory_space=pltpu.VMEM),
        out_shape=jax.ShapeDtypeStruct((N_ATTN_HEADS, 3, TILE, TILE), F32),
        name="bias_tiles",
    )(rel_bias)


def _key_to_f32(u):
    ks = u ^ I32(-2 ** 31)
    bits = jnp.where(ks >= 0, ks, ks ^ I32(0x7FFFFFFF))
    return lax.bitcast_convert_type(bits, F32)


PAIR = 2 * TILE
QUAD = 4 * TILE
QT = 2 * TILE
HALF_BITS = 16
HALF_MASK = 2 ** HALF_BITS - 1
HALF_BIAS = 2 ** (HALF_BITS - 1)
LO_BITS_FIRST = 10
I16 = jnp.int16
PV_ROWS = HEAD_DIM + 16


def _attn_kernel(q_ref, kv_ref, qi_ref, ki_ref, dtw_ref, qi_next_ref, dtw_next_ref, bias_ref, o_ref,
                 sc_ref, hi_ref, lo_ref, rk_ref, vt_ref, lg_ref, *, seq_len, n_sel):
    j = pl.program_id(1)
    i_hi = j * (QT // TILE) + QT // TILE - 1
    npair = i_hi // 2 + 1
    slot = lax.rem(j, 2)

    def quad_keys(v):
        return [QUAD] * (v // 2) + [PAIR] * (v % 2)

    rep = N_ATTN_HEADS // N_KV_HEADS
    gl = rep * QT
    nt_dims = (((1,), (1,)), ((), ()))
    v_lo = N_KV_HEADS * HEAD_DIM

    @pl.when(j == 0)
    def _():
        vt = kv_ref[0, :, v_lo:].astype(F32).T
        ones_row = jnp.where(lax.broadcasted_iota(I32, (PV_ROWS - HEAD_DIM, seq_len), 0) == 0, 1.0, 0.0)
        for g in range(N_KV_HEADS):
            vt_ref[g * PV_ROWS:g * PV_ROWS + HEAD_DIM, :] = vt[g * HEAD_DIM:(g + 1) * HEAD_DIM].astype(BF16)
            vt_ref[g * PV_ROWS + HEAD_DIM:(g + 1) * PV_ROWS, :] = ones_row.astype(BF16)

    idx_scale = (N_IDX_HEADS ** -0.5) * (IDX_DIM ** -0.5)
    s_loc = lax.broadcasted_iota(I32, (PAIR, QT), 0)
    s_minus_t = s_loc - lax.broadcasted_iota(I32, (PAIR, QT), 1)

    def score_quads(v, to_slot, step, qi_src, dtw_src):
        w_t = dtw_src[0].T
        w_rows = [w_t[DTW_WI + h:DTW_WI + h + 1, :] * idx_scale for h in range(N_IDX_HEADS)]
        qi = qi_src[0]
        qi_all = jnp.concatenate(
            [qi[:, h * IDX_DIM:(h + 1) * IDX_DIM] for h in range(N_IDX_HEADS)], axis=0)
        def piece(c, nk):
            kt = ki_ref[0, c * QUAD:c * QUAD + nk, :]
            d = lax.dot_general(kt, qi_all, nt_dims, preferred_element_type=F32)
            s = jnp.zeros((nk, QT), F32)
            for h in range(N_IDX_HEADS):
                s = s + w_rows[h] * jnp.maximum(d[:, h * QT:(h + 1) * QT], 0.0)
            for u in range(nk // PAIR):
                pr = c * (QUAD // PAIR) + u
                su = jnp.where(s_minus_t <= step * QT - pr * PAIR, s[u * PAIR:(u + 1) * PAIR], NEG_INF)
                sc_ref[to_slot, pr] = su
                bits = lax.bitcast_convert_type(su, I32)
                key = bits ^ (lax.shift_right_arithmetic(bits, I32(31)) & I32(0x7FFFFFFF))
                hi_ref[to_slot, pr] = lax.shift_right_arithmetic(key, I32(HALF_BITS)).astype(I16)
                lo_ref[to_slot, pr] = (key ^ I32(HALF_BIAS)).astype(I16)

        return [functools.partial(piece, c, nk) for c, nk in enumerate(quad_keys(v))]

    @pl.when(j == 0)
    def _():
        for piece in score_quads(1, 0, 0, qi_ref, dtw_ref):
            piece()

    t_glob = j * QT + lax.broadcasted_iota(I32, (1, QT), 1)
    k_eff = jnp.minimum(n_sel, t_glob + 1)
    acc_rows = 4 * 2 * SUBLANES
    one16, zero16 = jnp.ones((), I16), jnp.zeros((), I16)

    def fold(cnt, rows=acc_rows):
        parts = [cnt[k * rows:(k + 1) * rows] for k in range(cnt.shape[0] // rows)]
        while len(parts) > 1:
            parts = [a + b for a, b in zip(parts[::2], parts[1::2])]
        return parts[0]

    def total(acc):
        packed = fold(acc, 2 * SUBLANES).astype(I32)
        return (packed[:SUBLANES] + packed[SUBLANES:]).sum(axis=0, keepdims=True)

    def to16(u):
        return (u - I32(HALF_BIAS)).astype(I16)

    def search(v):
        def count_ge(plane_ref, cand16):
            acc = jnp.zeros((acc_rows, QT), I16)
            for pr in range(v):
                acc = acc + fold(jnp.where(plane_ref[slot, pr] >= cand16, one16, zero16))
            return total(acc)

        def search_step(b, carry, plane_ref, base):
            prefix, cnt_ge, settled = carry
            cand = prefix | lax.shift_left(I32(1), HALF_BITS - 1 - b)
            cnt = count_ge(plane_ref, to16(cand)) + base
            ok = cnt >= k_eff
            settled = jnp.where(ok, jnp.where(cnt == k_eff, 1, settled), settled)
            return jnp.where(ok, cand, prefix), jnp.where(ok, cnt, cnt_ge), settled

        carry = (jnp.zeros((1, QT), I32), t_glob + 1, jnp.zeros((1, QT), I32))
        hi_u, cnt_ge, settled = lax.fori_loop(
            0, HALF_BITS, functools.partial(search_step, plane_ref=hi_ref, base=0), carry)
        hi16 = to16(hi_u)

        acc = jnp.zeros((acc_rows, QT), I16)
        for pr in range(v):
            h = hi_ref[slot, pr]
            lo_ref[slot, pr] = jnp.where(h == hi16, lo_ref[slot, pr], I16(-HALF_BIAS))
            acc = acc + fold(jnp.where(h > hi16, one16, zero16))
        above = total(acc)
        lo_step = functools.partial(search_step, plane_ref=lo_ref, base=above)
        floor_tie = count_ge(lo_ref, to16(jnp.ones((1, QT), I32))) + above < k_eff
        carry = (jnp.zeros((1, QT), I32), cnt_ge, jnp.where(floor_tie, 1, settled))
        carry = lax.fori_loop(0, LO_BITS_FIRST, lo_step, carry)
        lo_u, cnt_ge, _ = lax.cond(
            jnp.min(carry[2]) > 0, lambda c: c,
            lambda c: lax.fori_loop(LO_BITS_FIRST, HALF_BITS, lo_step, c), carry)
        return _key_to_f32(lax.shift_left(hi_u, I32(HALF_BITS)) | lo_u), cnt_ge

    q = q_ref[0]
    qgs = [jnp.concatenate(
        [q[:, (g * rep + r) * HEAD_DIM:(g * rep + r + 1) * HEAD_DIM] for r in range(rep)],
        axis=0) for g in range(N_KV_HEADS)]
    below_diag = (lax.broadcasted_iota(I32, (PAIR, PAIR), 0)
                  > lax.broadcasted_iota(I32, (PAIR, PAIR), 1)).astype(BF16)
    tiles_per_quad = QUAD // TILE
    pairs_per_quad = QUAD // PAIR
    n_qt = QT // TILE
    int_min = I32(-2 ** 31)

    def bias_rows(c, g, u):
        cols = []
        for r in range(rep):
            for w in range(n_qt):
                off = jnp.clip(j * n_qt + w - (c * tiles_per_quad + u), 0, 2)
                cols.append(bias_ref[g * rep + r, off])
        return jnp.concatenate(cols, axis=1)

    def attend(v):
        thr, cnt_ge = search(v)

        def tied(s):
            eq = jnp.where(s == thr, 1.0, 0.0)
            neg0 = eq * jnp.where(lax.bitcast_convert_type(s, I32) == int_min, 1.0, 0.0)
            return eq - neg0, neg0

        tots = []
        for pr in range(v):
            e2 = jnp.concatenate(tied(sc_ref[slot, pr]), axis=1).astype(BF16)
            r = jnp.dot(below_diag, e2, preferred_element_type=F32)
            rk_ref[pr] = r
            tots.append(r[PAIR - 1:PAIR] + e2[PAIR - 1:PAIR].astype(F32))
        tot = tots[0]
        for t in tots[1:]:
            tot = tot + t
        need = (k_eff - cnt_ge).astype(F32) + tot[:, :QT] + tot[:, QT:]
        off = jnp.concatenate([jnp.zeros((1, QT), F32), tot[:, :QT]], axis=1)

        m8 = [jnp.full((SUBLANES, gl), NEG_INF, F32) for _ in range(N_KV_HEADS)]
        nks = quad_keys(v)
        for c, nk in enumerate(nks):
            masks = []
            for u in range(nk // PAIR):
                pr = c * pairs_per_quad + u
                s = sc_ref[slot, pr]
                r = rk_ref[pr] + off
                off = off + tots[pr]
                neg0 = lax.bitcast_convert_type(s, I32) == int_min
                keep_tie = jnp.where(jnp.where(neg0, r[:, QT:], r[:, :QT]) < need, 0.0, NEG_INF)
                masks.append(jnp.where(s > thr, 0.0, jnp.where(s == thr, keep_tie, NEG_INF)))
            mask = jnp.concatenate(masks, axis=0)
            mask = jnp.concatenate([mask] * rep, axis=1)
            keys = slice(c * QUAD, c * QUAD + nk)
            for g in range(N_KV_HEADS):
                kt = kv_ref[0, keys, g * HEAD_DIM:(g + 1) * HEAD_DIM]
                lg = lax.dot_general(kt, qgs[g], nt_dims, preferred_element_type=F32) + mask
                if c == len(nks) - 1:
                    lg = lg + jnp.concatenate([bias_rows(c, g, u) for u in range(nk // TILE)], axis=0)
                elif c == len(nks) - 2 and v % 2 == 1:
                    u = tiles_per_quad - 1
                    lg = jnp.concatenate([lg[:u * TILE], lg[u * TILE:] + bias_rows(c, g, u)], axis=0)
                lg_ref[c, g, :nk] = lg
                m8[g] = jnp.maximum(m8[g], lg.reshape(nk // SUBLANES, SUBLANES, gl).max(axis=0))

        pieces = (score_quads(v + 1, 1 - slot, j + 1, qi_next_ref, dtw_next_ref)
                  if v < seq_len // PAIR else [])

        outs = []
        for g in range(N_KV_HEADS):
            m_row = jnp.max(m8[g], axis=0, keepdims=True)
            acc = jnp.zeros((PV_ROWS, gl), F32)
            for c, nk in enumerate(nks):
                p = jnp.exp2((lg_ref[c, g, :nk] - m_row).astype(BF16))
                vt = vt_ref[g * PV_ROWS:(g + 1) * PV_ROWS, c * QUAD:c * QUAD + nk]
                acc = acc + jnp.dot(vt, p, preferred_element_type=F32)
                if pieces:
                    pieces.pop(0)()
            if g == N_KV_HEADS - 1:
                for piece in pieces:
                    piece()
            o_g = acc[:HEAD_DIM] / acc[HEAD_DIM:HEAD_DIM + 1]
            outs += [o_g[:, r * QT:(r + 1) * QT] for r in range(rep)]
        o_ref[0] = jnp.concatenate(outs, axis=0).T.astype(BF16)

    for v in range(1, seq_len // PAIR + 1):
        pl.when(npair == v)(functools.partial(attend, v))


def _attention(q3, kv3, qi3, ki3, dtw3, bias, n_sel):
    bsz, seq_len, _ = q3.shape
    assert seq_len % QUAD == 0 and QUAD % QT == 0
    rep = N_ATTN_HEADS // N_KV_HEADS
    kern = functools.partial(_attn_kernel, seq_len=seq_len, n_sel=n_sel)
    steps = seq_len // QT
    this_step = lambda b, j: (b, j, 0)
    next_step = lambda b, j: (b, jnp.minimum(j + 1, steps - 1), 0)
    return pl.pallas_call(
        kern,
        grid=(bsz, steps),
        in_specs=[
            pl.BlockSpec((1, QT, ATTN_WIDTH), this_step),
            pl.BlockSpec((1, seq_len, KV_W), lambda b, j: (b, 0, 0)),
            pl.BlockSpec((1, QT, QI_W), this_step),
            pl.BlockSpec((1, seq_len, IDX_DIM), lambda b, j: (b, 0, 0)),
            pl.BlockSpec((1, QT, LANES), this_step),
            pl.BlockSpec((1, QT, QI_W), next_step),
            pl.BlockSpec((1, QT, LANES), next_step),
            pl.BlockSpec((N_ATTN_HEADS, 3, TILE, TILE), lambda b, j: (0, 0, 0, 0)),
        ],
        out_specs=pl.BlockSpec((1, QT, ATTN_WIDTH), this_step),
        out_shape=jax.ShapeDtypeStruct((bsz, seq_len, ATTN_WIDTH), BF16),
        scratch_shapes=[
            pltpu.VMEM((2, seq_len // PAIR, PAIR, QT), F32),
            pltpu.VMEM((2, seq_len // PAIR, PAIR, QT), I16),
            pltpu.VMEM((2, seq_len // PAIR, PAIR, QT), I16),
            pltpu.VMEM((seq_len // PAIR, PAIR, 2 * QT), F32),
            pltpu.VMEM((N_KV_HEADS * PV_ROWS, seq_len), BF16),
            pltpu.VMEM((seq_len // QUAD, N_KV_HEADS, QUAD, rep * QT), F32),
        ],
        compiler_params=pltpu.CompilerParams(
            dimension_semantics=("arbitrary", "arbitrary"), vmem_limit_bytes=VMEM_LIMIT),
        name="sparse_attn",
    )(q3, kv3, qi3, ki3, dtw3, qi3, dtw3, bias)


SSD_STEP_CHUNKS = 2
SSD_STEP = SSD_STEP_CHUNKS * TILE


def _split_bf16(v, n):
    parts = []
    for _ in range(n):
        p = v.astype(BF16)
        parts.append(p)
        v = v - p.astype(F32)
    return parts


def _dot_01(v, mat01, n):
    out = None
    for p in _split_bf16(v, n):
        t = jnp.dot(p, mat01, preferred_element_type=F32)
        out = t if out is None else out + t
    return out


def _ssd_kernel(z_ref, xbc_ref, dtw_ref, cw_ref, cb_ref, dtb_ref, alog_ref, dsk_ref, nw_ref,
                o_ref, xpad_ref, st_ref):
    step = pl.program_id(1)
    gn = SSD_N_GROUPS * SSD_D_STATE
    hpg = SSD_N_HEADS // SSD_N_GROUPS
    gw = hpg * SSD_HEAD_DIM

    @pl.when(step == 0)
    def _():
        xpad_ref[0:SUBLANES, :] = jnp.zeros((SUBLANES, CONV_DIM), F32)
        st_ref[...] = jnp.zeros_like(st_ref)

    xpad_ref[SUBLANES:, :] = xbc_ref[0]
    xfull = xpad_ref[...]
    conv = cb_ref[...] + cw_ref[CONV_WIDTH - 1:CONV_WIDTH, :] * xfull[SUBLANES:]
    for k in range(CONV_WIDTH - 1):
        shifted = pltpu.roll(xfull, CONV_WIDTH - 1 - k, 0)
        conv = conv + cw_ref[k:k + 1, :] * shifted[SUBLANES:]
    xpad_ref[0:SUBLANES, :] = xfull[SSD_STEP:]
    act = conv * jax.nn.sigmoid(conv)
    xs = act[:, :SSD_D_INNER]
    bm = act[:, SSD_D_INNER:SSD_D_INNER + gn].astype(BF16)
    cm = act[:, SSD_D_INNER + gn:].astype(BF16)

    dt_in = dtw_ref[0] + dtb_ref[...]
    dt = jnp.maximum(dt_in, 0.0) + jnp.log1p(jnp.exp(-jnp.abs(dt_in)))
    adt = dt * (-jnp.exp(alog_ref[...]))
    row = lax.broadcasted_iota(I32, (SSD_STEP, SSD_STEP), 0)
    col = lax.broadcasted_iota(I32, (SSD_STEP, SSD_STEP), 1)
    same_chunk = (row // TILE) == (col // TILE)
    chunk_tril = jnp.where(same_chunk & (row >= col), 1.0, 0.0).astype(BF16)
    acs = None
    for p in _split_bf16(adt, 3):
        t = jnp.dot(chunk_tril, p, preferred_element_type=F32)
        acs = t if acs is None else acs + t
    acs_t = acs.T
    causal = (lax.broadcasted_iota(I32, (TILE, TILE), 0) >= lax.broadcasted_iota(I32, (TILE, TILE), 1))
    a_last = [acs[(c + 1) * TILE - 1:(c + 1) * TILE, :] for c in range(SSD_STEP_CHUNKS)]
    out_decay = jnp.concatenate(
        [jnp.exp(a_last[c] - acs[c * TILE:(c + 1) * TILE]) for c in range(SSD_STEP_CHUNKS)], axis=0)
    chunk_decay = jnp.concatenate(
        [jnp.exp(a) for a in a_last] + [jnp.zeros((SUBLANES - SSD_STEP_CHUNKS, LANES), F32)], axis=0)
    hsel = (lax.broadcasted_iota(I32, (LANES, SSD_D_INNER), 0)
            == lax.broadcasted_iota(I32, (LANES, SSD_D_INNER), 1) // SSD_HEAD_DIM).astype(BF16)
    expanded = _dot_01(jnp.concatenate([dt, jnp.exp(acs), out_decay, chunk_decay], axis=0), hsel, 2)
    dt_x = expanded[:SSD_STEP]
    in_decay_x = expanded[SSD_STEP:2 * SSD_STEP]
    out_decay_x = expanded[2 * SSD_STEP:3 * SSD_STEP]
    chunk_decay_x = expanded[3 * SSD_STEP:]

    x_dt = xs * dt_x
    x_dt_b = x_dt.astype(BF16)
    x_out_b = (x_dt * out_decay_x).astype(BF16)
    tn_dims = (((0,), (0,)), ((), ()))
    nt_dims = (((1,), (1,)), ((), ()))
    states = [st_ref[g] for g in range(SSD_N_GROUPS)]
    y_chunks = []
    for c in range(SSD_STEP_CHUNKS):
        rows = slice(c * TILE, (c + 1) * TILE)
        y_parts = []
        for g in range(SSD_N_GROUPS):
            bg = bm[rows, g * SSD_D_STATE:(g + 1) * SSD_D_STATE]
            cg = cm[rows, g * SSD_D_STATE:(g + 1) * SSD_D_STATE]
            lanes = slice(g * gw, (g + 1) * gw)
            cb = lax.dot_general(cg, bg, nt_dims, preferred_element_type=F32)
            y_diag = []
            for r in range(hpg):
                h = g * hpg + r
                seg = jnp.where(causal, acs[rows, h:h + 1] - acs_t[h:h + 1, rows], NEG_INF)
                w = (cb * jnp.exp(seg)).astype(BF16)
                y_diag.append(jnp.dot(w, x_dt_b[rows, h * SSD_HEAD_DIM:(h + 1) * SSD_HEAD_DIM],
                                      preferred_element_type=F32))
            y_off = (jnp.dot(cg, states[g].astype(BF16), preferred_element_type=F32)
                     * in_decay_x[rows, lanes])
            y_parts.append(jnp.concatenate(y_diag, axis=-1) + y_off)
            st_new = lax.dot_general(bg, x_out_b[rows, lanes], tn_dims, preferred_element_type=F32)
            states[g] = states[g] * chunk_decay_x[c:c + 1, lanes] + st_new
        y_chunks.append(jnp.concatenate(y_parts, axis=-1))
    for g in range(SSD_N_GROUPS):
        st_ref[g] = states[g]

    y = jnp.concatenate(y_chunks, axis=0) + xs * dsk_ref[...]
    zv = z_ref[0]
    y = y * (zv * jax.nn.sigmoid(zv))
    ng = SSD_D_INNER // SSD_N_GROUPS
    outs = []
    for g in range(SSD_N_GROUPS):
        yg = y[:, g * ng:(g + 1) * ng]
        outs.append(yg * lax.rsqrt(jnp.mean(yg * yg, axis=-1, keepdims=True) + EPS))
    o_ref[0] = (jnp.concatenate(outs, axis=-1) * nw_ref[...]).astype(BF16)


def _ssd(z3, xbc3, dtw3, conv_w, conv_b, dt_bias, a_log, d_skip, norm_w):
    bsz, seq_len, _ = z3.shape
    assert seq_len % SSD_STEP == 0
    nc = seq_len // SSD_STEP
    blk = lambda w: pl.BlockSpec((1, SSD_STEP, w), lambda b, c: (b, c, 0))
    par = lambda r, w: pl.BlockSpec((r, w), lambda b, c: (0, 0))
    return pl.pallas_call(
        _ssd_kernel,
        grid=(bsz, nc),
        in_specs=[blk(SSD_D_INNER), blk(CONV_DIM), blk(LANES),
                  par(CONV_WIDTH, CONV_DIM), par(1, CONV_DIM), par(1, LANES), par(1, LANES),
                  par(1, SSD_D_INNER), par(1, SSD_D_INNER)],
        out_specs=blk(SSD_D_INNER),
        out_shape=jax.ShapeDtypeStruct((bsz, seq_len, SSD_D_INNER), BF16),
        scratch_shapes=[
            pltpu.VMEM((SUBLANES + SSD_STEP, CONV_DIM), F32),
            pltpu.VMEM((SSD_N_GROUPS, SSD_D_STATE, SSD_D_INNER // SSD_N_GROUPS), F32),
        ],
        compiler_params=pltpu.CompilerParams(
            dimension_semantics=("arbitrary", "arbitrary"), vmem_limit_bytes=VMEM_LIMIT),
        name="ssd_mixer",
    )(z3, xbc3, dtw3, conv_w, conv_b, dt_bias, a_log, d_skip, norm_w)


FF_CHUNK = 1024


def _mlp_kernel(x_ref, attn_ref, ssd_ref, wo_ref, g1_ref, g2_ref, wu_ref, wd_ref, g3_ref, o_ref):
    mix = (jnp.dot(attn_ref[...], wo_ref[:ATTN_WIDTH, :], preferred_element_type=F32)
           + jnp.dot(ssd_ref[...], wo_ref[ATTN_WIDTH:, :], preferred_element_type=F32))
    h1 = x_ref[...] + _rms(mix, g1_ref[...])
    u = _rms(h1, g2_ref[...]).astype(BF16)
    acc = jnp.zeros(h1.shape, F32)
    for c in range(0, D_FF, FF_CHUNK):
        f = jnp.dot(u, wu_ref[:, c:c + FF_CHUNK], preferred_element_type=F32)
        f = jnp.square(jnp.maximum(f, 0.0)).astype(BF16)
        acc = acc + jnp.dot(f, wd_ref[c:c + FF_CHUNK, :], preferred_element_type=F32)
    o_ref[...] = h1 + _rms(acc, g3_ref[...])


def _mlp(x2, attn2, ssd2, wo, g1, g2, wu, wd, g3, tm):
    n = x2.shape[0]
    row = lambda i: (i, 0)
    const = lambda i: (0, 0)
    single = dict(pipeline_mode=pl.Buffered(1))
    return pl.pallas_call(
        _mlp_kernel,
        grid=(n // tm,),
        in_specs=[
            pl.BlockSpec((tm, D_MODEL), row),
            pl.BlockSpec((tm, ATTN_WIDTH), row),
            pl.BlockSpec((tm, SSD_D_INNER), row),
            pl.BlockSpec((ATTN_WIDTH + SSD_D_INNER, D_MODEL), const, **single),
            pl.BlockSpec((1, D_MODEL), const),
            pl.BlockSpec((1, D_MODEL), const),
            pl.BlockSpec((D_MODEL, D_FF), const, **single),
            pl.BlockSpec((D_FF, D_MODEL), const, **single),
            pl.BlockSpec((1, D_MODEL), const),
        ],
        out_specs=pl.BlockSpec((tm, D_MODEL), row),
        out_shape=jax.ShapeDtypeStruct((n, D_MODEL), F32),
        compiler_params=pltpu.CompilerParams(
            dimension_semantics=("arbitrary",), vmem_limit_bytes=VMEM_LIMIT),
        name="out_proj_mlp",
    )(x2, attn2, ssd2, wo, g1, g2, wu, wd, g3)


def _pad_lanes(v, n):
    return jnp.pad(v, (0, n - v.shape[0])).reshape(1, n)


def kernel(x, norm_pre_mix, norm_post_mix, norm_pre_mlp, norm_post_mlp, w_in, k_idx_ln_w, k_idx_ln_b, conv_w, conv_b, dt_bias, a_log, d_skip, ssd_norm_w, w_out, w_mlp_up, w_mlp_down, rel_bias):
    bsz, seq_len, d = x.shape
    n = bsz * seq_len
    assert d == D_MODEL and seq_len % TILE == 0
    tm = next(t for t in (1024, 512, TILE) if n % t == 0)
    n_sel = min(TOPK_MAX, seq_len // 4)
    bias = _bias_tiles(rel_bias)
    h = x.reshape(n, d)
    for i in range(norm_pre_mix.shape[0]):
        row = lambda v: v[i].reshape(1, -1)
        q, kv, qi, ki, z, xbc, dtw = _in_proj(h, row(norm_pre_mix), w_in[i],
                                              row(k_idx_ln_w), row(k_idx_ln_b), tm)
        r3 = lambda a: a.reshape(bsz, seq_len, a.shape[-1])
        dtw3 = r3(dtw)
        attn = _attention(r3(q), r3(kv), r3(qi), r3(ki), dtw3, bias, n_sel)
        ssd = _ssd(r3(z), r3(xbc), dtw3, conv_w[i], row(conv_b),
                   _pad_lanes(dt_bias[i], LANES), _pad_lanes(a_log[i], LANES),
                   jnp.repeat(d_skip[i], SSD_HEAD_DIM).reshape(1, -1), row(ssd_norm_w))
        h = _mlp(h, attn.reshape(n, -1), ssd.reshape(n, -1), w_out[i].astype(BF16),
                 row(norm_post_mix), row(norm_pre_mlp), w_mlp_up[i].astype(BF16),
                 w_mlp_down[i].astype(BF16), row(norm_post_mlp), tm)
    return h.reshape(bsz, seq_len, d)
```

```python
import functools
import math

import jax
import jax.numpy as jnp
from jax import lax
from jax.experimental import pallas as pl
from jax.experimental.pallas import tpu as pltpu

F32 = jnp.float32
BF16 = jnp.bfloat16
I32 = jnp.int32

D_MODEL = 1024
N_ATTN_HEADS = 8
N_KV_HEADS = 2
HEAD_DIM = 64
ATTN_WIDTH = N_ATTN_HEADS * HEAD_DIM
N_IDX_HEADS = 4
IDX_DIM = 64
TOPK_MAX = 256
NUM_BUCKETS = 32
MAX_DISTANCE = 128
SSD_D_INNER = 512
SSD_HEAD_DIM = 64
SSD_N_HEADS = 8
SSD_N_GROUPS = 2
SSD_D_STATE = 128
CONV_WIDTH = 4
CONV_DIM = SSD_D_INNER + 2 * SSD_N_GROUPS * SSD_D_STATE
D_FF = 4 * D_MODEL
EPS = 1e-6

LANES = 128
SUBLANES = 8
TILE = 128
VMEM_LIMIT = 56 * 1024 * 1024

QKV_W = ATTN_WIDTH + 2 * N_KV_HEADS * HEAD_DIM + N_IDX_HEADS * IDX_DIM
KV_W = 2 * N_KV_HEADS * HEAD_DIM
QI_W = N_IDX_HEADS * IDX_DIM
COL_QKV = 0
COL_KV = COL_QKV + ATTN_WIDTH
COL_QI = COL_KV + KV_W
COL_KI = COL_QKV + QKV_W
COL_Z = COL_KI + LANES
COL_XBC = COL_Z + SSD_D_INNER
COL_DTW = COL_XBC + CONV_DIM
W_CAT = COL_DTW + LANES
DTW_WI = SSD_N_HEADS
SRC_QKV = 0
SRC_KI = SRC_QKV + QKV_W
SRC_WI = SRC_KI + IDX_DIM
SRC_Z = SRC_WI + N_IDX_HEADS
SRC_XBC = SRC_Z + SSD_D_INNER
SRC_DT = SRC_XBC + CONV_DIM

NEG_INF = float("-inf")
LOG2E = 1.4426950408889634
Q_SCALE = LOG2E * HEAD_DIM ** -0.5


def _rms(x, g):
    return x * lax.rsqrt(jnp.mean(x * x, axis=-1, keepdims=True) + EPS) * g


W_PACK_ROWS = 128


def _in_proj_kernel(x_ref, g_ref, win_ref, lnw_ref, lnb_ref,
                    q_ref, kv_ref, qi_ref, ki_ref, z_ref, xbc_ref, dtw_ref, w_ref):
    @pl.when(pl.program_id(0) == 0)
    def _():
        def pack_rows(r, carry):
            rows = pl.ds(pl.multiple_of(r * W_PACK_ROWS, W_PACK_ROWS), W_PACK_ROWS)
            src = lambda lo, width: win_ref[rows, lo:lo + width]
            zeros = lambda width: jnp.zeros((W_PACK_ROWS, width), F32)
            w_ref[rows, COL_QKV:COL_KI] = src(SRC_QKV, QKV_W).astype(BF16)
            w_ref[rows, COL_KI:COL_Z] = jnp.concatenate(
                [src(SRC_KI, IDX_DIM), zeros(LANES - IDX_DIM)], axis=1).astype(BF16)
            w_ref[rows, COL_Z:COL_XBC] = src(SRC_Z, SSD_D_INNER).astype(BF16)
            w_ref[rows, COL_XBC:COL_DTW] = src(SRC_XBC, CONV_DIM).astype(BF16)
            w_ref[rows, COL_DTW:W_CAT] = jnp.concatenate(
                [src(SRC_DT, SSD_N_HEADS), src(SRC_WI, N_IDX_HEADS),
                 zeros(LANES - SSD_N_HEADS - N_IDX_HEADS)], axis=1).astype(BF16)
            return carry

        lax.fori_loop(0, D_MODEL // W_PACK_ROWS, pack_rows, 0)

    u = _rms(x_ref[...], g_ref[...]).astype(BF16)

    def mm(lo, hi):
        return jnp.dot(u, w_ref[:, lo:hi], preferred_element_type=F32)

    q_ref[...] = (mm(COL_QKV, COL_KV) * Q_SCALE).astype(BF16)
    kv_ref[...] = mm(COL_KV, COL_QI).astype(BF16)
    qi_ref[...] = mm(COL_QI, COL_KI).astype(BF16)
    ki = mm(COL_KI, COL_Z)[:, :IDX_DIM]
    mu = jnp.mean(ki, axis=-1, keepdims=True)
    var = jnp.mean(jnp.square(ki - mu), axis=-1, keepdims=True)
    ki_ref[...] = ((ki - mu) * lax.rsqrt(var + EPS) * lnw_ref[...] + lnb_ref[...]).astype(BF16)
    z_ref[...] = mm(COL_Z, COL_XBC)
    xbc_ref[...] = mm(COL_XBC, COL_DTW)
    dtw_ref[...] = mm(COL_DTW, W_CAT)


def _in_proj(x2, g, w_in, lnw, lnb, tm):
    n = x2.shape[0]
    assert w_in.shape == (D_MODEL, SRC_DT + SSD_N_HEADS)
    row = lambda i: (i, 0)
    const = lambda i: (0, 0)
    return pl.pallas_call(
        _in_proj_kernel,
        grid=(n // tm,),
        in_specs=[
            pl.BlockSpec((tm, D_MODEL), row),
            pl.BlockSpec((1, D_MODEL), const),
            pl.BlockSpec(w_in.shape, const, pipeline_mode=pl.Buffered(1)),
            pl.BlockSpec((1, IDX_DIM), const),
            pl.BlockSpec((1, IDX_DIM), const),
        ],
        out_specs=[
            pl.BlockSpec((tm, ATTN_WIDTH), row),
            pl.BlockSpec((tm, KV_W), row),
            pl.BlockSpec((tm, QI_W), row),
            pl.BlockSpec((tm, IDX_DIM), row),
            pl.BlockSpec((tm, SSD_D_INNER), row),
            pl.BlockSpec((tm, CONV_DIM), row),
            pl.BlockSpec((tm, LANES), row),
        ],
        out_shape=[
            jax.ShapeDtypeStruct((n, ATTN_WIDTH), BF16),
            jax.ShapeDtypeStruct((n, KV_W), BF16),
            jax.ShapeDtypeStruct((n, QI_W), BF16),
            jax.ShapeDtypeStruct((n, IDX_DIM), BF16),
            jax.ShapeDtypeStruct((n, SSD_D_INNER), F32),
            jax.ShapeDtypeStruct((n, CONV_DIM), F32),
            jax.ShapeDtypeStruct((n, LANES), F32),
        ],
        scratch_shapes=[pltpu.VMEM((D_MODEL, W_CAT), BF16)],
        compiler_params=pltpu.CompilerParams(
            dimension_semantics=("arbitrary",), vmem_limit_bytes=VMEM_LIMIT),
        name="in_proj",
    )(x2, g, w_in, lnw, lnb)


def _bias_kernel(tbl_ref, out_ref):
    sk = lax.broadcasted_iota(I32, (TILE, TILE), 0)
    tq = lax.broadcasted_iota(I32, (TILE, TILE), 1)
    max_exact = NUM_BUCKETS // 2
    for off in range(2):
        dist = jnp.maximum(off * TILE + tq - sk, 0)
        df = jnp.maximum(dist, 1).astype(F32)
        large = max_exact + (jnp.log(df / max_exact) / math.log(MAX_DISTANCE / max_exact)
                             * (NUM_BUCKETS - max_exact)).astype(I32)
        large = jnp.minimum(large, NUM_BUCKETS - 1)
        bucket = jnp.where(dist < max_exact, dist, large)
        for h in range(N_ATTN_HEADS):
            acc = jnp.zeros((TILE, TILE), F32)
            for b in range(NUM_BUCKETS):
                acc = jnp.where(bucket == b, tbl_ref[b, h], acc)
            out_ref[h, off] = (acc - tbl_ref[NUM_BUCKETS - 1, h]) * LOG2E
    for h in range(N_ATTN_HEADS):
        out_ref[h, 2] = jnp.zeros((TILE, TILE), F32)


def _bias_tiles(rel_bias):
    return pl.pallas_call(
        _bias_kernel,
        in_specs=[pl.BlockSpec(memory_space=pltpu.SMEM)],
        out_specs=pl.BlockSpec(memory_space=pltpu.VMEM),
        out_shape=jax.ShapeDtypeStruct((N_ATTN_HEADS, 3, TILE, TILE), F32),
        name="bias_tiles",
    )(rel_bias)


def _key_to_f32(u):
    ks = u ^ I32(-2 ** 31)
    bits = jnp.where(ks >= 0, ks, ks ^ I32(0x7FFFFFFF))
    return lax.bitcast_convert_type(bits, F32)


PAIR = 2 * TILE
QUAD = 4 * TILE
QT = 2 * TILE
HALF_BITS = 16
HALF_MASK = 2 ** HALF_BITS - 1
HALF_BIAS = 2 ** (HALF_BITS - 1)
LO_BITS_FIRST = 10
I16 = jnp.int16
PV_ROWS = HEAD_DIM + 16


def _attn_kernel(q_ref, kv_ref, qi_ref, ki_ref, dtw_ref, qi_next_ref, dtw_next_ref, bias_ref, o_ref,
                 sc_ref, hi_ref, lo_ref, rk_ref, vt_ref, lg_ref, thr_ref, cnt_ref, *, seq_len, n_sel):
    j = pl.program_id(1)
    i_hi = j * (QT // TILE) + QT // TILE - 1
    npair = i_hi // 2 + 1
    slot = lax.rem(j, 2)

    def quad_keys(v):
        return [QUAD] * (v // 2) + [PAIR] * (v % 2)

    rep = N_ATTN_HEADS // N_KV_HEADS
    gl = rep * QT
    nt_dims = (((1,), (1,)), ((), ()))
    v_lo = N_KV_HEADS * HEAD_DIM

    @pl.when(j == 0)
    def _():
        vt = kv_ref[0, :, v_lo:].astype(F32).T
        ones_row = jnp.where(lax.broadcasted_iota(I32, (PV_ROWS - HEAD_DIM, seq_len), 0) == 0, 1.0, 0.0)
        for g in range(N_KV_HEADS):
            vt_ref[g * PV_ROWS:g * PV_ROWS + HEAD_DIM, :] = vt[g * HEAD_DIM:(g + 1) * HEAD_DIM].astype(BF16)
            vt_ref[g * PV_ROWS + HEAD_DIM:(g + 1) * PV_ROWS, :] = ones_row.astype(BF16)

    idx_scale = (N_IDX_HEADS ** -0.5) * (IDX_DIM ** -0.5)
    s_loc = lax.broadcasted_iota(I32, (PAIR, QT), 0)
    s_minus_t = s_loc - lax.broadcasted_iota(I32, (PAIR, QT), 1)

    def score_quads(v, to_slot, step, qi_src, dtw_src):
        w_t = dtw_src[0].T
        w_rows = [w_t[DTW_WI + h:DTW_WI + h + 1, :] * idx_scale for h in range(N_IDX_HEADS)]
        qi = qi_src[0]
        qi_all = jnp.concatenate(
            [qi[:, h * IDX_DIM:(h + 1) * IDX_DIM] for h in range(N_IDX_HEADS)], axis=0)
        def piece(c, nk):
            kt = ki_ref[0, c * QUAD:c * QUAD + nk, :]
            d = lax.dot_general(kt, qi_all, nt_dims, preferred_element_type=F32)
            s = jnp.zeros((nk, QT), F32)
            for h in range(N_IDX_HEADS):
                s = s + w_rows[h] * jnp.maximum(d[:, h * QT:(h + 1) * QT], 0.0)
            for u in range(nk // PAIR):
                pr = c * (QUAD // PAIR) + u
                su = jnp.where(s_minus_t <= step * QT - pr * PAIR, s[u * PAIR:(u + 1) * PAIR], NEG_INF)
                sc_ref[to_slot, pr] = su
                bits = lax.bitcast_convert_type(su, I32)
                key = bits ^ (lax.shift_right_arithmetic(bits, I32(31)) & I32(0x7FFFFFFF))
                hi_ref[to_slot, pr] = lax.shift_right_arithmetic(key, I32(HALF_BITS)).astype(I16)
                lo_ref[to_slot, pr] = (key ^ I32(HALF_BIAS)).astype(I16)

        return [functools.partial(piece, c, nk) for c, nk in enumerate(quad_keys(v))]

    @pl.when(j == 0)
    def _():
        for piece in score_quads(1, 0, 0, qi_ref, dtw_ref):
            piece()

    t_glob = j * QT + lax.broadcasted_iota(I32, (1, QT), 1)
    k_eff = jnp.minimum(n_sel, t_glob + 1)
    acc_rows = 4 * 2 * SUBLANES
    one16, zero16 = jnp.ones((), I16), jnp.zeros((), I16)

    def fold(cnt, rows=acc_rows):
        parts = [cnt[k * rows:(k + 1) * rows] for k in range(cnt.shape[0] // rows)]
        while len(parts) > 1:
            parts = [a + b for a, b in zip(parts[::2], parts[1::2])]
        return parts[0]

    def total(acc):
        packed = fold(acc, 2 * SUBLANES).astype(I32)
        return (packed[:SUBLANES] + packed[SUBLANES:]).sum(axis=0, keepdims=True)

    def to16(u):
        return (u - I32(HALF_BIAS)).astype(I16)

    def search(v):
        def count_ge(plane_ref, cand16):
            acc = jnp.zeros((acc_rows, QT), I16)
            for pr in range(v):
                acc = acc + fold(jnp.where(plane_ref[slot, pr] >= cand16, one16, zero16))
            return total(acc)

        def search_step(b, carry, plane_ref, base):
            prefix, cnt_ge, settled = carry
            cand = prefix | lax.shift_left(I32(1), HALF_BITS - 1 - b)
            cnt = count_ge(plane_ref, to16(cand)) + base
            ok = cnt >= k_eff
            settled = jnp.where(ok, jnp.where(cnt == k_eff, 1, settled), settled)
            return jnp.where(ok, cand, prefix), jnp.where(ok, cnt, cnt_ge), settled

        carry = (jnp.zeros((1, QT), I32), t_glob + 1, jnp.zeros((1, QT), I32))
        hi_u, cnt_ge, settled = lax.fori_loop(
            0, HALF_BITS, functools.partial(search_step, plane_ref=hi_ref, base=0), carry)
        hi16 = to16(hi_u)

        acc = jnp.zeros((acc_rows, QT), I16)
        for pr in range(v):
            h = hi_ref[slot, pr]
            lo_ref[slot, pr] = jnp.where(h == hi16, lo_ref[slot, pr], I16(-HALF_BIAS))
            acc = acc + fold(jnp.where(h > hi16, one16, zero16))
        above = total(acc)
        lo_step = functools.partial(search_step, plane_ref=lo_ref, base=above)
        floor_tie = count_ge(lo_ref, to16(jnp.ones((1, QT), I32))) + above < k_eff
        carry = (jnp.zeros((1, QT), I32), cnt_ge, jnp.where(floor_tie, 1, settled))
        carry = lax.fori_loop(0, LO_BITS_FIRST, lo_step, carry)
        lo_u, cnt_ge, _ = lax.cond(
            jnp.min(carry[2]) > 0, lambda c: c,
            lambda c: lax.fori_loop(LO_BITS_FIRST, HALF_BITS, lo_step, c), carry)
        thr_ref[...] = _key_to_f32(lax.shift_left(hi_u, I32(HALF_BITS)) | lo_u)
        cnt_ref[...] = cnt_ge

    for v in range(1, seq_len // PAIR + 1):
        pl.when(npair == v)(functools.partial(search, v))

    q = q_ref[0]
    qgs = [jnp.concatenate(
        [q[:, (g * rep + r) * HEAD_DIM:(g * rep + r + 1) * HEAD_DIM] for r in range(rep)],
        axis=0) for g in range(N_KV_HEADS)]
    below_diag = (lax.broadcasted_iota(I32, (PAIR, PAIR), 0)
                  > lax.broadcasted_iota(I32, (PAIR, PAIR), 1)).astype(BF16)
    tiles_per_quad = QUAD // TILE
    pairs_per_quad = QUAD // PAIR
    n_qt = QT // TILE
    int_min = I32(-2 ** 31)

    def bias_rows(c, g, u):
        cols = []
        for r in range(rep):
            for w in range(n_qt):
                off = jnp.clip(j * n_qt + w - (c * tiles_per_quad + u), 0, 2)
                cols.append(bias_ref[g * rep + r, off])
        return jnp.concatenate(cols, axis=1)

    def attend(v):
        thr = thr_ref[...]
        cnt_ge = cnt_ref[...]

        def tied(s):
            eq = jnp.where(s == thr, 1.0, 0.0)
            neg0 = eq * jnp.where(lax.bitcast_convert_type(s, I32) == int_min, 1.0, 0.0)
            return eq - neg0, neg0

        tots = []
        for pr in range(v):
            e2 = jnp.concatenate(tied(sc_ref[slot, pr]), axis=1).astype(BF16)
            r = jnp.dot(below_diag, e2, preferred_element_type=F32)
            rk_ref[pr] = r
            tots.append(r[PAIR - 1:PAIR] + e2[PAIR - 1:PAIR].astype(F32))
        tot = tots[0]
        for t in tots[1:]:
            tot = tot + t
        need = (k_eff - cnt_ge).astype(F32) + tot[:, :QT] + tot[:, QT:]
        off = jnp.concatenate([jnp.zeros((1, QT), F32), tot[:, :QT]], axis=1)

        m8 = [jnp.full((SUBLANES, gl), NEG_INF, F32) for _ in range(N_KV_HEADS)]
        nks = quad_keys(v)
        for c, nk in enumerate(nks):
            masks = []
            for u in range(nk // PAIR):
                pr = c * pairs_per_quad + u
                s = sc_ref[slot, pr]
                r = rk_ref[pr] + off
                off = off + tots[pr]
                neg0 = lax.bitcast_convert_type(s, I32) == int_min
                keep_tie = jnp.where(jnp.where(neg0, r[:, QT:], r[:, :QT]) < need, 0.0, NEG_INF)
                masks.append(jnp.where(s > thr, 0.0, jnp.where(s == thr, keep_tie, NEG_INF)))
            mask = jnp.concatenate(masks, axis=0)
            mask = jnp.concatenate([mask] * rep, axis=1)
            keys = slice(c * QUAD, c * QUAD + nk)
            for g in range(N_KV_HEADS):
                kt = kv_ref[0, keys, g * HEAD_DIM:(g + 1) * HEAD_DIM]
                lg = lax.dot_general(kt, qgs[g], nt_dims, preferred_element_type=F32) + mask
                if c == len(nks) - 1:
                    lg = lg + jnp.concatenate([bias_rows(c, g, u) for u in range(nk // TILE)], axis=0)
                elif c == len(nks) - 2 and v % 2 == 1:
                    u = tiles_per_quad - 1
                    lg = jnp.concatenate([lg[:u * TILE], lg[u * TILE:] + bias_rows(c, g, u)], axis=0)
                lg_ref[c, g, :nk] = lg
                m8[g] = jnp.maximum(m8[g], lg.reshape(nk // SUBLANES, SUBLANES, gl).max(axis=0))

        pieces = (score_quads(v + 1, 1 - slot, j + 1, qi_next_ref, dtw_next_ref)
                  if v < seq_len // PAIR else [])

        outs = []
        for g in range(N_KV_HEADS):
            m_row = jnp.max(m8[g], axis=0, keepdims=True)
            acc = jnp.zeros((PV_ROWS, gl), F32)
            for c, nk in enumerate(nks):
                p = jnp.exp2((lg_ref[c, g, :nk] - m_row).astype(BF16))
                vt = vt_ref[g * PV_ROWS:(g + 1) * PV_ROWS, c * QUAD:c * QUAD + nk]
                acc = acc + jnp.dot(vt, p, preferred_element_type=F32)
                if pieces:
                    pieces.pop(0)()
            if g == N_KV_HEADS - 1:
                for piece in pieces:
                    piece()
            o_g = acc[:HEAD_DIM] / acc[HEAD_DIM:HEAD_DIM + 1]
            outs += [o_g[:, r * QT:(r + 1) * QT] for r in range(rep)]
        o_ref[0] = jnp.concatenate(outs, axis=0).T.astype(BF16)

    for v in range(1, seq_len // PAIR + 1):
        pl.when(npair == v)(functools.partial(attend, v))


def _attention(q3, kv3, qi3, ki3, dtw3, bias, n_sel):
    bsz, seq_len, _ = q3.shape
    assert seq_len % QUAD == 0 and QUAD % QT == 0
    rep = N_ATTN_HEADS // N_KV_HEADS
    kern = functools.partial(_attn_kernel, seq_len=seq_len, n_sel=n_sel)
    steps = seq_len // QT
    this_step = lambda b, j: (b, j, 0)
    next_step = lambda b, j: (b, jnp.minimum(j + 1, steps - 1), 0)
    return pl.pallas_call(
        kern,
        grid=(bsz, steps),
        in_specs=[
            pl.BlockSpec((1, QT, ATTN_WIDTH), this_step),
            pl.BlockSpec((1, seq_len, KV_W), lambda b, j: (b, 0, 0)),
            pl.BlockSpec((1, QT, QI_W), this_step),
            pl.BlockSpec((1, seq_len, IDX_DIM), lambda b, j: (b, 0, 0)),
            pl.BlockSpec((1, QT, LANES), this_step),
            pl.BlockSpec((1, QT, QI_W), next_step),
            pl.BlockSpec((1, QT, LANES), next_step),
            pl.BlockSpec((N_ATTN_HEADS, 3, TILE, TILE), lambda b, j: (0, 0, 0, 0)),
        ],
        out_specs=pl.BlockSpec((1, QT, ATTN_WIDTH), this_step),
        out_shape=jax.ShapeDtypeStruct((bsz, seq_len, ATTN_WIDTH), BF16),
        scratch_shapes=[
            pltpu.VMEM((2, seq_len // PAIR, PAIR, QT), F32),
            pltpu.VMEM((2, seq_len // PAIR, PAIR, QT), I16),
            pltpu.VMEM((2, seq_len // PAIR, PAIR, QT), I16),
            pltpu.VMEM((seq_len // PAIR, PAIR, 2 * QT), F32),
            pltpu.VMEM((N_KV_HEADS * PV_ROWS, seq_len), BF16),
            pltpu.VMEM((seq_len // QUAD, N_KV_HEADS, QUAD, rep * QT), F32),
            pltpu.VMEM((1, QT), F32),
            pltpu.VMEM((1, QT), I32),
        ],
        compiler_params=pltpu.CompilerParams(
            dimension_semantics=("arbitrary", "arbitrary"), vmem_limit_bytes=VMEM_LIMIT),
        name="sparse_attn",
    )(q3, kv3, qi3, ki3, dtw3, qi3, dtw3, bias)


SSD_STEP_CHUNKS = 2
SSD_STEP = SSD_STEP_CHUNKS * TILE


def _split_bf16(v, n):
    parts = []
    for _ in range(n):
        p = v.astype(BF16)
        parts.append(p)
        v = v - p.astype(F32)
    return parts


def _dot_01(v, mat01, n):
    out = None
    for p in _split_bf16(v, n):
        t = jnp.dot(p, mat01, preferred_element_type=F32)
        out = t if out is None else out + t
    return out


def _ssd_kernel(z_ref, xbc_ref, dtw_ref, cw_ref, cb_ref, dtb_ref, alog_ref, dsk_ref, nw_ref,
                o_ref, xpad_ref, st_ref):
    step = pl.program_id(1)
    gn = SSD_N_GROUPS * SSD_D_STATE
    hpg = SSD_N_HEADS // SSD_N_GROUPS
    gw = hpg * SSD_HEAD_DIM

    @pl.when(step == 0)
    def _():
        xpad_ref[0:SUBLANES, :] = jnp.zeros((SUBLANES, CONV_DIM), F32)
        st_ref[...] = jnp.zeros_like(st_ref)

    xpad_ref[SUBLANES:, :] = xbc_ref[0]
    xfull = xpad_ref[...]
    conv = cb_ref[...] + cw_ref[CONV_WIDTH - 1:CONV_WIDTH, :] * xfull[SUBLANES:]
    for k in range(CONV_WIDTH - 1):
        shifted = pltpu.roll(xfull, CONV_WIDTH - 1 - k, 0)
        conv = conv + cw_ref[k:k + 1, :] * shifted[SUBLANES:]
    xpad_ref[0:SUBLANES, :] = xfull[SSD_STEP:]
    act = conv * jax.nn.sigmoid(conv)
    xs = act[:, :SSD_D_INNER]
    bm = act[:, SSD_D_INNER:SSD_D_INNER + gn].astype(BF16)
    cm = act[:, SSD_D_INNER + gn:].astype(BF16)

    dt_in = dtw_ref[0] + dtb_ref[...]
    dt = jnp.maximum(dt_in, 0.0) + jnp.log1p(jnp.exp(-jnp.abs(dt_in)))
    adt = dt * (-jnp.exp(alog_ref[...]))
    row = lax.broadcasted_iota(I32, (SSD_STEP, SSD_STEP), 0)
    col = lax.broadcasted_iota(I32, (SSD_STEP, SSD_STEP), 1)
    same_chunk = (row // TILE) == (col // TILE)
    chunk_tril = jnp.where(same_chunk & (row >= col), 1.0, 0.0).astype(BF16)
    acs = None
    for p in _split_bf16(adt, 3):
        t = jnp.dot(chunk_tril, p, preferred_element_type=F32)
        acs = t if acs is None else acs + t
    acs_t = acs.T
    causal = (lax.broadcasted_iota(I32, (TILE, TILE), 0) >= lax.broadcasted_iota(I32, (TILE, TILE), 1))
    a_last = [acs[(c + 1) * TILE - 1:(c + 1) * TILE, :] for c in range(SSD_STEP_CHUNKS)]
    out_decay = jnp.concatenate(
        [jnp.exp(a_last[c] - acs[c * TILE:(c + 1) * TILE]) for c in range(SSD_STEP_CHUNKS)], axis=0)
    chunk_decay = jnp.concatenate(
        [jnp.exp(a) for a in a_last] + [jnp.zeros((SUBLANES - SSD_STEP_CHUNKS, LANES), F32)], axis=0)
    hsel = (lax.broadcasted_iota(I32, (LANES, SSD_D_INNER), 0)
            == lax.broadcasted_iota(I32, (LANES, SSD_D_INNER), 1) // SSD_HEAD_DIM).astype(BF16)
    expanded = _dot_01(jnp.concatenate([dt, jnp.exp(acs), out_decay, chunk_decay], axis=0), hsel, 2)
    dt_x = expanded[:SSD_STEP]
    in_decay_x = expanded[SSD_STEP:2 * SSD_STEP]
    out_decay_x = expanded[2 * SSD_STEP:3 * SSD_STEP]
    chunk_decay_x = expanded[3 * SSD_STEP:]

    x_dt = xs * dt_x
    x_dt_b = x_dt.astype(BF16)
    x_out_b = (x_dt * out_decay_x).astype(BF16)
    tn_dims = (((0,), (0,)), ((), ()))
    nt_dims = (((1,), (1,)), ((), ()))
    states = [st_ref[g] for g in range(SSD_N_GROUPS)]
    y_chunks = []
    for c in range(SSD_STEP_CHUNKS):
        rows = slice(c * TILE, (c + 1) * TILE)
        y_parts = []
        for g in range(SSD_N_GROUPS):
            bg = bm[rows, g * SSD_D_STATE:(g + 1) * SSD_D_STATE]
            cg = cm[rows, g * SSD_D_STATE:(g + 1) * SSD_D_STATE]
            lanes = slice(g * gw, (g + 1) * gw)
            cb = lax.dot_general(cg, bg, nt_dims, preferred_element_type=F32)
            y_diag = []
            for r in range(hpg):
                h = g * hpg + r
                seg = jnp.where(causal, acs[rows, h:h + 1] - acs_t[h:h + 1, rows], NEG_INF)
                w = (cb * jnp.exp(seg)).astype(BF16)
                y_diag.append(jnp.dot(w, x_dt_b[rows, h * SSD_HEAD_DIM:(h + 1) * SSD_HEAD_DIM],
                                      preferred_element_type=F32))
            y_off = (jnp.dot(cg, states[g].astype(BF16), preferred_element_type=F32)
                     * in_decay_x[rows, lanes])
            y_parts.append(jnp.concatenate(y_diag, axis=-1) + y_off)
            st_new = lax.dot_general(bg, x_out_b[rows, lanes], tn_dims, preferred_element_type=F32)
            states[g] = states[g] * chunk_decay_x[c:c + 1, lanes] + st_new
        y_chunks.append(jnp.concatenate(y_parts, axis=-1))
    for g in range(SSD_N_GROUPS):
        st_ref[g] = states[g]

    y = jnp.concatenate(y_chunks, axis=0) + xs * dsk_ref[...]
    zv = z_ref[0]
    y = y * (zv * jax.nn.sigmoid(zv))
    ng = SSD_D_INNER // SSD_N_GROUPS
    outs = []
    for g in range(SSD_N_GROUPS):
        yg = y[:, g * ng:(g + 1) * ng]
        outs.append(yg * lax.rsqrt(jnp.mean(yg * yg, axis=-1, keepdims=True) + EPS))
    o_ref[0] = (jnp.concatenate(outs, axis=-1) * nw_ref[...]).astype(BF16)


def _ssd(z3, xbc3, dtw3, conv_w, conv_b, dt_bias, a_log, d_skip, norm_w):
    bsz, seq_len, _ = z3.shape
    assert seq_len % SSD_STEP == 0
    nc = seq_len // SSD_STEP
    blk = lambda w: pl.BlockSpec((1, SSD_STEP, w), lambda b, c: (b, c, 0))
    par = lambda r, w: pl.BlockSpec((r, w), lambda b, c: (0, 0))
    return pl.pallas_call(
        _ssd_kernel,
        grid=(bsz, nc),
        in_specs=[blk(SSD_D_INNER), blk(CONV_DIM), blk(LANES),
                  par(CONV_WIDTH, CONV_DIM), par(1, CONV_DIM), par(1, LANES), par(1, LANES),
                  par(1, SSD_D_INNER), par(1, SSD_D_INNER)],
        out_specs=blk(SSD_D_INNER),
        out_shape=jax.ShapeDtypeStruct((bsz, seq_len, SSD_D_INNER), BF16),
        scratch_shapes=[
            pltpu.VMEM((SUBLANES + SSD_STEP, CONV_DIM), F32),
            pltpu.VMEM((SSD_N_GROUPS, SSD_D_STATE, SSD_D_INNER // SSD_N_GROUPS), F32),
        ],
        compiler_params=pltpu.CompilerParams(
            dimension_semantics=("arbitrary", "arbitrary"), vmem_limit_bytes=VMEM_LIMIT),
        name="ssd_mixer",
    )(z3, xbc3, dtw3, conv_w, conv_b, dt_bias, a_log, d_skip, norm_w)


FF_CHUNK = 1024


def _mlp_kernel(x_ref, attn_ref, ssd_ref, wo_ref, g1_ref, g2_ref, wu_ref, wd_ref, g3_ref, o_ref):
    mix = (jnp.dot(attn_ref[...], wo_ref[:ATTN_WIDTH, :], preferred_element_type=F32)
           + jnp.dot(ssd_ref[...], wo_ref[ATTN_WIDTH:, :], preferred_element_type=F32))
    h1 = x_ref[...] + _rms(mix, g1_ref[...])
    u = _rms(h1, g2_ref[...]).astype(BF16)
    acc = jnp.zeros(h1.shape, F32)
    for c in range(0, D_FF, FF_CHUNK):
        f = jnp.dot(u, wu_ref[:, c:c + FF_CHUNK], preferred_element_type=F32)
        f = jnp.square(jnp.maximum(f, 0.0)).astype(BF16)
        acc = acc + jnp.dot(f, wd_ref[c:c + FF_CHUNK, :], preferred_element_type=F32)
    o_ref[...] = h1 + _rms(acc, g3_ref[...])


def _mlp(x2, attn2, ssd2, wo, g1, g2, wu, wd, g3, tm):
    n = x2.shape[0]
    row = lambda i: (i, 0)
    const = lambda i: (0, 0)
    single = dict(pipeline_mode=pl.Buffered(1))
    return pl.pallas_call(
        _mlp_kernel,
        grid=(n // tm,),
        in_specs=[
            pl.BlockSpec((tm, D_MODEL), row),
            pl.BlockSpec((tm, ATTN_WIDTH), row),
            pl.BlockSpec((tm, SSD_D_INNER), row),
            pl.BlockSpec((ATTN_WIDTH + SSD_D_INNER, D_MODEL), const, **single),
            pl.BlockSpec((1, D_MODEL), const),
            pl.BlockSpec((1, D_MODEL), const),
            pl.BlockSpec((D_MODEL, D_FF), const, **single),
            pl.BlockSpec((D_FF, D_MODEL), const, **single),
            pl.BlockSpec((1, D_MODEL), const),
        ],
        out_specs=pl.BlockSpec((tm, D_MODEL), row),
        out_shape=jax.ShapeDtypeStruct((n, D_MODEL), F32),
        compiler_params=pltpu.CompilerParams(
            dimension_semantics=("arbitrary",), vmem_limit_bytes=VMEM_LIMIT),
        name="out_proj_mlp",
    )(x2, attn2, ssd2, wo, g1, g2, wu, wd, g3)


def _pad_lanes(v, n):
    return jnp.pad(v, (0, n - v.shape[0])).reshape(1, n)


def kernel(x, norm_pre_mix, norm_post_mix, norm_pre_mlp, norm_post_mlp, w_in, k_idx_ln_w, k_idx_ln_b, conv_w, conv_b, dt_bias, a_log, d_skip, ssd_norm_w, w_out, w_mlp_up, w_mlp_down, rel_bias):
    bsz, seq_len, d = x.shape
    n = bsz * seq_len
    assert d == D_MODEL and seq_len % TILE == 0
    tm = next(t for t in (1024, 512, TILE) if n % t == 0)
    n_sel = min(TOPK_MAX, seq_len // 4)
    bias = _bias_tiles(rel_bias)
    h = x.reshape(n, d)
    for i in range(norm_pre_mix.shape[0]):
        row = lambda v: v[i].reshape(1, -1)
        q, kv, qi, ki, z, xbc, dtw = _in_proj(h, row(norm_pre_mix), w_in[i],
                                              row(k_idx_ln_w), row(k_idx_ln_b), tm)
        r3 = lambda a: a.reshape(bsz, seq_len, a.shape[-1])
        dtw3 = r3(dtw)
        attn = _attention(r3(q), r3(kv), r3(qi), r3(ki), dtw3, bias, n_sel)
        ssd = _ssd(r3(z), r3(xbc), dtw3, conv_w[i], row(conv_b),
                   _pad_lanes(dt_bias[i], LANES), _pad_lanes(a_log[i], LANES),
                   jnp.repeat(d_skip[i], SSD_HEAD_DIM).reshape(1, -1), row(ssd_norm_w))
        h = _mlp(h, attn.reshape(n, -1), ssd.reshape(n, -1), w_out[i].astype(BF16),
                 row(norm_post_mix), row(norm_pre_mlp), w_mlp_up[i].astype(BF16),
                 w_mlp_down[i].astype(BF16), row(norm_post_mlp), tm)
    return h.reshape(bsz, seq_len, d)
```

```python
import functools
import math

import jax
import jax.numpy as jnp
from jax import lax
from jax.experimental import pallas as pl
from jax.experimental.pallas import tpu as pltpu

F32 = jnp.float32
BF16 = jnp.bfloat16
I32 = jnp.int32

D_MODEL = 1024
N_ATTN_HEADS = 8
N_KV_HEADS = 2
HEAD_DIM = 64
ATTN_WIDTH = N_ATTN_HEADS * HEAD_DIM
N_IDX_HEADS = 4
IDX_DIM = 64
TOPK_MAX = 256
NUM_BUCKETS = 32
MAX_DISTANCE = 128
SSD_D_INNER = 512
SSD_HEAD_DIM = 64
SSD_N_HEADS = 8
SSD_N_GROUPS = 2
SSD_D_STATE = 128
CONV_WIDTH = 4
CONV_DIM = SSD_D_INNER + 2 * SSD_N_GROUPS * SSD_D_STATE
D_FF = 4 * D_MODEL
EPS = 1e-6

LANES = 128
SUBLANES = 8
TILE = 128
VMEM_LIMIT = 56 * 1024 * 1024

QKV_W = ATTN_WIDTH + 2 * N_KV_HEADS * HEAD_DIM + N_IDX_HEADS * IDX_DIM
KV_W = 2 * N_KV_HEADS * HEAD_DIM
QI_W = N_IDX_HEADS * IDX_DIM
COL_QKV = 0
COL_KV = COL_QKV + ATTN_WIDTH
COL_QI = COL_KV + KV_W
COL_KI = COL_QKV + QKV_W
COL_Z = COL_KI + LANES
COL_XBC = COL_Z + SSD_D_INNER
COL_DTW = COL_XBC + CONV_DIM
W_CAT = COL_DTW + LANES
DTW_WI = SSD_N_HEADS
SRC_QKV = 0
SRC_KI = SRC_QKV + QKV_W
SRC_WI = SRC_KI + IDX_DIM
SRC_Z = SRC_WI + N_IDX_HEADS
SRC_XBC = SRC_Z + SSD_D_INNER
SRC_DT = SRC_XBC + CONV_DIM

NEG_INF = float("-inf")
LOG2E = 1.4426950408889634
Q_SCALE = LOG2E * HEAD_DIM ** -0.5


def _rms(x, g):
    return x * lax.rsqrt(jnp.mean(x * x, axis=-1, keepdims=True) + EPS) * g


W_PACK_ROWS = 128


def _in_proj_kernel(x_ref, g_ref, win_ref, lnw_ref, lnb_ref,
                    q_ref, kv_ref, qi_ref, ki_ref, z_ref, xbc_ref, dtw_ref, w_ref):
    @pl.when(pl.program_id(0) == 0)
    def _():
        def pack_rows(r, carry):
            rows = pl.ds(pl.multiple_of(r * W_PACK_ROWS, W_PACK_ROWS), W_PACK_ROWS)
            src = lambda lo, width: win_ref[rows, lo:lo + width]
            zeros = lambda width: jnp.zeros((W_PACK_ROWS, width), F32)
            w_ref[rows, COL_QKV:COL_KI] = src(SRC_QKV, QKV_W).astype(BF16)
            w_ref[rows, COL_KI:COL_Z] = jnp.concatenate(
                [src(SRC_KI, IDX_DIM), zeros(LANES - IDX_DIM)], axis=1).astype(BF16)
            w_ref[rows, COL_Z:COL_XBC] = src(SRC_Z, SSD_D_INNER).astype(BF16)
            w_ref[rows, COL_XBC:COL_DTW] = src(SRC_XBC, CONV_DIM).astype(BF16)
            w_ref[rows, COL_DTW:W_CAT] = jnp.concatenate(
                [src(SRC_DT, SSD_N_HEADS), src(SRC_WI, N_IDX_HEADS),
                 zeros(LANES - SSD_N_HEADS - N_IDX_HEADS)], axis=1).astype(BF16)
            return carry

        lax.fori_loop(0, D_MODEL // W_PACK_ROWS, pack_rows, 0)

    u = _rms(x_ref[...], g_ref[...]).astype(BF16)

    def mm(lo, hi):
        return jnp.dot(u, w_ref[:, lo:hi], preferred_element_type=F32)

    q_ref[...] = (mm(COL_QKV, COL_KV) * Q_SCALE).astype(BF16)
    kv_ref[...] = mm(COL_KV, COL_QI).astype(BF16)
    qi_ref[...] = mm(COL_QI, COL_KI).astype(BF16)
    ki = mm(COL_KI, COL_Z)[:, :IDX_DIM]
    mu = jnp.mean(ki, axis=-1, keepdims=True)
    var = jnp.mean(jnp.square(ki - mu), axis=-1, keepdims=True)
    ki_ref[...] = ((ki - mu) * lax.rsqrt(var + EPS) * lnw_ref[...] + lnb_ref[...]).astype(BF16)
    z_ref[...] = mm(COL_Z, COL_XBC)
    xbc_ref[...] = mm(COL_XBC, COL_DTW)
    dtw_ref[...] = mm(COL_DTW, W_CAT)


def _in_proj(x2, g, w_in, lnw, lnb, tm):
    n = x2.shape[0]
    assert w_in.shape == (D_MODEL, SRC_DT + SSD_N_HEADS)
    row = lambda i: (i, 0)
    const = lambda i: (0, 0)
    return pl.pallas_call(
        _in_proj_kernel,
        grid=(n // tm,),
        in_specs=[
            pl.BlockSpec((tm, D_MODEL), row),
            pl.BlockSpec((1, D_MODEL), const),
            pl.BlockSpec(w_in.shape, const, pipeline_mode=pl.Buffered(1)),
            pl.BlockSpec((1, IDX_DIM), const),
            pl.BlockSpec((1, IDX_DIM), const),
        ],
        out_specs=[
            pl.BlockSpec((tm, ATTN_WIDTH), row),
            pl.BlockSpec((tm, KV_W), row),
            pl.BlockSpec((tm, QI_W), row),
            pl.BlockSpec((tm, IDX_DIM), row),
            pl.BlockSpec((tm, SSD_D_INNER), row),
            pl.BlockSpec((tm, CONV_DIM), row),
            pl.BlockSpec((tm, LANES), row),
        ],
        out_shape=[
            jax.ShapeDtypeStruct((n, ATTN_WIDTH), BF16),
            jax.ShapeDtypeStruct((n, KV_W), BF16),
            jax.ShapeDtypeStruct((n, QI_W), BF16),
            jax.ShapeDtypeStruct((n, IDX_DIM), BF16),
            jax.ShapeDtypeStruct((n, SSD_D_INNER), F32),
            jax.ShapeDtypeStruct((n, CONV_DIM), F32),
            jax.ShapeDtypeStruct((n, LANES), F32),
        ],
        scratch_shapes=[pltpu.VMEM((D_MODEL, W_CAT), BF16)],
        compiler_params=pltpu.CompilerParams(
            dimension_semantics=("arbitrary",), vmem_limit_bytes=VMEM_LIMIT),
        name="in_proj",
    )(x2, g, w_in, lnw, lnb)


def _bias_kernel(tbl_ref, out_ref):
    sk = lax.broadcasted_iota(I32, (TILE, TILE), 0)
    tq = lax.broadcasted_iota(I32, (TILE, TILE), 1)
    max_exact = NUM_BUCKETS // 2
    for off in range(2):
        dist = jnp.maximum(off * TILE + tq - sk, 0)
        df = jnp.maximum(dist, 1).astype(F32)
        large = max_exact + (jnp.log(df / max_exact) / math.log(MAX_DISTANCE / max_exact)
                             * (NUM_BUCKETS - max_exact)).astype(I32)
        large = jnp.minimum(large, NUM_BUCKETS - 1)
        bucket = jnp.where(dist < max_exact, dist, large)
        for h in range(N_ATTN_HEADS):
            acc = jnp.zeros((TILE, TILE), F32)
            for b in range(NUM_BUCKETS):
                acc = jnp.where(bucket == b, tbl_ref[b, h], acc)
            out_ref[h, off] = (acc - tbl_ref[NUM_BUCKETS - 1, h]) * LOG2E
    for h in range(N_ATTN_HEADS):
        out_ref[h, 2] = jnp.zeros((TILE, TILE), F32)


def _bias_tiles(rel_bias):
    return pl.pallas_call(
        _bias_kernel,
        in_specs=[pl.BlockSpec(memory_space=pltpu.SMEM)],
        out_specs=pl.BlockSpec(memory_space=pltpu.VMEM),
        out_shape=jax.ShapeDtypeStruct((N_ATTN_HEADS, 3, TILE, TILE), F32),
        name="bias_tiles",
    )(rel_bias)


def _key_to_f32(u):
    ks = u ^ I32(-2 ** 31)
    bits = jnp.where(ks >= 0, ks, ks ^ I32(0x7FFFFFFF))
    return lax.bitcast_convert_type(bits, F32)


PAIR = 2 * TILE
QUAD = 4 * TILE
QT = 2 * TILE
HALF_BITS = 16
HALF_MASK = 2 ** HALF_BITS - 1
HALF_BIAS = 2 ** (HALF_BITS - 1)
LO_BITS_FIRST = 10
I16 = jnp.int16
PV_ROWS = HEAD_DIM + 16


def _attn_kernel(q_ref, kv_ref, qi_ref, ki_ref, dtw_ref, qi_next_ref, dtw_next_ref, bias_ref, o_ref,
                 sc_ref, hi_ref, lo_ref, rk_ref, vt_ref, lg_ref, thr_ref, cnt_ref, *, seq_len, n_sel):
    j = pl.program_id(1)
    i_hi = j * (QT // TILE) + QT // TILE - 1
    npair = i_hi // 2 + 1
    n_total = seq_len // PAIR
    merged = n_total - 1 if n_total >= 4 else 0
    eff = lambda v: n_total if v == merged else v
    variants = [v for v in range(1, n_total + 1) if v != merged]
    nv = jnp.where(npair == merged, n_total, npair)
    slot = lax.rem(j, 2)

    def quad_keys(v):
        return [QUAD] * (v // 2) + [PAIR] * (v % 2)

    rep = N_ATTN_HEADS // N_KV_HEADS
    gl = rep * QT
    nt_dims = (((1,), (1,)), ((), ()))
    v_lo = N_KV_HEADS * HEAD_DIM

    @pl.when(j == 0)
    def _():
        vt = kv_ref[0, :, v_lo:].astype(F32).T
        ones_row = jnp.where(lax.broadcasted_iota(I32, (PV_ROWS - HEAD_DIM, seq_len), 0) == 0, 1.0, 0.0)
        for g in range(N_KV_HEADS):
            vt_ref[g * PV_ROWS:g * PV_ROWS + HEAD_DIM, :] = vt[g * HEAD_DIM:(g + 1) * HEAD_DIM].astype(BF16)
            vt_ref[g * PV_ROWS + HEAD_DIM:(g + 1) * PV_ROWS, :] = ones_row.astype(BF16)

    idx_scale = (N_IDX_HEADS ** -0.5) * (IDX_DIM ** -0.5)
    s_loc = lax.broadcasted_iota(I32, (PAIR, QT), 0)
    s_minus_t = s_loc - lax.broadcasted_iota(I32, (PAIR, QT), 1)

    def score_quads(v, to_slot, step, qi_src, dtw_src):
        w_t = dtw_src[0].T
        w_rows = [w_t[DTW_WI + h:DTW_WI + h + 1, :] * idx_scale for h in range(N_IDX_HEADS)]
        qi = qi_src[0]
        qi_all = jnp.concatenate(
            [qi[:, h * IDX_DIM:(h + 1) * IDX_DIM] for h in range(N_IDX_HEADS)], axis=0)
        def piece(c, nk):
            kt = ki_ref[0, c * QUAD:c * QUAD + nk, :]
            d = lax.dot_general(kt, qi_all, nt_dims, preferred_element_type=F32)
            s = jnp.zeros((nk, QT), F32)
            for h in range(N_IDX_HEADS):
                s = s + w_rows[h] * jnp.maximum(d[:, h * QT:(h + 1) * QT], 0.0)
            for u in range(nk // PAIR):
                pr = c * (QUAD // PAIR) + u
                su = jnp.where(s_minus_t <= step * QT - pr * PAIR, s[u * PAIR:(u + 1) * PAIR], NEG_INF)
                sc_ref[to_slot, pr] = su
                bits = lax.bitcast_convert_type(su, I32)
                key = bits ^ (lax.shift_right_arithmetic(bits, I32(31)) & I32(0x7FFFFFFF))
                hi_ref[to_slot, pr] = lax.shift_right_arithmetic(key, I32(HALF_BITS)).astype(I16)
                lo_ref[to_slot, pr] = (key ^ I32(HALF_BIAS)).astype(I16)

        return [functools.partial(piece, c, nk) for c, nk in enumerate(quad_keys(v))]

    @pl.when(j == 0)
    def _():
        for piece in score_quads(1, 0, 0, qi_ref, dtw_ref):
            piece()

    t_glob = j * QT + lax.broadcasted_iota(I32, (1, QT), 1)
    k_eff = jnp.minimum(n_sel, t_glob + 1)
    acc_rows = 4 * 2 * SUBLANES
    one16, zero16 = jnp.ones((), I16), jnp.zeros((), I16)

    def fold(cnt, rows=acc_rows):
        parts = [cnt[k * rows:(k + 1) * rows] for k in range(cnt.shape[0] // rows)]
        while len(parts) > 1:
            parts = [a + b for a, b in zip(parts[::2], parts[1::2])]
        return parts[0]

    def total(acc):
        packed = fold(acc, 2 * SUBLANES).astype(I32)
        return (packed[:SUBLANES] + packed[SUBLANES:]).sum(axis=0, keepdims=True)

    def to16(u):
        return (u - I32(HALF_BIAS)).astype(I16)

    def search(v):
        def count_ge(plane_ref, cand16):
            acc = jnp.zeros((acc_rows, QT), I16)
            for pr in range(v):
                acc = acc + fold(jnp.where(plane_ref[slot, pr] >= cand16, one16, zero16))
            return total(acc)

        def search_step(b, carry, plane_ref, base):
            prefix, cnt_ge, settled = carry
            cand = prefix | lax.shift_left(I32(1), HALF_BITS - 1 - b)
            cnt = count_ge(plane_ref, to16(cand)) + base
            ok = cnt >= k_eff
            settled = jnp.where(ok, jnp.where(cnt == k_eff, 1, settled), settled)
            return jnp.where(ok, cand, prefix), jnp.where(ok, cnt, cnt_ge), settled

        carry = (jnp.zeros((1, QT), I32), t_glob + 1, jnp.zeros((1, QT), I32))
        hi_u, cnt_ge, settled = lax.fori_loop(
            0, HALF_BITS, functools.partial(search_step, plane_ref=hi_ref, base=0), carry)
        hi16 = to16(hi_u)

        acc = jnp.zeros((acc_rows, QT), I16)
        for pr in range(v):
            h = hi_ref[slot, pr]
            lo_ref[slot, pr] = jnp.where(h == hi16, lo_ref[slot, pr], I16(-HALF_BIAS))
            acc = acc + fold(jnp.where(h > hi16, one16, zero16))
        above = total(acc)
        lo_step = functools.partial(search_step, plane_ref=lo_ref, base=above)
        floor_tie = count_ge(lo_ref, to16(jnp.ones((1, QT), I32))) + above < k_eff
        carry = (jnp.zeros((1, QT), I32), cnt_ge, jnp.where(floor_tie, 1, settled))
        carry = lax.fori_loop(0, LO_BITS_FIRST, lo_step, carry)
        lo_u, cnt_ge, _ = lax.cond(
            jnp.min(carry[2]) > 0, lambda c: c,
            lambda c: lax.fori_loop(LO_BITS_FIRST, HALF_BITS, lo_step, c), carry)
        thr_ref[...] = _key_to_f32(lax.shift_left(hi_u, I32(HALF_BITS)) | lo_u)
        cnt_ref[...] = cnt_ge

    for v in variants:
        pl.when(nv == v)(functools.partial(search, v))

    q = q_ref[0]
    qgs = [jnp.concatenate(
        [q[:, (g * rep + r) * HEAD_DIM:(g * rep + r + 1) * HEAD_DIM] for r in range(rep)],
        axis=0) for g in range(N_KV_HEADS)]
    below_diag = (lax.broadcasted_iota(I32, (PAIR, PAIR), 0)
                  > lax.broadcasted_iota(I32, (PAIR, PAIR), 1)).astype(BF16)
    tiles_per_quad = QUAD // TILE
    pairs_per_quad = QUAD // PAIR
    n_qt = QT // TILE
    int_min = I32(-2 ** 31)

    def bias_rows(c, g, u):
        cols = []
        for r in range(rep):
            for w in range(n_qt):
                off = jnp.clip(j * n_qt + w - (c * tiles_per_quad + u), 0, 2)
                cols.append(bias_ref[g * rep + r, off])
        return jnp.concatenate(cols, axis=1)

    def attend(v):
        thr = thr_ref[...]
        cnt_ge = cnt_ref[...]

        def tied(s):
            eq = jnp.where(s == thr, 1.0, 0.0)
            neg0 = eq * jnp.where(lax.bitcast_convert_type(s, I32) == int_min, 1.0, 0.0)
            return eq - neg0, neg0

        tots = []
        for pr in range(v):
            e2 = jnp.concatenate(tied(sc_ref[slot, pr]), axis=1).astype(BF16)
            r = jnp.dot(below_diag, e2, preferred_element_type=F32)
            rk_ref[pr] = r
            tots.append(r[PAIR - 1:PAIR] + e2[PAIR - 1:PAIR].astype(F32))
        tot = tots[0]
        for t in tots[1:]:
            tot = tot + t
        need = (k_eff - cnt_ge).astype(F32) + tot[:, :QT] + tot[:, QT:]
        off = jnp.concatenate([jnp.zeros((1, QT), F32), tot[:, :QT]], axis=1)

        m8 = [jnp.full((SUBLANES, gl), NEG_INF, F32) for _ in range(N_KV_HEADS)]
        nks = quad_keys(v)
        for c, nk in enumerate(nks):
            masks = []
            for u in range(nk // PAIR):
                pr = c * pairs_per_quad + u
                s = sc_ref[slot, pr]
                r = rk_ref[pr] + off
                off = off + tots[pr]
                neg0 = lax.bitcast_convert_type(s, I32) == int_min
                keep_tie = jnp.where(jnp.where(neg0, r[:, QT:], r[:, :QT]) < need, 0.0, NEG_INF)
                masks.append(jnp.where(s > thr, 0.0, jnp.where(s == thr, keep_tie, NEG_INF)))
            mask = jnp.concatenate(masks, axis=0)
            mask = jnp.concatenate([mask] * rep, axis=1)
            keys = slice(c * QUAD, c * QUAD + nk)
            for g in range(N_KV_HEADS):
                kt = kv_ref[0, keys, g * HEAD_DIM:(g + 1) * HEAD_DIM]
                lg = lax.dot_general(kt, qgs[g], nt_dims, preferred_element_type=F32) + mask
                if c == len(nks) - 1:
                    lg = lg + jnp.concatenate([bias_rows(c, g, u) for u in range(nk // TILE)], axis=0)
                elif c == len(nks) - 2 and (v % 2 == 1 or (v == n_total and merged)):
                    u = tiles_per_quad - 1
                    lg = jnp.concatenate([lg[:u * TILE], lg[u * TILE:] + bias_rows(c, g, u)], axis=0)
                lg_ref[c, g, :nk] = lg
                m8[g] = jnp.maximum(m8[g], lg.reshape(nk // SUBLANES, SUBLANES, gl).max(axis=0))

        pieces = (score_quads(eff(v + 1), 1 - slot, j + 1, qi_next_ref, dtw_next_ref)
                  if v < n_total else [])
        if v == n_total and merged:
            @pl.when(npair < n_total)
            def _():
                for piece in score_quads(n_total, 1 - slot, j + 1, qi_next_ref, dtw_next_ref):
                    piece()

        outs = []
        for g in range(N_KV_HEADS):
            m_row = jnp.max(m8[g], axis=0, keepdims=True)
            acc = jnp.zeros((PV_ROWS, gl), F32)
            for c, nk in enumerate(nks):
                p = jnp.exp2((lg_ref[c, g, :nk] - m_row).astype(BF16))
                vt = vt_ref[g * PV_ROWS:(g + 1) * PV_ROWS, c * QUAD:c * QUAD + nk]
                acc = acc + jnp.dot(vt, p, preferred_element_type=F32)
                if pieces:
                    pieces.pop(0)()
            if g == N_KV_HEADS - 1:
                for piece in pieces:
                    piece()
            o_g = acc[:HEAD_DIM] / acc[HEAD_DIM:HEAD_DIM + 1]
            outs += [o_g[:, r * QT:(r + 1) * QT] for r in range(rep)]
        o_ref[0] = jnp.concatenate(outs, axis=0).T.astype(BF16)

    for v in variants:
        pl.when(nv == v)(functools.partial(attend, v))


def _attention(q3, kv3, qi3, ki3, dtw3, bias, n_sel):
    bsz, seq_len, _ = q3.shape
    assert seq_len % QUAD == 0 and QUAD % QT == 0
    rep = N_ATTN_HEADS // N_KV_HEADS
    kern = functools.partial(_attn_kernel, seq_len=seq_len, n_sel=n_sel)
    steps = seq_len // QT
    this_step = lambda b, j: (b, j, 0)
    next_step = lambda b, j: (b, jnp.minimum(j + 1, steps - 1), 0)
    return pl.pallas_call(
        kern,
        grid=(bsz, steps),
        in_specs=[
            pl.BlockSpec((1, QT, ATTN_WIDTH), this_step),
            pl.BlockSpec((1, seq_len, KV_W), lambda b, j: (b, 0, 0)),
            pl.BlockSpec((1, QT, QI_W), this_step),
            pl.BlockSpec((1, seq_len, IDX_DIM), lambda b, j: (b, 0, 0)),
            pl.BlockSpec((1, QT, LANES), this_step),
            pl.BlockSpec((1, QT, QI_W), next_step),
            pl.BlockSpec((1, QT, LANES), next_step),
            pl.BlockSpec((N_ATTN_HEADS, 3, TILE, TILE), lambda b, j: (0, 0, 0, 0)),
        ],
        out_specs=pl.BlockSpec((1, QT, ATTN_WIDTH), this_step),
        out_shape=jax.ShapeDtypeStruct((bsz, seq_len, ATTN_WIDTH), BF16),
        scratch_shapes=[
            pltpu.VMEM((2, seq_len // PAIR, PAIR, QT), F32),
            pltpu.VMEM((2, seq_len // PAIR, PAIR, QT), I16),
            pltpu.VMEM((2, seq_len // PAIR, PAIR, QT), I16),
            pltpu.VMEM((seq_len // PAIR, PAIR, 2 * QT), F32),
            pltpu.VMEM((N_KV_HEADS * PV_ROWS, seq_len), BF16),
            pltpu.VMEM((seq_len // QUAD, N_KV_HEADS, QUAD, rep * QT), F32),
            pltpu.VMEM((1, QT), F32),
            pltpu.VMEM((1, QT), I32),
        ],
        compiler_params=pltpu.CompilerParams(
            dimension_semantics=("arbitrary", "arbitrary"), vmem_limit_bytes=VMEM_LIMIT),
        name="sparse_attn",
    )(q3, kv3, qi3, ki3, dtw3, qi3, dtw3, bias)


SSD_STEP_CHUNKS = 2
SSD_STEP = SSD_STEP_CHUNKS * TILE


def _split_bf16(v, n):
    parts = []
    for _ in range(n):
        p = v.astype(BF16)
        parts.append(p)
        v = v - p.astype(F32)
    return parts


def _dot_01(v, mat01, n):
    out = None
    for p in _split_bf16(v, n):
        t = jnp.dot(p, mat01, preferred_element_type=F32)
        out = t if out is None else out + t
    return out


def _ssd_kernel(z_ref, xbc_ref, dtw_ref, cw_ref, cb_ref, dtb_ref, alog_ref, dsk_ref, nw_ref,
                o_ref, xpad_ref, st_ref):
    step = pl.program_id(1)
    gn = SSD_N_GROUPS * SSD_D_STATE
    hpg = SSD_N_HEADS // SSD_N_GROUPS
    gw = hpg * SSD_HEAD_DIM

    @pl.when(step == 0)
    def _():
        xpad_ref[0:SUBLANES, :] = jnp.zeros((SUBLANES, CONV_DIM), F32)
        st_ref[...] = jnp.zeros_like(st_ref)

    xpad_ref[SUBLANES:, :] = xbc_ref[0]
    xfull = xpad_ref[...]
    conv = cb_ref[...] + cw_ref[CONV_WIDTH - 1:CONV_WIDTH, :] * xfull[SUBLANES:]
    for k in range(CONV_WIDTH - 1):
        shifted = pltpu.roll(xfull, CONV_WIDTH - 1 - k, 0)
        conv = conv + cw_ref[k:k + 1, :] * shifted[SUBLANES:]
    xpad_ref[0:SUBLANES, :] = xfull[SSD_STEP:]
    act = conv * jax.nn.sigmoid(conv)
    xs = act[:, :SSD_D_INNER]
    bm = act[:, SSD_D_INNER:SSD_D_INNER + gn].astype(BF16)
    cm = act[:, SSD_D_INNER + gn:].astype(BF16)

    dt_in = dtw_ref[0] + dtb_ref[...]
    dt = jnp.maximum(dt_in, 0.0) + jnp.log1p(jnp.exp(-jnp.abs(dt_in)))
    adt = dt * (-jnp.exp(alog_ref[...]))
    row = lax.broadcasted_iota(I32, (SSD_STEP, SSD_STEP), 0)
    col = lax.broadcasted_iota(I32, (SSD_STEP, SSD_STEP), 1)
    same_chunk = (row // TILE) == (col // TILE)
    chunk_tril = jnp.where(same_chunk & (row >= col), 1.0, 0.0).astype(BF16)
    acs = None
    for p in _split_bf16(adt, 3):
        t = jnp.dot(chunk_tril, p, preferred_element_type=F32)
        acs = t if acs is None else acs + t
    acs_t = acs.T
    causal = (lax.broadcasted_iota(I32, (TILE, TILE), 0) >= lax.broadcasted_iota(I32, (TILE, TILE), 1))
    a_last = [acs[(c + 1) * TILE - 1:(c + 1) * TILE, :] for c in range(SSD_STEP_CHUNKS)]
    out_decay = jnp.concatenate(
        [jnp.exp(a_last[c] - acs[c * TILE:(c + 1) * TILE]) for c in range(SSD_STEP_CHUNKS)], axis=0)
    chunk_decay = jnp.concatenate(
        [jnp.exp(a) for a in a_last] + [jnp.zeros((SUBLANES - SSD_STEP_CHUNKS, LANES), F32)], axis=0)
    hsel = (lax.broadcasted_iota(I32, (LANES, SSD_D_INNER), 0)
            == lax.broadcasted_iota(I32, (LANES, SSD_D_INNER), 1) // SSD_HEAD_DIM).astype(BF16)
    expanded = _dot_01(jnp.concatenate([dt, jnp.exp(acs), out_decay, chunk_decay], axis=0), hsel, 2)
    dt_x = expanded[:SSD_STEP]
    in_decay_x = expanded[SSD_STEP:2 * SSD_STEP]
    out_decay_x = expanded[2 * SSD_STEP:3 * SSD_STEP]
    chunk_decay_x = expanded[3 * SSD_STEP:]

    x_dt = xs * dt_x
    x_dt_b = x_dt.astype(BF16)
    x_out_b = (x_dt * out_decay_x).astype(BF16)
    tn_dims = (((0,), (0,)), ((), ()))
    nt_dims = (((1,), (1,)), ((), ()))
    states = [st_ref[g] for g in range(SSD_N_GROUPS)]
    y_chunks = []
    for c in range(SSD_STEP_CHUNKS):
        rows = slice(c * TILE, (c + 1) * TILE)
        y_parts = []
        for g in range(SSD_N_GROUPS):
            bg = bm[rows, g * SSD_D_STATE:(g + 1) * SSD_D_STATE]
            cg = cm[rows, g * SSD_D_STATE:(g + 1) * SSD_D_STATE]
            lanes = slice(g * gw, (g + 1) * gw)
            cb = lax.dot_general(cg, bg, nt_dims, preferred_element_type=F32)
            y_diag = []
            for r in range(hpg):
                h = g * hpg + r
                seg = jnp.where(causal, acs[rows, h:h + 1] - acs_t[h:h + 1, rows], NEG_INF)
                w = (cb * jnp.exp(seg)).astype(BF16)
                y_diag.append(jnp.dot(w, x_dt_b[rows, h * SSD_HEAD_DIM:(h + 1) * SSD_HEAD_DIM],
                                      preferred_element_type=F32))
            y_off = (jnp.dot(cg, states[g].astype(BF16), preferred_element_type=F32)
                     * in_decay_x[rows, lanes])
            y_parts.append(jnp.concatenate(y_diag, axis=-1) + y_off)
            st_new = lax.dot_general(bg, x_out_b[rows, lanes], tn_dims, preferred_element_type=F32)
            states[g] = states[g] * chunk_decay_x[c:c + 1, lanes] + st_new
        y_chunks.append(jnp.concatenate(y_parts, axis=-1))
    for g in range(SSD_N_GROUPS):
        st_ref[g] = states[g]

    y = jnp.concatenate(y_chunks, axis=0) + xs * dsk_ref[...]
    zv = z_ref[0]
    y = y * (zv * jax.nn.sigmoid(zv))
    ng = SSD_D_INNER // SSD_N_GROUPS
    outs = []
    for g in range(SSD_N_GROUPS):
        yg = y[:, g * ng:(g + 1) * ng]
        outs.append(yg * lax.rsqrt(jnp.mean(yg * yg, axis=-1, keepdims=True) + EPS))
    o_ref[0] = (jnp.concatenate(outs, axis=-1) * nw_ref[...]).astype(BF16)


def _ssd(z3, xbc3, dtw3, conv_w, conv_b, dt_bias, a_log, d_skip, norm_w):
    bsz, seq_len, _ = z3.shape
    assert seq_len % SSD_STEP == 0
    nc = seq_len // SSD_STEP
    blk = lambda w: pl.BlockSpec((1, SSD_STEP, w), lambda b, c: (b, c, 0))
    par = lambda r, w: pl.BlockSpec((r, w), lambda b, c: (0, 0))
    return pl.pallas_call(
        _ssd_kernel,
        grid=(bsz, nc),
        in_specs=[blk(SSD_D_INNER), blk(CONV_DIM), blk(LANES),
                  par(CONV_WIDTH, CONV_DIM), par(1, CONV_DIM), par(1, LANES), par(1, LANES),
                  par(1, SSD_D_INNER), par(1, SSD_D_INNER)],
        out_specs=blk(SSD_D_INNER),
        out_shape=jax.ShapeDtypeStruct((bsz, seq_len, SSD_D_INNER), BF16),
        scratch_shapes=[
            pltpu.VMEM((SUBLANES + SSD_STEP, CONV_DIM), F32),
            pltpu.VMEM((SSD_N_GROUPS, SSD_D_STATE, SSD_D_INNER // SSD_N_GROUPS), F32),
        ],
        compiler_params=pltpu.CompilerParams(
            dimension_semantics=("arbitrary", "arbitrary"), vmem_limit_bytes=VMEM_LIMIT),
        name="ssd_mixer",
    )(z3, xbc3, dtw3, conv_w, conv_b, dt_bias, a_log, d_skip, norm_w)


FF_CHUNK = 1024


def _mlp_kernel(x_ref, attn_ref, ssd_ref, wo_ref, g1_ref, g2_ref, wu_ref, wd_ref, g3_ref, o_ref):
    mix = (jnp.dot(attn_ref[...], wo_ref[:ATTN_WIDTH, :], preferred_element_type=F32)
           + jnp.dot(ssd_ref[...], wo_ref[ATTN_WIDTH:, :], preferred_element_type=F32))
    h1 = x_ref[...] + _rms(mix, g1_ref[...])
    u = _rms(h1, g2_ref[...]).astype(BF16)
    acc = jnp.zeros(h1.shape, F32)
    for c in range(0, D_FF, FF_CHUNK):
        f = jnp.dot(u, wu_ref[:, c:c + FF_CHUNK], preferred_element_type=F32)
        f = jnp.square(jnp.maximum(f, 0.0)).astype(BF16)
        acc = acc + jnp.dot(f, wd_ref[c:c + FF_CHUNK, :], preferred_element_type=F32)
    o_ref[...] = h1 + _rms(acc, g3_ref[...])


def _mlp(x2, attn2, ssd2, wo, g1, g2, wu, wd, g3, tm):
    n = x2.shape[0]
    row = lambda i: (i, 0)
    const = lambda i: (0, 0)
    single = dict(pipeline_mode=pl.Buffered(1))
    return pl.pallas_call(
        _mlp_kernel,
        grid=(n // tm,),
        in_specs=[
            pl.BlockSpec((tm, D_MODEL), row),
            pl.BlockSpec((tm, ATTN_WIDTH), row),
            pl.BlockSpec((tm, SSD_D_INNER), row),
            pl.BlockSpec((ATTN_WIDTH + SSD_D_INNER, D_MODEL), const, **single),
            pl.BlockSpec((1, D_MODEL), const),
            pl.BlockSpec((1, D_MODEL), const),
            pl.BlockSpec((D_MODEL, D_FF), const, **single),
            pl.BlockSpec((D_FF, D_MODEL), const, **single),
            pl.BlockSpec((1, D_MODEL), const),
        ],
        out_specs=pl.BlockSpec((tm, D_MODEL), row),
        out_shape=jax.ShapeDtypeStruct((n, D_MODEL), F32),
        compiler_params=pltpu.CompilerParams(
            dimension_semantics=("arbitrary",), vmem_limit_bytes=VMEM_LIMIT),
        name="out_proj_mlp",
    )(x2, attn2, ssd2, wo, g1, g2, wu, wd, g3)


def _pad_lanes(v, n):
    return jnp.pad(v, (0, n - v.shape[0])).reshape(1, n)


def kernel(x, norm_pre_mix, norm_post_mix, norm_pre_mlp, norm_post_mlp, w_in, k_idx_ln_w, k_idx_ln_b, conv_w, conv_b, dt_bias, a_log, d_skip, ssd_norm_w, w_out, w_mlp_up, w_mlp_down, rel_bias):
    bsz, seq_len, d = x.shape
    n = bsz * seq_len
    assert d == D_MODEL and seq_len % TILE == 0
    tm = next(t for t in (1024, 512, TILE) if n % t == 0)
    n_sel = min(TOPK_MAX, seq_len // 4)
    bias = _bias_tiles(rel_bias)
    h = x.reshape(n, d)
    for i in range(norm_pre_mix.shape[0]):
        row = lambda v: v[i].reshape(1, -1)
        q, kv, qi, ki, z, xbc, dtw = _in_proj(h, row(norm_pre_mix), w_in[i],
                                              row(k_idx_ln_w), row(k_idx_ln_b), tm)
        r3 = lambda a: a.reshape(bsz, seq_len, a.shape[-1])
        dtw3 = r3(dtw)
        attn = _attention(r3(q), r3(kv), r3(qi), r3(ki), dtw3, bias, n_sel)
        ssd = _ssd(r3(z), r3(xbc), dtw3, conv_w[i], row(conv_b),
                   _pad_lanes(dt_bias[i], LANES), _pad_lanes(a_log[i], LANES),
                   jnp.repeat(d_skip[i], SSD_HEAD_DIM).reshape(1, -1), row(ssd_norm_w))
        h = _mlp(h, attn.reshape(n, -1), ssd.reshape(n, -1), w_out[i].astype(BF16),
                 row(norm_post_mix), row(norm_pre_mlp), w_mlp_up[i].astype(BF16),
                 w_mlp_down[i].astype(BF16), row(norm_post_mlp), tm)
    return h.reshape(bsz, seq_len, d)
```

```python
import functools
import math

import jax
import jax.numpy as jnp
from jax import lax
from jax.experimental import pallas as pl
from jax.experimental.pallas import tpu as pltpu

F32 = jnp.float32
BF16 = jnp.bfloat16
I32 = jnp.int32

D_MODEL = 1024
N_ATTN_HEADS = 8
N_KV_HEADS = 2
HEAD_DIM = 64
ATTN_WIDTH = N_ATTN_HEADS * HEAD_DIM
N_IDX_HEADS = 4
IDX_DIM = 64
TOPK_MAX = 256
NUM_BUCKETS = 32
MAX_DISTANCE = 128
SSD_D_INNER = 512
SSD_HEAD_DIM = 64
SSD_N_HEADS = 8
SSD_N_GROUPS = 2
SSD_D_STATE = 128
CONV_WIDTH = 4
CONV_DIM = SSD_D_INNER + 2 * SSD_N_GROUPS * SSD_D_STATE
D_FF = 4 * D_MODEL
EPS = 1e-6

LANES = 128
SUBLANES = 8
TILE = 128
VMEM_LIMIT = 56 * 1024 * 1024

QKV_W = ATTN_WIDTH + 2 * N_KV_HEADS * HEAD_DIM + N_IDX_HEADS * IDX_DIM
KV_W = 2 * N_KV_HEADS * HEAD_DIM
QI_W = N_IDX_HEADS * IDX_DIM
COL_QKV = 0
COL_KV = COL_QKV + ATTN_WIDTH
COL_QI = COL_KV + KV_W
COL_KI = COL_QKV + QKV_W
COL_Z = COL_KI + LANES
COL_XBC = COL_Z + SSD_D_INNER
COL_DTW = COL_XBC + CONV_DIM
W_CAT = COL_DTW + LANES
DTW_WI = SSD_N_HEADS
SRC_QKV = 0
SRC_KI = SRC_QKV + QKV_W
SRC_WI = SRC_KI + IDX_DIM
SRC_Z = SRC_WI + N_IDX_HEADS
SRC_XBC = SRC_Z + SSD_D_INNER
SRC_DT = SRC_XBC + CONV_DIM

NEG_INF = float("-inf")
LOG2E = 1.4426950408889634
Q_SCALE = LOG2E * HEAD_DIM ** -0.5


def _rms(x, g):
    return x * lax.rsqrt(jnp.mean(x * x, axis=-1, keepdims=True) + EPS) * g


def _in_proj_kernel(x_ref, g_ref, wt_ref, lnw_ref, lnb_ref,
                    q_ref, kv_ref, qi_ref, ki_ref, z_ref, xbc_ref, dtw_ref, w_ref):
    @pl.when(pl.program_id(0) == 0)
    def _():
        def put(col, pieces):
            rows = [wt_ref[lo:lo + width, :] for lo, width in pieces]
            used = sum(width for _, width in pieces)
            if used < LANES:
                rows.append(jnp.zeros((LANES - used, D_MODEL), F32))
            blk = rows[0] if len(rows) == 1 else jnp.concatenate(rows, axis=0)
            w_ref[:, col:col + LANES] = blk.T.astype(BF16)

        for k in range(QKV_W // LANES):
            put(COL_QKV + k * LANES, [(SRC_QKV + k * LANES, LANES)])
        put(COL_KI, [(SRC_KI, IDX_DIM)])
        for k in range(SSD_D_INNER // LANES):
            put(COL_Z + k * LANES, [(SRC_Z + k * LANES, LANES)])
        for k in range(CONV_DIM // LANES):
            put(COL_XBC + k * LANES, [(SRC_XBC + k * LANES, LANES)])
        put(COL_DTW, [(SRC_DT, SSD_N_HEADS), (SRC_WI, N_IDX_HEADS)])

    u = _rms(x_ref[...], g_ref[...]).astype(BF16)

    def mm(lo, hi):
        return jnp.dot(u, w_ref[:, lo:hi], preferred_element_type=F32)

    q_ref[...] = (mm(COL_QKV, COL_KV) * Q_SCALE).astype(BF16)
    kv_ref[...] = mm(COL_KV, COL_QI).astype(BF16)
    qi_ref[...] = mm(COL_QI, COL_KI).astype(BF16)
    ki = mm(COL_KI, COL_Z)[:, :IDX_DIM]
    mu = jnp.mean(ki, axis=-1, keepdims=True)
    var = jnp.mean(jnp.square(ki - mu), axis=-1, keepdims=True)
    ki_ref[...] = ((ki - mu) * lax.rsqrt(var + EPS) * lnw_ref[...] + lnb_ref[...]).astype(BF16)
    z_ref[...] = mm(COL_Z, COL_XBC)
    xbc_ref[...] = mm(COL_XBC, COL_DTW)
    dtw_ref[...] = mm(COL_DTW, W_CAT)


def _in_proj(x2, g, w_in, lnw, lnb, tm):
    n = x2.shape[0]
    assert w_in.shape == (SRC_DT + SSD_N_HEADS, D_MODEL)
    row = lambda i: (i, 0)
    const = lambda i: (0, 0)
    return pl.pallas_call(
        _in_proj_kernel,
        grid=(n // tm,),
        in_specs=[
            pl.BlockSpec((tm, D_MODEL), row),
            pl.BlockSpec((1, D_MODEL), const),
            pl.BlockSpec(w_in.shape, const, pipeline_mode=pl.Buffered(1)),
            pl.BlockSpec((1, IDX_DIM), const),
            pl.BlockSpec((1, IDX_DIM), const),
        ],
        out_specs=[
            pl.BlockSpec((tm, ATTN_WIDTH), row),
            pl.BlockSpec((tm, KV_W), row),
            pl.BlockSpec((tm, QI_W), row),
            pl.BlockSpec((tm, IDX_DIM), row),
            pl.BlockSpec((tm, SSD_D_INNER), row),
            pl.BlockSpec((tm, CONV_DIM), row),
            pl.BlockSpec((tm, LANES), row),
        ],
        out_shape=[
            jax.ShapeDtypeStruct((n, ATTN_WIDTH), BF16),
            jax.ShapeDtypeStruct((n, KV_W), BF16),
            jax.ShapeDtypeStruct((n, QI_W), BF16),
            jax.ShapeDtypeStruct((n, IDX_DIM), BF16),
            jax.ShapeDtypeStruct((n, SSD_D_INNER), F32),
            jax.ShapeDtypeStruct((n, CONV_DIM), F32),
            jax.ShapeDtypeStruct((n, LANES), F32),
        ],
        scratch_shapes=[pltpu.VMEM((D_MODEL, W_CAT), BF16)],
        compiler_params=pltpu.CompilerParams(
            dimension_semantics=("arbitrary",), vmem_limit_bytes=VMEM_LIMIT),
        name="in_proj",
    )(x2, g, w_in, lnw, lnb)


def _bias_kernel(tbl_ref, out_ref):
    sk = lax.broadcasted_iota(I32, (TILE, TILE), 0)
    tq = lax.broadcasted_iota(I32, (TILE, TILE), 1)
    max_exact = NUM_BUCKETS // 2
    for off in range(2):
        dist = jnp.maximum(off * TILE + tq - sk, 0)
        df = jnp.maximum(dist, 1).astype(F32)
        large = max_exact + (jnp.log(df / max_exact) / math.log(MAX_DISTANCE / max_exact)
                             * (NUM_BUCKETS - max_exact)).astype(I32)
        large = jnp.minimum(large, NUM_BUCKETS - 1)
        bucket = jnp.where(dist < max_exact, dist, large)
        for h in range(N_ATTN_HEADS):
            acc = jnp.zeros((TILE, TILE), F32)
            for b in range(NUM_BUCKETS):
                acc = jnp.where(bucket == b, tbl_ref[b, h], acc)
            out_ref[h, off] = (acc - tbl_ref[NUM_BUCKETS - 1, h]) * LOG2E
    for h in range(N_ATTN_HEADS):
        out_ref[h, 2] = jnp.zeros((TILE, TILE), F32)


def _bias_tiles(rel_bias):
    return pl.pallas_call(
        _bias_kernel,
        in_specs=[pl.BlockSpec(memory_space=pltpu.SMEM)],
        out_specs=pl.BlockSpec(memory_space=pltpu.VMEM),
        out_shape=jax.ShapeDtypeStruct((N_ATTN_HEADS, 3, TILE, TILE), F32),
        name="bias_tiles",
    )(rel_bias)


def _key_to_f32(u):
    ks = u ^ I32(-2 ** 31)
    bits = jnp.where(ks >= 0, ks, ks ^ I32(0x7FFFFFFF))
    return lax.bitcast_convert_type(bits, F32)


PAIR = 2 * TILE
QUAD = 4 * TILE
QT = 2 * TILE
HALF_BITS = 16
HALF_MASK = 2 ** HALF_BITS - 1
HALF_BIAS = 2 ** (HALF_BITS - 1)
LO_BITS_FIRST = 10
I16 = jnp.int16
PV_ROWS = HEAD_DIM + 16


def _attn_kernel(q_ref, kv_ref, qi_ref, ki_ref, dtw_ref, qi_next_ref, dtw_next_ref, bias_ref, o_ref,
                 sc_ref, hi_ref, lo_ref, rk_ref, vt_ref, lg_ref, thr_ref, cnt_ref, *, seq_len, n_sel):
    j = pl.program_id(1)
    i_hi = j * (QT // TILE) + QT // TILE - 1
    npair = i_hi // 2 + 1
    n_total = seq_len // PAIR
    merged = n_total - 1 if n_total >= 4 else 0
    eff = lambda v: n_total if v == merged else v
    variants = [v for v in range(1, n_total + 1) if v != merged]
    nv = jnp.where(npair == merged, n_total, npair)
    slot = lax.rem(j, 2)

    def quad_keys(v):
        return [QUAD] * (v // 2) + [PAIR] * (v % 2)

    rep = N_ATTN_HEADS // N_KV_HEADS
    gl = rep * QT
    nt_dims = (((1,), (1,)), ((), ()))
    v_lo = N_KV_HEADS * HEAD_DIM

    @pl.when(j == 0)
    def _():
        vt = kv_ref[0, :, v_lo:].astype(F32).T
        ones_row = jnp.where(lax.broadcasted_iota(I32, (PV_ROWS - HEAD_DIM, seq_len), 0) == 0, 1.0, 0.0)
        for g in range(N_KV_HEADS):
            vt_ref[g * PV_ROWS:g * PV_ROWS + HEAD_DIM, :] = vt[g * HEAD_DIM:(g + 1) * HEAD_DIM].astype(BF16)
            vt_ref[g * PV_ROWS + HEAD_DIM:(g + 1) * PV_ROWS, :] = ones_row.astype(BF16)

    idx_scale = (N_IDX_HEADS ** -0.5) * (IDX_DIM ** -0.5)
    s_loc = lax.broadcasted_iota(I32, (PAIR, QT), 0)
    s_minus_t = s_loc - lax.broadcasted_iota(I32, (PAIR, QT), 1)

    def score_quads(v, to_slot, step, qi_src, dtw_src):
        w_t = dtw_src[0].T
        w_rows = [w_t[DTW_WI + h:DTW_WI + h + 1, :] * idx_scale for h in range(N_IDX_HEADS)]
        qi = qi_src[0]
        qi_all = jnp.concatenate(
            [qi[:, h * IDX_DIM:(h + 1) * IDX_DIM] for h in range(N_IDX_HEADS)], axis=0)
        def piece(c, nk):
            kt = ki_ref[0, c * QUAD:c * QUAD + nk, :]
            d = lax.dot_general(kt, qi_all, nt_dims, preferred_element_type=F32)
            s = jnp.zeros((nk, QT), F32)
            for h in range(N_IDX_HEADS):
                s = s + w_rows[h] * jnp.maximum(d[:, h * QT:(h + 1) * QT], 0.0)
            for u in range(nk // PAIR):
                pr = c * (QUAD // PAIR) + u
                su = jnp.where(s_minus_t <= step * QT - pr * PAIR, s[u * PAIR:(u + 1) * PAIR], NEG_INF)
                sc_ref[to_slot, pr] = su
                bits = lax.bitcast_convert_type(su, I32)
                key = bits ^ (lax.shift_right_arithmetic(bits, I32(31)) & I32(0x7FFFFFFF))
                hi_ref[to_slot, pr] = lax.shift_right_arithmetic(key, I32(HALF_BITS)).astype(I16)
                lo_ref[to_slot, pr] = (key ^ I32(HALF_BIAS)).astype(I16)

        return [functools.partial(piece, c, nk) for c, nk in enumerate(quad_keys(v))]

    @pl.when(j == 0)
    def _():
        for piece in score_quads(1, 0, 0, qi_ref, dtw_ref):
            piece()

    t_glob = j * QT + lax.broadcasted_iota(I32, (1, QT), 1)
    k_eff = jnp.minimum(n_sel, t_glob + 1)
    acc_rows = 4 * 2 * SUBLANES
    one16, zero16 = jnp.ones((), I16), jnp.zeros((), I16)

    def fold(cnt, rows=acc_rows):
        parts = [cnt[k * rows:(k + 1) * rows] for k in range(cnt.shape[0] // rows)]
        while len(parts) > 1:
            parts = [a + b for a, b in zip(parts[::2], parts[1::2])]
        return parts[0]

    def total(acc):
        packed = fold(acc, 2 * SUBLANES).astype(I32)
        return (packed[:SUBLANES] + packed[SUBLANES:]).sum(axis=0, keepdims=True)

    def to16(u):
        return (u - I32(HALF_BIAS)).astype(I16)

    def search(v):
        def count_ge(plane_ref, cand16):
            acc = jnp.zeros((acc_rows, QT), I16)
            for pr in range(v):
                acc = acc + fold(jnp.where(plane_ref[slot, pr] >= cand16, one16, zero16))
            return total(acc)

        def search_step(b, carry, plane_ref, base):
            prefix, cnt_ge, settled = carry
            cand = prefix | lax.shift_left(I32(1), HALF_BITS - 1 - b)
            cnt = count_ge(plane_ref, to16(cand)) + base
            ok = cnt >= k_eff
            settled = jnp.where(ok, jnp.where(cnt == k_eff, 1, settled), settled)
            return jnp.where(ok, cand, prefix), jnp.where(ok, cnt, cnt_ge), settled

        carry = (jnp.zeros((1, QT), I32), t_glob + 1, jnp.zeros((1, QT), I32))
        hi_u, cnt_ge, settled = lax.fori_loop(
            0, HALF_BITS, functools.partial(search_step, plane_ref=hi_ref, base=0), carry)
        hi16 = to16(hi_u)

        acc = jnp.zeros((acc_rows, QT), I16)
        for pr in range(v):
            h = hi_ref[slot, pr]
            lo_ref[slot, pr] = jnp.where(h == hi16, lo_ref[slot, pr], I16(-HALF_BIAS))
            acc = acc + fold(jnp.where(h > hi16, one16, zero16))
        above = total(acc)
        lo_step = functools.partial(search_step, plane_ref=lo_ref, base=above)
        floor_tie = count_ge(lo_ref, to16(jnp.ones((1, QT), I32))) + above < k_eff
        carry = (jnp.zeros((1, QT), I32), cnt_ge, jnp.where(floor_tie, 1, settled))
        carry = lax.fori_loop(0, LO_BITS_FIRST, lo_step, carry)
        lo_u, cnt_ge, _ = lax.cond(
            jnp.min(carry[2]) > 0, lambda c: c,
            lambda c: lax.fori_loop(LO_BITS_FIRST, HALF_BITS, lo_step, c), carry)
        thr_ref[...] = _key_to_f32(lax.shift_left(hi_u, I32(HALF_BITS)) | lo_u)
        cnt_ref[...] = cnt_ge

    for v in variants:
        pl.when(nv == v)(functools.partial(search, v))

    q = q_ref[0]
    qgs = [jnp.concatenate(
        [q[:, (g * rep + r) * HEAD_DIM:(g * rep + r + 1) * HEAD_DIM] for r in range(rep)],
        axis=0) for g in range(N_KV_HEADS)]
    below_diag = (lax.broadcasted_iota(I32, (PAIR, PAIR), 0)
                  > lax.broadcasted_iota(I32, (PAIR, PAIR), 1)).astype(BF16)
    tiles_per_quad = QUAD // TILE
    pairs_per_quad = QUAD // PAIR
    n_qt = QT // TILE
    int_min = I32(-2 ** 31)

    def bias_rows(c, g, u):
        cols = []
        for r in range(rep):
            for w in range(n_qt):
                off = jnp.clip(j * n_qt + w - (c * tiles_per_quad + u), 0, 2)
                cols.append(bias_ref[g * rep + r, off])
        return jnp.concatenate(cols, axis=1)

    def attend(v):
        thr = thr_ref[...]
        cnt_ge = cnt_ref[...]

        def tied(s):
            eq = jnp.where(s == thr, 1.0, 0.0)
            neg0 = eq * jnp.where(lax.bitcast_convert_type(s, I32) == int_min, 1.0, 0.0)
            return eq - neg0, neg0

        tots = []
        for pr in range(v):
            e2 = jnp.concatenate(tied(sc_ref[slot, pr]), axis=1).astype(BF16)
            r = jnp.dot(below_diag, e2, preferred_element_type=F32)
            rk_ref[pr] = r
            tots.append(r[PAIR - 1:PAIR] + e2[PAIR - 1:PAIR].astype(F32))
        tot = tots[0]
        for t in tots[1:]:
            tot = tot + t
        need = (k_eff - cnt_ge).astype(F32) + tot[:, :QT] + tot[:, QT:]
        off = jnp.concatenate([jnp.zeros((1, QT), F32), tot[:, :QT]], axis=1)

        m8 = [jnp.full((SUBLANES, gl), NEG_INF, F32) for _ in range(N_KV_HEADS)]
        nks = quad_keys(v)
        for c, nk in enumerate(nks):
            masks = []
            for u in range(nk // PAIR):
                pr = c * pairs_per_quad + u
                s = sc_ref[slot, pr]
                r = rk_ref[pr] + off
                off = off + tots[pr]
                neg0 = lax.bitcast_convert_type(s, I32) == int_min
                keep_tie = jnp.where(jnp.where(neg0, r[:, QT:], r[:, :QT]) < need, 0.0, NEG_INF)
                masks.append(jnp.where(s > thr, 0.0, jnp.where(s == thr, keep_tie, NEG_INF)))
            mask = jnp.concatenate(masks, axis=0)
            mask = jnp.concatenate([mask] * rep, axis=1)
            keys = slice(c * QUAD, c * QUAD + nk)
            for g in range(N_KV_HEADS):
                kt = kv_ref[0, keys, g * HEAD_DIM:(g + 1) * HEAD_DIM]
                lg = lax.dot_general(kt, qgs[g], nt_dims, preferred_element_type=F32) + mask
                if c == len(nks) - 1:
                    lg = lg + jnp.concatenate([bias_rows(c, g, u) for u in range(nk // TILE)], axis=0)
                elif c == len(nks) - 2 and (v % 2 == 1 or (v == n_total and merged)):
                    u = tiles_per_quad - 1
                    lg = jnp.concatenate([lg[:u * TILE], lg[u * TILE:] + bias_rows(c, g, u)], axis=0)
                lg_ref[c, g, :nk] = lg
                m8[g] = jnp.maximum(m8[g], lg.reshape(nk // SUBLANES, SUBLANES, gl).max(axis=0))

        pieces = (score_quads(eff(v + 1), 1 - slot, j + 1, qi_next_ref, dtw_next_ref)
                  if v < n_total else [])
        if v == n_total and merged:
            @pl.when(npair < n_total)
            def _():
                for piece in score_quads(n_total, 1 - slot, j + 1, qi_next_ref, dtw_next_ref):
                    piece()

        outs = []
        for g in range(N_KV_HEADS):
            m_row = jnp.max(m8[g], axis=0, keepdims=True)
            acc = jnp.zeros((PV_ROWS, gl), F32)
            for c, nk in enumerate(nks):
                p = jnp.exp2((lg_ref[c, g, :nk] - m_row).astype(BF16))
                vt = vt_ref[g * PV_ROWS:(g + 1) * PV_ROWS, c * QUAD:c * QUAD + nk]
                acc = acc + jnp.dot(vt, p, preferred_element_type=F32)
                if pieces:
                    pieces.pop(0)()
            if g == N_KV_HEADS - 1:
                for piece in pieces:
                    piece()
            o_g = acc[:HEAD_DIM] / acc[HEAD_DIM:HEAD_DIM + 1]
            outs += [o_g[:, r * QT:(r + 1) * QT] for r in range(rep)]
        o_ref[0] = jnp.concatenate(outs, axis=0).T.astype(BF16)

    for v in variants:
        pl.when(nv == v)(functools.partial(attend, v))


def _attention(q3, kv3, qi3, ki3, dtw3, bias, n_sel):
    bsz, seq_len, _ = q3.shape
    assert seq_len % QUAD == 0 and QUAD % QT == 0
    rep = N_ATTN_HEADS // N_KV_HEADS
    kern = functools.partial(_attn_kernel, seq_len=seq_len, n_sel=n_sel)
    steps = seq_len // QT
    this_step = lambda b, j: (b, j, 0)
    next_step = lambda b, j: (b, jnp.minimum(j + 1, steps - 1), 0)
    return pl.pallas_call(
        kern,
        grid=(bsz, steps),
        in_specs=[
            pl.BlockSpec((1, QT, ATTN_WIDTH), this_step),
            pl.BlockSpec((1, seq_len, KV_W), lambda b, j: (b, 0, 0)),
            pl.BlockSpec((1, QT, QI_W), this_step),
            pl.BlockSpec((1, seq_len, IDX_DIM), lambda b, j: (b, 0, 0)),
            pl.BlockSpec((1, QT, LANES), this_step),
            pl.BlockSpec((1, QT, QI_W), next_step),
            pl.BlockSpec((1, QT, LANES), next_step),
            pl.BlockSpec((N_ATTN_HEADS, 3, TILE, TILE), lambda b, j: (0, 0, 0, 0)),
        ],
        out_specs=pl.BlockSpec((1, QT, ATTN_WIDTH), this_step),
        out_shape=jax.ShapeDtypeStruct((bsz, seq_len, ATTN_WIDTH), BF16),
        scratch_shapes=[
            pltpu.VMEM((2, seq_len // PAIR, PAIR, QT), F32),
            pltpu.VMEM((2, seq_len // PAIR, PAIR, QT), I16),
            pltpu.VMEM((2, seq_len // PAIR, PAIR, QT), I16),
            pltpu.VMEM((seq_len // PAIR, PAIR, 2 * QT), F32),
            pltpu.VMEM((N_KV_HEADS * PV_ROWS, seq_len), BF16),
            pltpu.VMEM((seq_len // QUAD, N_KV_HEADS, QUAD, rep * QT), F32),
            pltpu.VMEM((1, QT), F32),
            pltpu.VMEM((1, QT), I32),
        ],
        compiler_params=pltpu.CompilerParams(
            dimension_semantics=("arbitrary", "arbitrary"), vmem_limit_bytes=VMEM_LIMIT),
        name="sparse_attn",
    )(q3, kv3, qi3, ki3, dtw3, qi3, dtw3, bias)


SSD_STEP_CHUNKS = 2
SSD_STEP = SSD_STEP_CHUNKS * TILE


def _split_bf16(v, n):
    parts = []
    for _ in range(n):
        p = v.astype(BF16)
        parts.append(p)
        v = v - p.astype(F32)
    return parts


def _dot_01(v, mat01, n):
    out = None
    for p in _split_bf16(v, n):
        t = jnp.dot(p, mat01, preferred_element_type=F32)
        out = t if out is None else out + t
    return out


def _ssd_kernel(z_ref, xbc_ref, dtw_ref, cw_ref, cb_ref, dtb_ref, alog_ref, dsk_ref, nw_ref,
                o_ref, xpad_ref, st_ref):
    step = pl.program_id(1)
    gn = SSD_N_GROUPS * SSD_D_STATE
    hpg = SSD_N_HEADS // SSD_N_GROUPS
    gw = hpg * SSD_HEAD_DIM

    @pl.when(step == 0)
    def _():
        xpad_ref[0:SUBLANES, :] = jnp.zeros((SUBLANES, CONV_DIM), F32)
        st_ref[...] = jnp.zeros_like(st_ref)

    xpad_ref[SUBLANES:, :] = xbc_ref[0]
    xfull = xpad_ref[...]
    conv = cb_ref[...] + cw_ref[CONV_WIDTH - 1:CONV_WIDTH, :] * xfull[SUBLANES:]
    for k in range(CONV_WIDTH - 1):
        shifted = pltpu.roll(xfull, CONV_WIDTH - 1 - k, 0)
        conv = conv + cw_ref[k:k + 1, :] * shifted[SUBLANES:]
    xpad_ref[0:SUBLANES, :] = xfull[SSD_STEP:]
    act = conv * jax.nn.sigmoid(conv)
    xs = act[:, :SSD_D_INNER]
    bm = act[:, SSD_D_INNER:SSD_D_INNER + gn].astype(BF16)
    cm = act[:, SSD_D_INNER + gn:].astype(BF16)

    dt_in = dtw_ref[0] + dtb_ref[...]
    dt = jnp.maximum(dt_in, 0.0) + jnp.log1p(jnp.exp(-jnp.abs(dt_in)))
    adt = dt * (-jnp.exp(alog_ref[...]))
    row = lax.broadcasted_iota(I32, (SSD_STEP, SSD_STEP), 0)
    col = lax.broadcasted_iota(I32, (SSD_STEP, SSD_STEP), 1)
    same_chunk = (row // TILE) == (col // TILE)
    chunk_tril = jnp.where(same_chunk & (row >= col), 1.0, 0.0).astype(BF16)
    acs = None
    for p in _split_bf16(adt, 3):
        t = jnp.dot(chunk_tril, p, preferred_element_type=F32)
        acs = t if acs is None else acs + t
    acs_t = acs.T
    causal = (lax.broadcasted_iota(I32, (TILE, TILE), 0) >= lax.broadcasted_iota(I32, (TILE, TILE), 1))
    a_last = [acs[(c + 1) * TILE - 1:(c + 1) * TILE, :] for c in range(SSD_STEP_CHUNKS)]
    out_decay = jnp.concatenate(
        [jnp.exp(a_last[c] - acs[c * TILE:(c + 1) * TILE]) for c in range(SSD_STEP_CHUNKS)], axis=0)
    chunk_decay = jnp.concatenate(
        [jnp.exp(a) for a in a_last] + [jnp.zeros((SUBLANES - SSD_STEP_CHUNKS, LANES), F32)], axis=0)
    hsel = (lax.broadcasted_iota(I32, (LANES, SSD_D_INNER), 0)
            == lax.broadcasted_iota(I32, (LANES, SSD_D_INNER), 1) // SSD_HEAD_DIM).astype(BF16)
    expanded = _dot_01(jnp.concatenate([dt, jnp.exp(acs), out_decay, chunk_decay], axis=0), hsel, 2)
    dt_x = expanded[:SSD_STEP]
    in_decay_x = expanded[SSD_STEP:2 * SSD_STEP]
    out_decay_x = expanded[2 * SSD_STEP:3 * SSD_STEP]
    chunk_decay_x = expanded[3 * SSD_STEP:]

    x_dt = xs * dt_x
    x_dt_b = x_dt.astype(BF16)
    x_out_b = (x_dt * out_decay_x).astype(BF16)
    tn_dims = (((0,), (0,)), ((), ()))
    nt_dims = (((1,), (1,)), ((), ()))
    states = [st_ref[g] for g in range(SSD_N_GROUPS)]
    y_chunks = []
    for c in range(SSD_STEP_CHUNKS):
        rows = slice(c * TILE, (c + 1) * TILE)
        y_parts = []
        for g in range(SSD_N_GROUPS):
            bg = bm[rows, g * SSD_D_STATE:(g + 1) * SSD_D_STATE]
            cg = cm[rows, g * SSD_D_STATE:(g + 1) * SSD_D_STATE]
            lanes = slice(g * gw, (g + 1) * gw)
            cb = lax.dot_general(cg, bg, nt_dims, preferred_element_type=F32)
            y_diag = []
            for r in range(hpg):
                h = g * hpg + r
                seg = jnp.where(causal, acs[rows, h:h + 1] - acs_t[h:h + 1, rows], NEG_INF)
                w = (cb * jnp.exp(seg)).astype(BF16)
                y_diag.append(jnp.dot(w, x_dt_b[rows, h * SSD_HEAD_DIM:(h + 1) * SSD_HEAD_DIM],
                                      preferred_element_type=F32))
            y_off = (jnp.dot(cg, states[g].astype(BF16), preferred_element_type=F32)
                     * in_decay_x[rows, lanes])
            y_parts.append(jnp.concatenate(y_diag, axis=-1) + y_off)
            st_new = lax.dot_general(bg, x_out_b[rows, lanes], tn_dims, preferred_element_type=F32)
            states[g] = states[g] * chunk_decay_x[c:c + 1, lanes] + st_new
        y_chunks.append(jnp.concatenate(y_parts, axis=-1))
    for g in range(SSD_N_GROUPS):
        st_ref[g] = states[g]

    y = jnp.concatenate(y_chunks, axis=0) + xs * dsk_ref[...]
    zv = z_ref[0]
    y = y * (zv * jax.nn.sigmoid(zv))
    ng = SSD_D_INNER // SSD_N_GROUPS
    outs = []
    for g in range(SSD_N_GROUPS):
        yg = y[:, g * ng:(g + 1) * ng]
        outs.append(yg * lax.rsqrt(jnp.mean(yg * yg, axis=-1, keepdims=True) + EPS))
    o_ref[0] = (jnp.concatenate(outs, axis=-1) * nw_ref[...]).astype(BF16)


def _ssd(z3, xbc3, dtw3, conv_w, conv_b, dt_bias, a_log, d_skip, norm_w):
    bsz, seq_len, _ = z3.shape
    assert seq_len % SSD_STEP == 0
    nc = seq_len // SSD_STEP
    blk = lambda w: pl.BlockSpec((1, SSD_STEP, w), lambda b, c: (b, c, 0))
    par = lambda r, w: pl.BlockSpec((r, w), lambda b, c: (0, 0))
    return pl.pallas_call(
        _ssd_kernel,
        grid=(bsz, nc),
        in_specs=[blk(SSD_D_INNER), blk(CONV_DIM), blk(LANES),
                  par(CONV_WIDTH, CONV_DIM), par(1, CONV_DIM), par(1, LANES), par(1, LANES),
                  par(1, SSD_D_INNER), par(1, SSD_D_INNER)],
        out_specs=blk(SSD_D_INNER),
        out_shape=jax.ShapeDtypeStruct((bsz, seq_len, SSD_D_INNER), BF16),
        scratch_shapes=[
            pltpu.VMEM((SUBLANES + SSD_STEP, CONV_DIM), F32),
            pltpu.VMEM((SSD_N_GROUPS, SSD_D_STATE, SSD_D_INNER // SSD_N_GROUPS), F32),
        ],
        compiler_params=pltpu.CompilerParams(
            dimension_semantics=("arbitrary", "arbitrary"), vmem_limit_bytes=VMEM_LIMIT),
        name="ssd_mixer",
    )(z3, xbc3, dtw3, conv_w, conv_b, dt_bias, a_log, d_skip, norm_w)


FF_CHUNK = 1024


def _mlp_kernel(x_ref, attn_ref, ssd_ref, wo_ref, g1_ref, g2_ref, wu_ref, wd_ref, g3_ref, o_ref):
    mix = (jnp.dot(attn_ref[...], wo_ref[:ATTN_WIDTH, :], preferred_element_type=F32)
           + jnp.dot(ssd_ref[...], wo_ref[ATTN_WIDTH:, :], preferred_element_type=F32))
    h1 = x_ref[...] + _rms(mix, g1_ref[...])
    u = _rms(h1, g2_ref[...]).astype(BF16)
    acc = jnp.zeros(h1.shape, F32)
    for c in range(0, D_FF, FF_CHUNK):
        f = jnp.dot(u, wu_ref[:, c:c + FF_CHUNK], preferred_element_type=F32)
        f = jnp.square(jnp.maximum(f, 0.0)).astype(BF16)
        acc = acc + jnp.dot(f, wd_ref[c:c + FF_CHUNK, :], preferred_element_type=F32)
    o_ref[...] = h1 + _rms(acc, g3_ref[...])


def _mlp(x2, attn2, ssd2, wo, g1, g2, wu, wd, g3, tm):
    n = x2.shape[0]
    row = lambda i: (i, 0)
    const = lambda i: (0, 0)
    single = dict(pipeline_mode=pl.Buffered(1))
    return pl.pallas_call(
        _mlp_kernel,
        grid=(n // tm,),
        in_specs=[
            pl.BlockSpec((tm, D_MODEL), row),
            pl.BlockSpec((tm, ATTN_WIDTH), row),
            pl.BlockSpec((tm, SSD_D_INNER), row),
            pl.BlockSpec((ATTN_WIDTH + SSD_D_INNER, D_MODEL), const, **single),
            pl.BlockSpec((1, D_MODEL), const),
            pl.BlockSpec((1, D_MODEL), const),
            pl.BlockSpec((D_MODEL, D_FF), const, **single),
            pl.BlockSpec((D_FF, D_MODEL), const, **single),
            pl.BlockSpec((1, D_MODEL), const),
        ],
        out_specs=pl.BlockSpec((tm, D_MODEL), row),
        out_shape=jax.ShapeDtypeStruct((n, D_MODEL), F32),
        compiler_params=pltpu.CompilerParams(
            dimension_semantics=("arbitrary",), vmem_limit_bytes=VMEM_LIMIT),
        name="out_proj_mlp",
    )(x2, attn2, ssd2, wo, g1, g2, wu, wd, g3)


def _pad_lanes(v, n):
    return jnp.pad(v, (0, n - v.shape[0])).reshape(1, n)


def kernel(x, norm_pre_mix, norm_post_mix, norm_pre_mlp, norm_post_mlp, w_in, k_idx_ln_w, k_idx_ln_b, conv_w, conv_b, dt_bias, a_log, d_skip, ssd_norm_w, w_out, w_mlp_up, w_mlp_down, rel_bias):
    bsz, seq_len, d = x.shape
    n = bsz * seq_len
    assert d == D_MODEL and seq_len % TILE == 0
    tm = next(t for t in (1024, 512, TILE) if n % t == 0)
    n_sel = min(TOPK_MAX, seq_len // 4)
    bias = _bias_tiles(rel_bias)
    h = x.reshape(n, d)
    for i in range(norm_pre_mix.shape[0]):
        row = lambda v: v[i].reshape(1, -1)
        q, kv, qi, ki, z, xbc, dtw = _in_proj(h, row(norm_pre_mix), w_in[i].T,
                                              row(k_idx_ln_w), row(k_idx_ln_b), tm)
        r3 = lambda a: a.reshape(bsz, seq_len, a.shape[-1])
        dtw3 = r3(dtw)
        attn = _attention(r3(q), r3(kv), r3(qi), r3(ki), dtw3, bias, n_sel)
        ssd = _ssd(r3(z), r3(xbc), dtw3, conv_w[i], row(conv_b),
                   _pad_lanes(dt_bias[i], LANES), _pad_lanes(a_log[i], LANES),
                   jnp.repeat(d_skip[i], SSD_HEAD_DIM).reshape(1, -1), row(ssd_norm_w))
        h = _mlp(h, attn.reshape(n, -1), ssd.reshape(n, -1), w_out[i].astype(BF16),
                 row(norm_post_mix), row(norm_pre_mlp), w_mlp_up[i].astype(BF16),
                 w_mlp_down[i].astype(BF16), row(norm_post_mlp), tm)
    return h.reshape(bsz, seq_len, d)
```

```python
import functools
import math

import jax
import jax.numpy as jnp
from jax import lax
from jax.experimental import pallas as pl
from jax.experimental.pallas import tpu as pltpu

F32 = jnp.float32
BF16 = jnp.bfloat16
I32 = jnp.int32

D_MODEL = 1024
N_ATTN_HEADS = 8
N_KV_HEADS = 2
HEAD_DIM = 64
ATTN_WIDTH = N_ATTN_HEADS * HEAD_DIM
N_IDX_HEADS = 4
IDX_DIM = 64
TOPK_MAX = 256
NUM_BUCKETS = 32
MAX_DISTANCE = 128
SSD_D_INNER = 512
SSD_HEAD_DIM = 64
SSD_N_HEADS = 8
SSD_N_GROUPS = 2
SSD_D_STATE = 128
CONV_WIDTH = 4
CONV_DIM = SSD_D_INNER + 2 * SSD_N_GROUPS * SSD_D_STATE
D_FF = 4 * D_MODEL
EPS = 1e-6

LANES = 128
SUBLANES = 8
TILE = 128
VMEM_LIMIT = 56 * 1024 * 1024

QKV_W = ATTN_WIDTH + 2 * N_KV_HEADS * HEAD_DIM + N_IDX_HEADS * IDX_DIM
KV_W = 2 * N_KV_HEADS * HEAD_DIM
QI_W = N_IDX_HEADS * IDX_DIM
COL_QKV = 0
COL_KV = COL_QKV + ATTN_WIDTH
COL_QI = COL_KV + KV_W
COL_KI = COL_QKV + QKV_W
COL_Z = COL_KI + LANES
COL_XBC = COL_Z + SSD_D_INNER
COL_DTW = COL_XBC + CONV_DIM
W_CAT = COL_DTW + LANES
DTW_WI = SSD_N_HEADS
SRC_QKV = 0
SRC_KI = SRC_QKV + QKV_W
SRC_WI = SRC_KI + IDX_DIM
SRC_Z = SRC_WI + N_IDX_HEADS
SRC_XBC = SRC_Z + SSD_D_INNER
SRC_DT = SRC_XBC + CONV_DIM

NEG_INF = float("-inf")
LOG2E = 1.4426950408889634
Q_SCALE = LOG2E * HEAD_DIM ** -0.5


def _rms(x, g):
    return x * lax.rsqrt(jnp.mean(x * x, axis=-1, keepdims=True) + EPS) * g


def _in_proj_kernel(x_ref, g_ref, wt_ref, lnw_ref, lnb_ref,
                    q_ref, kv_ref, qi_ref, ki_ref, z_ref, xbc_ref, dtw_ref, w_ref):
    @pl.when(pl.program_id(0) == 0)
    def _():
        def put(col, pieces):
            rows = [wt_ref[lo:lo + width, :] for lo, width in pieces]
            used = sum(width for _, width in pieces)
            if used < LANES:
                rows.append(jnp.zeros((LANES - used, D_MODEL), F32))
            blk = rows[0] if len(rows) == 1 else jnp.concatenate(rows, axis=0)
            w_ref[:, col:col + LANES] = blk.T.astype(BF16)

        for k in range(QKV_W // LANES):
            put(COL_QKV + k * LANES, [(SRC_QKV + k * LANES, LANES)])
        put(COL_KI, [(SRC_KI, IDX_DIM)])
        for k in range(SSD_D_INNER // LANES):
            put(COL_Z + k * LANES, [(SRC_Z + k * LANES, LANES)])
        for k in range(CONV_DIM // LANES):
            put(COL_XBC + k * LANES, [(SRC_XBC + k * LANES, LANES)])
        put(COL_DTW, [(SRC_DT, SSD_N_HEADS), (SRC_WI, N_IDX_HEADS)])

    u = _rms(x_ref[...], g_ref[...]).astype(BF16)

    def mm(lo, hi):
        return jnp.dot(u, w_ref[:, lo:hi], preferred_element_type=F32)

    q_ref[...] = (mm(COL_QKV, COL_KV) * Q_SCALE).astype(BF16)
    kv_ref[...] = mm(COL_KV, COL_QI).astype(BF16)
    qi_ref[...] = mm(COL_QI, COL_KI).astype(BF16)
    ki = mm(COL_KI, COL_Z)[:, :IDX_DIM]
    mu = jnp.mean(ki, axis=-1, keepdims=True)
    var = jnp.mean(jnp.square(ki - mu), axis=-1, keepdims=True)
    ki_ref[...] = ((ki - mu) * lax.rsqrt(var + EPS) * lnw_ref[...] + lnb_ref[...]).astype(BF16)
    z_ref[...] = mm(COL_Z, COL_XBC)
    xbc_ref[...] = mm(COL_XBC, COL_DTW)
    dtw_ref[...] = mm(COL_DTW, W_CAT)


def _in_proj(x2, g, w_in, lnw, lnb, tm):
    n = x2.shape[0]
    assert w_in.shape == (SRC_DT + SSD_N_HEADS, D_MODEL)
    row = lambda i: (i, 0)
    const = lambda i: (0, 0)
    return pl.pallas_call(
        _in_proj_kernel,
        grid=(n // tm,),
        in_specs=[
            pl.BlockSpec((tm, D_MODEL), row),
            pl.BlockSpec((1, D_MODEL), const),
            pl.BlockSpec(w_in.shape, const, pipeline_mode=pl.Buffered(1)),
            pl.BlockSpec((1, IDX_DIM), const),
            pl.BlockSpec((1, IDX_DIM), const),
        ],
        out_specs=[
            pl.BlockSpec((tm, ATTN_WIDTH), row),
            pl.BlockSpec((tm, KV_W), row),
            pl.BlockSpec((tm, QI_W), row),
            pl.BlockSpec((tm, IDX_DIM), row),
            pl.BlockSpec((tm, SSD_D_INNER), row),
            pl.BlockSpec((tm, CONV_DIM), row),
            pl.BlockSpec((tm, LANES), row),
        ],
        out_shape=[
            jax.ShapeDtypeStruct((n, ATTN_WIDTH), BF16),
            jax.ShapeDtypeStruct((n, KV_W), BF16),
            jax.ShapeDtypeStruct((n, QI_W), BF16),
            jax.ShapeDtypeStruct((n, IDX_DIM), BF16),
            jax.ShapeDtypeStruct((n, SSD_D_INNER), F32),
            jax.ShapeDtypeStruct((n, CONV_DIM), F32),
            jax.ShapeDtypeStruct((n, LANES), F32),
        ],
        scratch_shapes=[pltpu.VMEM((D_MODEL, W_CAT), BF16)],
        compiler_params=pltpu.CompilerParams(
            dimension_semantics=("arbitrary",), vmem_limit_bytes=VMEM_LIMIT),
        name="in_proj",
    )(x2, g, w_in, lnw, lnb)


def _bias_kernel(tbl_ref, out_ref):
    sk = lax.broadcasted_iota(I32, (TILE, TILE), 0)
    tq = lax.broadcasted_iota(I32, (TILE, TILE), 1)
    max_exact = NUM_BUCKETS // 2
    for off in range(2):
        dist = jnp.maximum(off * TILE + tq - sk, 0)
        df = jnp.maximum(dist, 1).astype(F32)
        large = max_exact + (jnp.log(df / max_exact) / math.log(MAX_DISTANCE / max_exact)
                             * (NUM_BUCKETS - max_exact)).astype(I32)
        large = jnp.minimum(large, NUM_BUCKETS - 1)
        bucket = jnp.where(dist < max_exact, dist, large)
        for h in range(N_ATTN_HEADS):
            acc = jnp.zeros((TILE, TILE), F32)
            for b in range(NUM_BUCKETS):
                acc = jnp.where(bucket == b, tbl_ref[b, h], acc)
            out_ref[h, off] = (acc - tbl_ref[NUM_BUCKETS - 1, h]) * LOG2E
    for h in range(N_ATTN_HEADS):
        out_ref[h, 2] = jnp.zeros((TILE, TILE), F32)


def _bias_tiles(rel_bias):
    return pl.pallas_call(
        _bias_kernel,
        in_specs=[pl.BlockSpec(memory_space=pltpu.SMEM)],
        out_specs=pl.BlockSpec(memory_space=pltpu.VMEM),
        out_shape=jax.ShapeDtypeStruct((N_ATTN_HEADS, 3, TILE, TILE), F32),
        name="bias_tiles",
    )(rel_bias)


def _key_to_f32(u):
    ks = u ^ I32(-2 ** 31)
    bits = jnp.where(ks >= 0, ks, ks ^ I32(0x7FFFFFFF))
    return lax.bitcast_convert_type(bits, F32)


PAIR = 2 * TILE
QUAD = 4 * TILE
QT = 2 * TILE
HALF_BITS = 16
HALF_MASK = 2 ** HALF_BITS - 1
HALF_BIAS = 2 ** (HALF_BITS - 1)
LO_BITS_FIRST = 10
I16 = jnp.int16
PV_ROWS = HEAD_DIM + 16


def _attn_kernel(q_ref, kv_ref, qi_ref, ki_ref, dtw_ref, qi_next_ref, dtw_next_ref, bias_ref, o_ref,
                 sc_ref, hi_ref, lo_ref, rk_ref, vt_ref, lg_ref, thr_ref, cnt_ref, *, seq_len, n_sel):
    j = pl.program_id(1)
    i_hi = j * (QT // TILE) + QT // TILE - 1
    npair = i_hi // 2 + 1
    n_total = seq_len // PAIR
    merged = n_total - 1 if n_total >= 4 else 0
    eff = lambda v: n_total if v == merged else v
    variants = [v for v in range(1, n_total + 1) if v != merged]
    nv = jnp.where(npair == merged, n_total, npair)
    slot = lax.rem(j, 2)

    def quad_keys(v):
        return [QUAD] * (v // 2) + [PAIR] * (v % 2)

    rep = N_ATTN_HEADS // N_KV_HEADS
    gl = rep * QT
    nt_dims = (((1,), (1,)), ((), ()))
    v_lo = N_KV_HEADS * HEAD_DIM

    @pl.when(j == 0)
    def _():
        vt = kv_ref[0, :, v_lo:].astype(F32).T
        ones_row = jnp.where(lax.broadcasted_iota(I32, (PV_ROWS - HEAD_DIM, seq_len), 0) == 0, 1.0, 0.0)
        for g in range(N_KV_HEADS):
            vt_ref[g * PV_ROWS:g * PV_ROWS + HEAD_DIM, :] = vt[g * HEAD_DIM:(g + 1) * HEAD_DIM].astype(BF16)
            vt_ref[g * PV_ROWS + HEAD_DIM:(g + 1) * PV_ROWS, :] = ones_row.astype(BF16)

    idx_scale = (N_IDX_HEADS ** -0.5) * (IDX_DIM ** -0.5)
    s_loc = lax.broadcasted_iota(I32, (PAIR, QT), 0)
    s_minus_t = s_loc - lax.broadcasted_iota(I32, (PAIR, QT), 1)

    def score_quads(v, to_slot, step, qi_src, dtw_src):
        w_t = dtw_src[0].T
        w_rows = [w_t[DTW_WI + h:DTW_WI + h + 1, :] * idx_scale for h in range(N_IDX_HEADS)]
        qi = qi_src[0]
        qi_all = jnp.concatenate(
            [qi[:, h * IDX_DIM:(h + 1) * IDX_DIM] for h in range(N_IDX_HEADS)], axis=0)
        def piece(c, nk):
            kt = ki_ref[0, c * QUAD:c * QUAD + nk, :]
            d = lax.dot_general(kt, qi_all, nt_dims, preferred_element_type=F32)
            s = jnp.zeros((nk, QT), F32)
            for h in range(N_IDX_HEADS):
                s = s + w_rows[h] * jnp.maximum(d[:, h * QT:(h + 1) * QT], 0.0)
            for u in range(nk // PAIR):
                pr = c * (QUAD // PAIR) + u
                su = jnp.where(s_minus_t <= step * QT - pr * PAIR, s[u * PAIR:(u + 1) * PAIR], NEG_INF)
                sc_ref[to_slot, pr] = su
                bits = lax.bitcast_convert_type(su, I32)
                key = bits ^ (lax.shift_right_arithmetic(bits, I32(31)) & I32(0x7FFFFFFF))
                hi_ref[to_slot, pr] = lax.shift_right_arithmetic(key, I32(HALF_BITS)).astype(I16)
                lo_ref[to_slot, pr] = (key ^ I32(HALF_BIAS)).astype(I16)

        return [functools.partial(piece, c, nk) for c, nk in enumerate(quad_keys(v))]

    @pl.when(j == 0)
    def _():
        for piece in score_quads(1, 0, 0, qi_ref, dtw_ref):
            piece()

    t_glob = j * QT + lax.broadcasted_iota(I32, (1, QT), 1)
    k_eff = jnp.minimum(n_sel, t_glob + 1)
    acc_rows = 4 * 2 * SUBLANES
    one16, zero16 = jnp.ones((), I16), jnp.zeros((), I16)

    def fold(cnt, rows=acc_rows):
        parts = [cnt[k * rows:(k + 1) * rows] for k in range(cnt.shape[0] // rows)]
        while len(parts) > 1:
            parts = [a + b for a, b in zip(parts[::2], parts[1::2])]
        return parts[0]

    def total(acc):
        packed = fold(acc, 2 * SUBLANES).astype(I32)
        return (packed[:SUBLANES] + packed[SUBLANES:]).sum(axis=0, keepdims=True)

    def to16(u):
        return (u - I32(HALF_BIAS)).astype(I16)

    def search(v):
        def count_ge(plane_ref, cand16):
            acc = jnp.zeros((acc_rows, QT), I16)
            for pr in range(v):
                acc = acc + fold(jnp.where(plane_ref[slot, pr] >= cand16, one16, zero16))
            return total(acc)

        def search_step(b, carry, plane_ref, base):
            prefix, cnt_ge, settled = carry
            cand = prefix | lax.shift_left(I32(1), HALF_BITS - 1 - b)
            cnt = count_ge(plane_ref, to16(cand)) + base
            ok = cnt >= k_eff
            settled = jnp.where(ok, jnp.where(cnt == k_eff, 1, settled), settled)
            return jnp.where(ok, cand, prefix), jnp.where(ok, cnt, cnt_ge), settled

        carry = (jnp.zeros((1, QT), I32), t_glob + 1, jnp.zeros((1, QT), I32))
        hi_u, cnt_ge, settled = lax.fori_loop(
            0, HALF_BITS, functools.partial(search_step, plane_ref=hi_ref, base=0), carry)
        hi16 = to16(hi_u)

        acc = jnp.zeros((acc_rows, QT), I16)
        for pr in range(v):
            h = hi_ref[slot, pr]
            lo_ref[slot, pr] = jnp.where(h == hi16, lo_ref[slot, pr], I16(-HALF_BIAS))
            acc = acc + fold(jnp.where(h > hi16, one16, zero16))
        above = total(acc)
        lo_step = functools.partial(search_step, plane_ref=lo_ref, base=above)
        floor_tie = count_ge(lo_ref, to16(jnp.ones((1, QT), I32))) + above < k_eff
        carry = (jnp.zeros((1, QT), I32), cnt_ge, jnp.where(floor_tie, 1, settled))
        carry = lax.fori_loop(0, LO_BITS_FIRST, lo_step, carry)
        lo_u, cnt_ge, _ = lax.cond(
            jnp.min(carry[2]) > 0, lambda c: c,
            lambda c: lax.fori_loop(LO_BITS_FIRST, HALF_BITS, lo_step, c), carry)
        thr_ref[...] = _key_to_f32(lax.shift_left(hi_u, I32(HALF_BITS)) | lo_u)
        cnt_ref[...] = cnt_ge

    for v in variants:
        pl.when(nv == v)(functools.partial(search, v))

    q = q_ref[0]
    qgs = [jnp.concatenate(
        [q[:, (g * rep + r) * HEAD_DIM:(g * rep + r + 1) * HEAD_DIM] for r in range(rep)],
        axis=0) for g in range(N_KV_HEADS)]
    below_diag = (lax.broadcasted_iota(I32, (PAIR, PAIR), 0)
                  > lax.broadcasted_iota(I32, (PAIR, PAIR), 1)).astype(BF16)
    tiles_per_quad = QUAD // TILE
    pairs_per_quad = QUAD // PAIR
    n_qt = QT // TILE
    int_min = I32(-2 ** 31)

    def bias_rows(c, g, u):
        cols = []
        for r in range(rep):
            for w in range(n_qt):
                off = jnp.clip(j * n_qt + w - (c * tiles_per_quad + u), 0, 2)
                cols.append(bias_ref[g * rep + r, off])
        return jnp.concatenate(cols, axis=1)

    def attend(v):
        thr = thr_ref[...]
        cnt_ge = cnt_ref[...]

        def tied(s):
            eq = jnp.where(s == thr, 1.0, 0.0)
            neg0 = eq * jnp.where(lax.bitcast_convert_type(s, I32) == int_min, 1.0, 0.0)
            return eq - neg0, neg0

        tots = []
        for pr in range(v):
            e2 = jnp.concatenate(tied(sc_ref[slot, pr]), axis=1).astype(BF16)
            r = jnp.dot(below_diag, e2, preferred_element_type=F32)
            rk_ref[pr] = r
            tots.append(r[PAIR - 1:PAIR] + e2[PAIR - 1:PAIR].astype(F32))
        tot = tots[0]
        for t in tots[1:]:
            tot = tot + t
        need = (k_eff - cnt_ge).astype(F32) + tot[:, :QT] + tot[:, QT:]
        off = jnp.concatenate([jnp.zeros((1, QT), F32), tot[:, :QT]], axis=1)

        m8 = [jnp.full((SUBLANES, gl), NEG_INF, F32) for _ in range(N_KV_HEADS)]
        nks = quad_keys(v)
        for c, nk in enumerate(nks):
            masks = []
            for u in range(nk // PAIR):
                pr = c * pairs_per_quad + u
                s = sc_ref[slot, pr]
                r = rk_ref[pr] + off
                off = off + tots[pr]
                neg0 = lax.bitcast_convert_type(s, I32) == int_min
                keep_tie = jnp.where(jnp.where(neg0, r[:, QT:], r[:, :QT]) < need, 0.0, NEG_INF)
                masks.append(jnp.where(s > thr, 0.0, jnp.where(s == thr, keep_tie, NEG_INF)))
            mask = jnp.concatenate(masks, axis=0)
            mask = jnp.concatenate([mask] * rep, axis=1)
            keys = slice(c * QUAD, c * QUAD + nk)
            for g in range(N_KV_HEADS):
                kt = kv_ref[0, keys, g * HEAD_DIM:(g + 1) * HEAD_DIM]
                lg = lax.dot_general(kt, qgs[g], nt_dims, preferred_element_type=F32) + mask
                if c == len(nks) - 1:
                    lg = lg + jnp.concatenate([bias_rows(c, g, u) for u in range(nk // TILE)], axis=0)
                elif c == len(nks) - 2 and (v % 2 == 1 or (v == n_total and merged)):
                    u = tiles_per_quad - 1
                    lg = jnp.concatenate([lg[:u * TILE], lg[u * TILE:] + bias_rows(c, g, u)], axis=0)
                lg_ref[c, g, :nk] = lg
                m8[g] = jnp.maximum(m8[g], lg.reshape(nk // SUBLANES, SUBLANES, gl).max(axis=0))

        pieces = (score_quads(eff(v + 1), 1 - slot, j + 1, qi_next_ref, dtw_next_ref)
                  if v < n_total else [])
        if v == n_total and merged:
            @pl.when(npair < n_total)
            def _():
                for piece in score_quads(n_total, 1 - slot, j + 1, qi_next_ref, dtw_next_ref):
                    piece()

        outs = []
        for g in range(N_KV_HEADS):
            m_row = jnp.max(m8[g], axis=0, keepdims=True)
            acc = jnp.zeros((PV_ROWS, gl), F32)
            for c, nk in enumerate(nks):
                p = jnp.exp2((lg_ref[c, g, :nk] - m_row).astype(BF16))
                vt = vt_ref[g * PV_ROWS:(g + 1) * PV_ROWS, c * QUAD:c * QUAD + nk]
                acc = acc + jnp.dot(vt, p, preferred_element_type=F32)
                if pieces:
                    pieces.pop(0)()
            if g == N_KV_HEADS - 1:
                for piece in pieces:
                    piece()
            o_g = acc[:HEAD_DIM] / acc[HEAD_DIM:HEAD_DIM + 1]
            outs += [o_g[:, r * QT:(r + 1) * QT] for r in range(rep)]
        o_ref[0] = jnp.concatenate(outs, axis=0).T.astype(BF16)

    for v in variants:
        pl.when(nv == v)(functools.partial(attend, v))


def _attention(q3, kv3, qi3, ki3, dtw3, bias, n_sel):
    bsz, seq_len, _ = q3.shape
    assert seq_len % QUAD == 0 and QUAD % QT == 0
    rep = N_ATTN_HEADS // N_KV_HEADS
    kern = functools.partial(_attn_kernel, seq_len=seq_len, n_sel=n_sel)
    steps = seq_len // QT
    this_step = lambda b, j: (b, j, 0)
    next_step = lambda b, j: (b, jnp.minimum(j + 1, steps - 1), 0)
    return pl.pallas_call(
        kern,
        grid=(bsz, steps),
        in_specs=[
            pl.BlockSpec((1, QT, ATTN_WIDTH), this_step),
            pl.BlockSpec((1, seq_len, KV_W), lambda b, j: (b, 0, 0)),
            pl.BlockSpec((1, QT, QI_W), this_step),
            pl.BlockSpec((1, seq_len, IDX_DIM), lambda b, j: (b, 0, 0)),
            pl.BlockSpec((1, QT, LANES), this_step),
            pl.BlockSpec((1, QT, QI_W), next_step),
            pl.BlockSpec((1, QT, LANES), next_step),
            pl.BlockSpec((N_ATTN_HEADS, 3, TILE, TILE), lambda b, j: (0, 0, 0, 0)),
        ],
        out_specs=pl.BlockSpec((1, QT, ATTN_WIDTH), this_step),
        out_shape=jax.ShapeDtypeStruct((bsz, seq_len, ATTN_WIDTH), BF16),
        scratch_shapes=[
            pltpu.VMEM((2, seq_len // PAIR, PAIR, QT), F32),
            pltpu.VMEM((2, seq_len // PAIR, PAIR, QT), I16),
            pltpu.VMEM((2, seq_len // PAIR, PAIR, QT), I16),
            pltpu.VMEM((seq_len // PAIR, PAIR, 2 * QT), F32),
            pltpu.VMEM((N_KV_HEADS * PV_ROWS, seq_len), BF16),
            pltpu.VMEM((seq_len // QUAD, N_KV_HEADS, QUAD, rep * QT), F32),
            pltpu.VMEM((1, QT), F32),
            pltpu.VMEM((1, QT), I32),
        ],
        compiler_params=pltpu.CompilerParams(
            dimension_semantics=("arbitrary", "arbitrary"), vmem_limit_bytes=VMEM_LIMIT),
        name="sparse_attn",
    )(q3, kv3, qi3, ki3, dtw3, qi3, dtw3, bias)


SSD_STEP_CHUNKS = 2
SSD_STEP = SSD_STEP_CHUNKS * TILE


def _split_bf16(v, n):
    parts = []
    for _ in range(n):
        p = v.astype(BF16)
        parts.append(p)
        v = v - p.astype(F32)
    return parts


def _dot_01(v, mat01, n):
    out = None
    for p in _split_bf16(v, n):
        t = jnp.dot(p, mat01, preferred_element_type=F32)
        out = t if out is None else out + t
    return out


def _ssd_kernel(z_ref, xbc_ref, dtw_ref, cw_ref, cb_ref, dtb_ref, alog_ref, dsk_ref, nw_ref,
                o_ref, xpad_ref, st_ref):
    step = pl.program_id(1)
    gn = SSD_N_GROUPS * SSD_D_STATE
    hpg = SSD_N_HEADS // SSD_N_GROUPS
    gw = hpg * SSD_HEAD_DIM

    @pl.when(step == 0)
    def _():
        xpad_ref[0:SUBLANES, :] = jnp.zeros((SUBLANES, CONV_DIM), F32)
        st_ref[...] = jnp.zeros_like(st_ref)

    xpad_ref[SUBLANES:, :] = xbc_ref[0]
    xfull = xpad_ref[...]
    conv = cb_ref[...] + cw_ref[CONV_WIDTH - 1:CONV_WIDTH, :] * xfull[SUBLANES:]
    for k in range(CONV_WIDTH - 1):
        shifted = pltpu.roll(xfull, CONV_WIDTH - 1 - k, 0)
        conv = conv + cw_ref[k:k + 1, :] * shifted[SUBLANES:]
    xpad_ref[0:SUBLANES, :] = xfull[SSD_STEP:]
    act = conv * jax.nn.sigmoid(conv)
    xs = act[:, :SSD_D_INNER]
    bm = act[:, SSD_D_INNER:SSD_D_INNER + gn].astype(BF16)
    cm = act[:, SSD_D_INNER + gn:].astype(BF16)

    dt_in = dtw_ref[0] + dtb_ref[...]
    dt = jnp.maximum(dt_in, 0.0) + jnp.log1p(jnp.exp(-jnp.abs(dt_in)))
    adt = dt * (-jnp.exp(alog_ref[...]))
    row = lax.broadcasted_iota(I32, (SSD_STEP, SSD_STEP), 0)
    col = lax.broadcasted_iota(I32, (SSD_STEP, SSD_STEP), 1)
    same_chunk = (row // TILE) == (col // TILE)
    chunk_tril = jnp.where(same_chunk & (row >= col), 1.0, 0.0).astype(BF16)
    acs = None
    for p in _split_bf16(adt, 3):
        t = jnp.dot(chunk_tril, p, preferred_element_type=F32)
        acs = t if acs is None else acs + t
    acs_t = acs.T
    causal = (lax.broadcasted_iota(I32, (TILE, TILE), 0) >= lax.broadcasted_iota(I32, (TILE, TILE), 1))
    a_last = [acs[(c + 1) * TILE - 1:(c + 1) * TILE, :] for c in range(SSD_STEP_CHUNKS)]
    out_decay = jnp.concatenate(
        [jnp.exp(a_last[c] - acs[c * TILE:(c + 1) * TILE]) for c in range(SSD_STEP_CHUNKS)], axis=0)
    chunk_decay = jnp.concatenate(
        [jnp.exp(a) for a in a_last] + [jnp.zeros((SUBLANES - SSD_STEP_CHUNKS, LANES), F32)], axis=0)
    hsel = (lax.broadcasted_iota(I32, (LANES, SSD_D_INNER), 0)
            == lax.broadcasted_iota(I32, (LANES, SSD_D_INNER), 1) // SSD_HEAD_DIM).astype(BF16)
    expanded = _dot_01(jnp.concatenate([dt, jnp.exp(acs), out_decay, chunk_decay], axis=0), hsel, 2)
    dt_x = expanded[:SSD_STEP]
    in_decay_x = expanded[SSD_STEP:2 * SSD_STEP]
    out_decay_x = expanded[2 * SSD_STEP:3 * SSD_STEP]
    chunk_decay_x = expanded[3 * SSD_STEP:]

    x_dt = xs * dt_x
    x_dt_b = x_dt.astype(BF16)
    x_out_b = (x_dt * out_decay_x).astype(BF16)
    tn_dims = (((0,), (0,)), ((), ()))
    nt_dims = (((1,), (1,)), ((), ()))
    states = [st_ref[g] for g in range(SSD_N_GROUPS)]
    y_chunks = []
    for c in range(SSD_STEP_CHUNKS):
        rows = slice(c * TILE, (c + 1) * TILE)
        y_parts = []
        for g in range(SSD_N_GROUPS):
            bg = bm[rows, g * SSD_D_STATE:(g + 1) * SSD_D_STATE]
            cg = cm[rows, g * SSD_D_STATE:(g + 1) * SSD_D_STATE]
            lanes = slice(g * gw, (g + 1) * gw)
            cb = lax.dot_general(cg, bg, nt_dims, preferred_element_type=F32)
            y_diag = []
            for r in range(hpg):
                h = g * hpg + r
                seg = jnp.where(causal, acs[rows, h:h + 1] - acs_t[h:h + 1, rows], NEG_INF)
                w = (cb * jnp.exp(seg)).astype(BF16)
                y_diag.append(jnp.dot(w, x_dt_b[rows, h * SSD_HEAD_DIM:(h + 1) * SSD_HEAD_DIM],
                                      preferred_element_type=F32))
            y_off = (jnp.dot(cg, states[g].astype(BF16), preferred_element_type=F32)
                     * in_decay_x[rows, lanes])
            y_parts.append(jnp.concatenate(y_diag, axis=-1) + y_off)
            st_new = lax.dot_general(bg, x_out_b[rows, lanes], tn_dims, preferred_element_type=F32)
            states[g] = states[g] * chunk_decay_x[c:c + 1, lanes] + st_new
        y_chunks.append(jnp.concatenate(y_parts, axis=-1))
    for g in range(SSD_N_GROUPS):
        st_ref[g] = states[g]

    y = jnp.concatenate(y_chunks, axis=0) + xs * dsk_ref[...]
    zv = z_ref[0]
    y = y * (zv * jax.nn.sigmoid(zv))
    ng = SSD_D_INNER // SSD_N_GROUPS
    outs = []
    for g in range(SSD_N_GROUPS):
        yg = y[:, g * ng:(g + 1) * ng]
        outs.append(yg * lax.rsqrt(jnp.mean(yg * yg, axis=-1, keepdims=True) + EPS))
    o_ref[0] = (jnp.concatenate(outs, axis=-1) * nw_ref[...]).astype(BF16)


def _ssd(z3, xbc3, dtw3, conv_w, conv_b, dt_bias, a_log, d_skip, norm_w):
    bsz, seq_len, _ = z3.shape
    assert seq_len % SSD_STEP == 0
    nc = seq_len // SSD_STEP
    blk = lambda w: pl.BlockSpec((1, SSD_STEP, w), lambda b, c: (b, c, 0))
    par = lambda r, w: pl.BlockSpec((r, w), lambda b, c: (0, 0))
    return pl.pallas_call(
        _ssd_kernel,
        grid=(bsz, nc),
        in_specs=[blk(SSD_D_INNER), blk(CONV_DIM), blk(LANES),
                  par(CONV_WIDTH, CONV_DIM), par(1, CONV_DIM), par(1, LANES), par(1, LANES),
                  par(1, SSD_D_INNER), par(1, SSD_D_INNER)],
        out_specs=blk(SSD_D_INNER),
        out_shape=jax.ShapeDtypeStruct((bsz, seq_len, SSD_D_INNER), BF16),
        scratch_shapes=[
            pltpu.VMEM((SUBLANES + SSD_STEP, CONV_DIM), F32),
            pltpu.VMEM((SSD_N_GROUPS, SSD_D_STATE, SSD_D_INNER // SSD_N_GROUPS), F32),
        ],
        compiler_params=pltpu.CompilerParams(
            dimension_semantics=("arbitrary", "arbitrary"), vmem_limit_bytes=VMEM_LIMIT),
        name="ssd_mixer",
    )(z3, xbc3, dtw3, conv_w, conv_b, dt_bias, a_log, d_skip, norm_w)


FF_CHUNK = 1024


MLP_ROW_SPLIT = 2


def _mlp_kernel(x_ref, attn_ref, ssd_ref, wo_ref, g1_ref, g2_ref, wu_ref, wd_ref, g3_ref, o_ref):
    rows_per = x_ref.shape[0] // MLP_ROW_SPLIT
    groups = [slice(k * rows_per, (k + 1) * rows_per) for k in range(MLP_ROW_SPLIT)]
    h1, u, acc = [], [], []
    for rows in groups:
        mix = (jnp.dot(attn_ref[rows, :], wo_ref[:ATTN_WIDTH, :], preferred_element_type=F32)
               + jnp.dot(ssd_ref[rows, :], wo_ref[ATTN_WIDTH:, :], preferred_element_type=F32))
        h1.append(x_ref[rows, :] + _rms(mix, g1_ref[...]))
        u.append(_rms(h1[-1], g2_ref[...]).astype(BF16))
        acc.append(jnp.zeros(h1[-1].shape, F32))
    for c in range(0, D_FF, FF_CHUNK):
        for k in range(MLP_ROW_SPLIT):
            f = jnp.dot(u[k], wu_ref[:, c:c + FF_CHUNK], preferred_element_type=F32)
            f = jnp.square(jnp.maximum(f, 0.0)).astype(BF16)
            acc[k] = acc[k] + jnp.dot(f, wd_ref[c:c + FF_CHUNK, :], preferred_element_type=F32)
    for k, rows in enumerate(groups):
        o_ref[rows, :] = h1[k] + _rms(acc[k], g3_ref[...])


def _mlp(x2, attn2, ssd2, wo, g1, g2, wu, wd, g3, tm):
    n = x2.shape[0]
    row = lambda i: (i, 0)
    const = lambda i: (0, 0)
    single = dict(pipeline_mode=pl.Buffered(1))
    return pl.pallas_call(
        _mlp_kernel,
        grid=(n // tm,),
        in_specs=[
            pl.BlockSpec((tm, D_MODEL), row),
            pl.BlockSpec((tm, ATTN_WIDTH), row),
            pl.BlockSpec((tm, SSD_D_INNER), row),
            pl.BlockSpec((ATTN_WIDTH + SSD_D_INNER, D_MODEL), const, **single),
            pl.BlockSpec((1, D_MODEL), const),
            pl.BlockSpec((1, D_MODEL), const),
            pl.BlockSpec((D_MODEL, D_FF), const, **single),
            pl.BlockSpec((D_FF, D_MODEL), const, **single),
            pl.BlockSpec((1, D_MODEL), const),
        ],
        out_specs=pl.BlockSpec((tm, D_MODEL), row),
        out_shape=jax.ShapeDtypeStruct((n, D_MODEL), F32),
        compiler_params=pltpu.CompilerParams(
            dimension_semantics=("arbitrary",), vmem_limit_bytes=VMEM_LIMIT),
        name="out_proj_mlp",
    )(x2, attn2, ssd2, wo, g1, g2, wu, wd, g3)


def _pad_lanes(v, n):
    return jnp.pad(v, (0, n - v.shape[0])).reshape(1, n)


def kernel(x, norm_pre_mix, norm_post_mix, norm_pre_mlp, norm_post_mlp, w_in, k_idx_ln_w, k_idx_ln_b, conv_w, conv_b, dt_bias, a_log, d_skip, ssd_norm_w, w_out, w_mlp_up, w_mlp_down, rel_bias):
    bsz, seq_len, d = x.shape
    n = bsz * seq_len
    assert d == D_MODEL and seq_len % TILE == 0
    tm = next(t for t in (1024, 512, TILE) if n % t == 0)
    n_sel = min(TOPK_MAX, seq_len // 4)
    bias = _bias_tiles(rel_bias)
    h = x.reshape(n, d)
    for i in range(norm_pre_mix.shape[0]):
        row = lambda v: v[i].reshape(1, -1)
        q, kv, qi, ki, z, xbc, dtw = _in_proj(h, row(norm_pre_mix), w_in[i].T,
                                              row(k_idx_ln_w), row(k_idx_ln_b), tm)
        r3 = lambda a: a.reshape(bsz, seq_len, a.shape[-1])
        dtw3 = r3(dtw)
        attn = _attention(r3(q), r3(kv), r3(qi), r3(ki), dtw3, bias, n_sel)
        ssd = _ssd(r3(z), r3(xbc), dtw3, conv_w[i], row(conv_b),
                   _pad_lanes(dt_bias[i], LANES), _pad_lanes(a_log[i], LANES),
                   jnp.repeat(d_skip[i], SSD_HEAD_DIM).reshape(1, -1), row(ssd_norm_w))
        h = _mlp(h, attn.reshape(n, -1), ssd.reshape(n, -1), w_out[i].astype(BF16),
                 row(norm_post_mix), row(norm_pre_mlp), w_mlp_up[i].astype(BF16),
                 w_mlp_down[i].astype(BF16), row(norm_post_mlp), tm)
    return h.reshape(bsz, seq_len, d)
```

```python
import functools
import math

import jax
import jax.numpy as jnp
from jax import lax
from jax.experimental import pallas as pl
from jax.experimental.pallas import tpu as pltpu

F32 = jnp.float32
BF16 = jnp.bfloat16
I32 = jnp.int32

D_MODEL = 1024
N_ATTN_HEADS = 8
N_KV_HEADS = 2
HEAD_DIM = 64
ATTN_WIDTH = N_ATTN_HEADS * HEAD_DIM
N_IDX_HEADS = 4
IDX_DIM = 64
TOPK_MAX = 256
NUM_BUCKETS = 32
MAX_DISTANCE = 128
SSD_D_INNER = 512
SSD_HEAD_DIM = 64
SSD_N_HEADS = 8
SSD_N_GROUPS = 2
SSD_D_STATE = 128
CONV_WIDTH = 4
CONV_DIM = SSD_D_INNER + 2 * SSD_N_GROUPS * SSD_D_STATE
D_FF = 4 * D_MODEL
EPS = 1e-6

LANES = 128
SUBLANES = 8
TILE = 128
VMEM_LIMIT = 56 * 1024 * 1024

QKV_W = ATTN_WIDTH + 2 * N_KV_HEADS * HEAD_DIM + N_IDX_HEADS * IDX_DIM
KV_W = 2 * N_KV_HEADS * HEAD_DIM
QI_W = N_IDX_HEADS * IDX_DIM
COL_QKV = 0
COL_KV = COL_QKV + ATTN_WIDTH
COL_QI = COL_KV + KV_W
COL_KI = COL_QKV + QKV_W
COL_Z = COL_KI + LANES
COL_XBC = COL_Z + SSD_D_INNER
COL_DTW = COL_XBC + CONV_DIM
W_CAT = COL_DTW + LANES
DTW_WI = SSD_N_HEADS
SRC_QKV = 0
SRC_KI = SRC_QKV + QKV_W
SRC_WI = SRC_KI + IDX_DIM
SRC_Z = SRC_WI + N_IDX_HEADS
SRC_XBC = SRC_Z + SSD_D_INNER
SRC_DT = SRC_XBC + CONV_DIM

NEG_INF = float("-inf")
LOG2E = 1.4426950408889634
Q_SCALE = LOG2E * HEAD_DIM ** -0.5


def _rms(x, g):
    return x * lax.rsqrt(jnp.mean(x * x, axis=-1, keepdims=True) + EPS) * g


def _in_proj_kernel(x_ref, g_ref, wt_ref, lnw_ref, lnb_ref,
                    q_ref, kv_ref, qi_ref, ki_ref, z_ref, xbc_ref, dtw_ref, w_ref):
    @pl.when(pl.program_id(0) == 0)
    def _():
        def put(col, pieces):
            rows = [wt_ref[lo:lo + width, :] for lo, width in pieces]
            used = sum(width for _, width in pieces)
            if used < LANES:
                rows.append(jnp.zeros((LANES - used, D_MODEL), F32))
            blk = rows[0] if len(rows) == 1 else jnp.concatenate(rows, axis=0)
            w_ref[:, col:col + LANES] = blk.T.astype(BF16)

        for k in range(QKV_W // LANES):
            put(COL_QKV + k * LANES, [(SRC_QKV + k * LANES, LANES)])
        put(COL_KI, [(SRC_KI, IDX_DIM)])
        for k in range(SSD_D_INNER // LANES):
            put(COL_Z + k * LANES, [(SRC_Z + k * LANES, LANES)])
        for k in range(CONV_DIM // LANES):
            put(COL_XBC + k * LANES, [(SRC_XBC + k * LANES, LANES)])
        put(COL_DTW, [(SRC_DT, SSD_N_HEADS), (SRC_WI, N_IDX_HEADS)])

    u = _rms(x_ref[...], g_ref[...]).astype(BF16)

    def mm(lo, hi):
        return jnp.dot(u, w_ref[:, lo:hi], preferred_element_type=F32)

    q_ref[...] = (mm(COL_QKV, COL_KV) * Q_SCALE).astype(BF16)
    kv_ref[...] = mm(COL_KV, COL_QI).astype(BF16)
    qi_ref[...] = mm(COL_QI, COL_KI).astype(BF16)
    ki = mm(COL_KI, COL_Z)[:, :IDX_DIM]
    mu = jnp.mean(ki, axis=-1, keepdims=True)
    var = jnp.mean(jnp.square(ki - mu), axis=-1, keepdims=True)
    ki_ref[...] = ((ki - mu) * lax.rsqrt(var + EPS) * lnw_ref[...] + lnb_ref[...]).astype(BF16)
    z_ref[...] = mm(COL_Z, COL_XBC)
    xbc_ref[...] = mm(COL_XBC, COL_DTW)
    dtw_ref[...] = mm(COL_DTW, W_CAT)


def _in_proj(x2, g, w_in, lnw, lnb, tm):
    n = x2.shape[0]
    assert w_in.shape == (SRC_DT + SSD_N_HEADS, D_MODEL)
    row = lambda i: (i, 0)
    const = lambda i: (0, 0)
    return pl.pallas_call(
        _in_proj_kernel,
        grid=(n // tm,),
        in_specs=[
            pl.BlockSpec((tm, D_MODEL), row),
            pl.BlockSpec((1, D_MODEL), const),
            pl.BlockSpec(w_in.shape, const, pipeline_mode=pl.Buffered(1)),
            pl.BlockSpec((1, IDX_DIM), const),
            pl.BlockSpec((1, IDX_DIM), const),
        ],
        out_specs=[
            pl.BlockSpec((tm, ATTN_WIDTH), row),
            pl.BlockSpec((tm, KV_W), row),
            pl.BlockSpec((tm, QI_W), row),
            pl.BlockSpec((tm, IDX_DIM), row),
            pl.BlockSpec((tm, SSD_D_INNER), row),
            pl.BlockSpec((tm, CONV_DIM), row),
            pl.BlockSpec((tm, LANES), row),
        ],
        out_shape=[
            jax.ShapeDtypeStruct((n, ATTN_WIDTH), BF16),
            jax.ShapeDtypeStruct((n, KV_W), BF16),
            jax.ShapeDtypeStruct((n, QI_W), BF16),
            jax.ShapeDtypeStruct((n, IDX_DIM), BF16),
            jax.ShapeDtypeStruct((n, SSD_D_INNER), F32),
            jax.ShapeDtypeStruct((n, CONV_DIM), F32),
            jax.ShapeDtypeStruct((n, LANES), F32),
        ],
        scratch_shapes=[pltpu.VMEM((D_MODEL, W_CAT), BF16)],
        compiler_params=pltpu.CompilerParams(
            dimension_semantics=("arbitrary",), vmem_limit_bytes=VMEM_LIMIT),
        name="in_proj",
    )(x2, g, w_in, lnw, lnb)


def _bias_kernel(tbl_ref, out_ref):
    sk = lax.broadcasted_iota(I32, (TILE, TILE), 0)
    tq = lax.broadcasted_iota(I32, (TILE, TILE), 1)
    max_exact = NUM_BUCKETS // 2
    for off in range(2):
        dist = jnp.maximum(off * TILE + tq - sk, 0)
        df = jnp.maximum(dist, 1).astype(F32)
        large = max_exact + (jnp.log(df / max_exact) / math.log(MAX_DISTANCE / max_exact)
                             * (NUM_BUCKETS - max_exact)).astype(I32)
        large = jnp.minimum(large, NUM_BUCKETS - 1)
        bucket = jnp.where(dist < max_exact, dist, large)
        for h in range(N_ATTN_HEADS):
            acc = jnp.zeros((TILE, TILE), F32)
            for b in range(NUM_BUCKETS):
                acc = jnp.where(bucket == b, tbl_ref[b, h], acc)
            out_ref[h, off] = (acc - tbl_ref[NUM_BUCKETS - 1, h]) * LOG2E
    for h in range(N_ATTN_HEADS):
        out_ref[h, 2] = jnp.zeros((TILE, TILE), F32)


def _bias_tiles(rel_bias):
    return pl.pallas_call(
        _bias_kernel,
        in_specs=[pl.BlockSpec(memory_space=pltpu.SMEM)],
        out_specs=pl.BlockSpec(memory_space=pltpu.VMEM),
        out_shape=jax.ShapeDtypeStruct((N_ATTN_HEADS, 3, TILE, TILE), F32),
        name="bias_tiles",
    )(rel_bias)


def _key_to_f32(u):
    ks = u ^ I32(-2 ** 31)
    bits = jnp.where(ks >= 0, ks, ks ^ I32(0x7FFFFFFF))
    return lax.bitcast_convert_type(bits, F32)


PAIR = 2 * TILE
QUAD = 4 * TILE
QT = 2 * TILE
HALF_BITS = 16
HALF_MASK = 2 ** HALF_BITS - 1
HALF_BIAS = 2 ** (HALF_BITS - 1)
LO_BITS_FIRST = 10
I16 = jnp.int16
PV_ROWS = HEAD_DIM + 16


def _attn_kernel(q_ref, kv_ref, qi_ref, ki_ref, dtw_ref, qi_next_ref, dtw_next_ref, bias_ref, o_ref,
                 sc_ref, hi_ref, lo_ref, rk_ref, vt_ref, lg_ref, thr_ref, cnt_ref, *, seq_len, n_sel):
    j = pl.program_id(1)
    i_hi = j * (QT // TILE) + QT // TILE - 1
    npair = i_hi // 2 + 1
    n_total = seq_len // PAIR
    merged = n_total - 1 if n_total >= 4 else 0
    eff = lambda v: n_total if v == merged else v
    variants = [v for v in range(1, n_total + 1) if v != merged]
    nv = jnp.where(npair == merged, n_total, npair)
    slot = lax.rem(j, 2)

    def quad_keys(v):
        return [QUAD] * (v // 2) + [PAIR] * (v % 2)

    rep = N_ATTN_HEADS // N_KV_HEADS
    gl = rep * QT
    nt_dims = (((1,), (1,)), ((), ()))
    v_lo = N_KV_HEADS * HEAD_DIM

    @pl.when(j == 0)
    def _():
        vt = kv_ref[0, :, v_lo:].astype(F32).T
        ones_row = jnp.where(lax.broadcasted_iota(I32, (PV_ROWS - HEAD_DIM, seq_len), 0) == 0, 1.0, 0.0)
        for g in range(N_KV_HEADS):
            vt_ref[g * PV_ROWS:g * PV_ROWS + HEAD_DIM, :] = vt[g * HEAD_DIM:(g + 1) * HEAD_DIM].astype(BF16)
            vt_ref[g * PV_ROWS + HEAD_DIM:(g + 1) * PV_ROWS, :] = ones_row.astype(BF16)

    idx_scale = (N_IDX_HEADS ** -0.5) * (IDX_DIM ** -0.5)
    s_loc = lax.broadcasted_iota(I32, (PAIR, QT), 0)
    s_minus_t = s_loc - lax.broadcasted_iota(I32, (PAIR, QT), 1)

    def score_quads(v, to_slot, step, qi_src, dtw_src):
        w_t = dtw_src[0].T
        w_rows = [w_t[DTW_WI + h:DTW_WI + h + 1, :] * idx_scale for h in range(N_IDX_HEADS)]
        qi = qi_src[0]
        qi_all = jnp.concatenate(
            [qi[:, h * IDX_DIM:(h + 1) * IDX_DIM] for h in range(N_IDX_HEADS)], axis=0)
        def piece(c, nk):
            kt = ki_ref[0, c * QUAD:c * QUAD + nk, :]
            d = lax.dot_general(kt, qi_all, nt_dims, preferred_element_type=F32)
            s = jnp.zeros((nk, QT), F32)
            for h in range(N_IDX_HEADS):
                s = s + w_rows[h] * jnp.maximum(d[:, h * QT:(h + 1) * QT], 0.0)
            for u in range(nk // PAIR):
                pr = c * (QUAD // PAIR) + u
                su = jnp.where(s_minus_t <= step * QT - pr * PAIR, s[u * PAIR:(u + 1) * PAIR], NEG_INF)
                sc_ref[to_slot, pr] = su
                bits = lax.bitcast_convert_type(su, I32)
                key = bits ^ (lax.shift_right_arithmetic(bits, I32(31)) & I32(0x7FFFFFFF))
                hi_ref[to_slot, pr] = lax.shift_right_arithmetic(key, I32(HALF_BITS)).astype(I16)
                lo_ref[to_slot, pr] = (key ^ I32(HALF_BIAS)).astype(I16)

        return [functools.partial(piece, c, nk) for c, nk in enumerate(quad_keys(v))]

    @pl.when(j == 0)
    def _():
        for piece in score_quads(1, 0, 0, qi_ref, dtw_ref):
            piece()

    t_glob = j * QT + lax.broadcasted_iota(I32, (1, QT), 1)
    k_eff = jnp.minimum(n_sel, t_glob + 1)
    acc_rows = 4 * 2 * SUBLANES
    one16, zero16 = jnp.ones((), I16), jnp.zeros((), I16)

    def fold(cnt, rows=acc_rows):
        parts = [cnt[k * rows:(k + 1) * rows] for k in range(cnt.shape[0] // rows)]
        while len(parts) > 1:
            parts = [a + b for a, b in zip(parts[::2], parts[1::2])]
        return parts[0]

    def total(acc):
        packed = fold(acc, 2 * SUBLANES).astype(I32)
        return (packed[:SUBLANES] + packed[SUBLANES:]).sum(axis=0, keepdims=True)

    def to16(u):
        return (u - I32(HALF_BIAS)).astype(I16)

    def search(v):
        def count_ge(plane_ref, cand16):
            acc = jnp.zeros((acc_rows, QT), I16)
            for pr in range(v):
                acc = acc + fold(jnp.where(plane_ref[slot, pr] >= cand16, one16, zero16))
            return total(acc)

        def search_step(b, carry, plane_ref, base):
            prefix, cnt_ge, settled = carry
            cand = prefix | lax.shift_left(I32(1), HALF_BITS - 1 - b)
            cnt = count_ge(plane_ref, to16(cand)) + base
            ok = cnt >= k_eff
            settled = jnp.where(ok, jnp.where(cnt == k_eff, 1, settled), settled)
            return jnp.where(ok, cand, prefix), jnp.where(ok, cnt, cnt_ge), settled

        carry = (jnp.zeros((1, QT), I32), t_glob + 1, jnp.zeros((1, QT), I32))
        hi_u, cnt_ge, settled = lax.fori_loop(
            0, HALF_BITS, functools.partial(search_step, plane_ref=hi_ref, base=0), carry)
        hi16 = to16(hi_u)

        acc = jnp.zeros((acc_rows, QT), I16)
        for pr in range(v):
            h = hi_ref[slot, pr]
            lo_ref[slot, pr] = jnp.where(h == hi16, lo_ref[slot, pr], I16(-HALF_BIAS))
            acc = acc + fold(jnp.where(h > hi16, one16, zero16))
        above = total(acc)
        lo_step = functools.partial(search_step, plane_ref=lo_ref, base=above)
        floor_tie = count_ge(lo_ref, to16(jnp.ones((1, QT), I32))) + above < k_eff
        carry = (jnp.zeros((1, QT), I32), cnt_ge, jnp.where(floor_tie, 1, settled))
        carry = lax.fori_loop(0, LO_BITS_FIRST, lo_step, carry)
        lo_u, cnt_ge, _ = lax.cond(
            jnp.min(carry[2]) > 0, lambda c: c,
            lambda c: lax.fori_loop(LO_BITS_FIRST, HALF_BITS, lo_step, c), carry)
        thr_ref[...] = _key_to_f32(lax.shift_left(hi_u, I32(HALF_BITS)) | lo_u)
        cnt_ref[...] = cnt_ge

    for v in variants:
        pl.when(nv == v)(functools.partial(search, v))

    q = q_ref[0]
    qgs = [jnp.concatenate(
        [q[:, (g * rep + r) * HEAD_DIM:(g * rep + r + 1) * HEAD_DIM] for r in range(rep)],
        axis=0) for g in range(N_KV_HEADS)]
    below_diag = (lax.broadcasted_iota(I32, (PAIR, PAIR), 0)
                  > lax.broadcasted_iota(I32, (PAIR, PAIR), 1)).astype(BF16)
    tiles_per_quad = QUAD // TILE
    pairs_per_quad = QUAD // PAIR
    n_qt = QT // TILE
    int_min = I32(-2 ** 31)

    def bias_rows(c, g, u):
        cols = []
        for r in range(rep):
            for w in range(n_qt):
                off = jnp.clip(j * n_qt + w - (c * tiles_per_quad + u), 0, 2)
                cols.append(bias_ref[g * rep + r, off])
        return jnp.concatenate(cols, axis=1)

    def attend(v):
        thr = thr_ref[...]
        cnt_ge = cnt_ref[...]

        def tied(s):
            eq = jnp.where(s == thr, 1.0, 0.0)
            neg0 = eq * jnp.where(lax.bitcast_convert_type(s, I32) == int_min, 1.0, 0.0)
            return eq - neg0, neg0

        tots = []
        for pr in range(v):
            e2 = jnp.concatenate(tied(sc_ref[slot, pr]), axis=1).astype(BF16)
            r = jnp.dot(below_diag, e2, preferred_element_type=F32)
            rk_ref[pr] = r
            tots.append(r[PAIR - 1:PAIR] + e2[PAIR - 1:PAIR].astype(F32))
        tot = tots[0]
        for t in tots[1:]:
            tot = tot + t
        need = (k_eff - cnt_ge).astype(F32) + tot[:, :QT] + tot[:, QT:]
        off = jnp.concatenate([jnp.zeros((1, QT), F32), tot[:, :QT]], axis=1)

        m8 = [jnp.full((SUBLANES, gl), NEG_INF, F32) for _ in range(N_KV_HEADS)]
        nks = quad_keys(v)
        for c, nk in enumerate(nks):
            masks = []
            for u in range(nk // PAIR):
                pr = c * pairs_per_quad + u
                s = sc_ref[slot, pr]
                r = rk_ref[pr] + off
                off = off + tots[pr]
                neg0 = lax.bitcast_convert_type(s, I32) == int_min
                keep_tie = jnp.where(jnp.where(neg0, r[:, QT:], r[:, :QT]) < need, 0.0, NEG_INF)
                masks.append(jnp.where(s > thr, 0.0, jnp.where(s == thr, keep_tie, NEG_INF)))
            mask = jnp.concatenate(masks, axis=0)
            mask = jnp.concatenate([mask] * rep, axis=1)
            keys = slice(c * QUAD, c * QUAD + nk)
            for g in range(N_KV_HEADS):
                kt = kv_ref[0, keys, g * HEAD_DIM:(g + 1) * HEAD_DIM]
                lg = lax.dot_general(kt, qgs[g], nt_dims, preferred_element_type=F32) + mask
                if c == len(nks) - 1:
                    lg = lg + jnp.concatenate([bias_rows(c, g, u) for u in range(nk // TILE)], axis=0)
                elif c == len(nks) - 2 and (v % 2 == 1 or (v == n_total and merged)):
                    u = tiles_per_quad - 1
                    lg = jnp.concatenate([lg[:u * TILE], lg[u * TILE:] + bias_rows(c, g, u)], axis=0)
                lg_ref[c, g, :nk] = lg
                m8[g] = jnp.maximum(m8[g], lg.reshape(nk // SUBLANES, SUBLANES, gl).max(axis=0))

        pieces = (score_quads(eff(v + 1), 1 - slot, j + 1, qi_next_ref, dtw_next_ref)
                  if v < n_total else [])
        if v == n_total and merged:
            @pl.when(npair < n_total)
            def _():
                for piece in score_quads(n_total, 1 - slot, j + 1, qi_next_ref, dtw_next_ref):
                    piece()

        outs = []
        for g in range(N_KV_HEADS):
            m_row = jnp.max(m8[g], axis=0, keepdims=True)
            acc = jnp.zeros((PV_ROWS, gl), F32)
            for c, nk in enumerate(nks):
                p = jnp.exp2((lg_ref[c, g, :nk] - m_row).astype(BF16))
                vt = vt_ref[g * PV_ROWS:(g + 1) * PV_ROWS, c * QUAD:c * QUAD + nk]
                acc = acc + jnp.dot(vt, p, preferred_element_type=F32)
                if pieces:
                    pieces.pop(0)()
            if g == N_KV_HEADS - 1:
                for piece in pieces:
                    piece()
            o_g = acc[:HEAD_DIM] / acc[HEAD_DIM:HEAD_DIM + 1]
            outs += [o_g[:, r * QT:(r + 1) * QT] for r in range(rep)]
        o_ref[0] = jnp.concatenate(outs, axis=0).T.astype(BF16)

    for v in variants:
        pl.when(nv == v)(functools.partial(attend, v))


def _attention(q3, kv3, qi3, ki3, dtw3, bias, n_sel):
    bsz, seq_len, _ = q3.shape
    assert seq_len % QUAD == 0 and QUAD % QT == 0
    rep = N_ATTN_HEADS // N_KV_HEADS
    kern = functools.partial(_attn_kernel, seq_len=seq_len, n_sel=n_sel)
    steps = seq_len // QT
    this_step = lambda b, j: (b, j, 0)
    next_step = lambda b, j: (b, jnp.minimum(j + 1, steps - 1), 0)
    return pl.pallas_call(
        kern,
        grid=(bsz, steps),
        in_specs=[
            pl.BlockSpec((1, QT, ATTN_WIDTH), this_step),
            pl.BlockSpec((1, seq_len, KV_W), lambda b, j: (b, 0, 0)),
            pl.BlockSpec((1, QT, QI_W), this_step),
            pl.BlockSpec((1, seq_len, IDX_DIM), lambda b, j: (b, 0, 0)),
            pl.BlockSpec((1, QT, LANES), this_step),
            pl.BlockSpec((1, QT, QI_W), next_step),
            pl.BlockSpec((1, QT, LANES), next_step),
            pl.BlockSpec((N_ATTN_HEADS, 3, TILE, TILE), lambda b, j: (0, 0, 0, 0)),
        ],
        out_specs=pl.BlockSpec((1, QT, ATTN_WIDTH), this_step),
        out_shape=jax.ShapeDtypeStruct((bsz, seq_len, ATTN_WIDTH), BF16),
        scratch_shapes=[
            pltpu.VMEM((2, seq_len // PAIR, PAIR, QT), F32),
            pltpu.VMEM((2, seq_len // PAIR, PAIR, QT), I16),
            pltpu.VMEM((2, seq_len // PAIR, PAIR, QT), I16),
            pltpu.VMEM((seq_len // PAIR, PAIR, 2 * QT), F32),
            pltpu.VMEM((N_KV_HEADS * PV_ROWS, seq_len), BF16),
            pltpu.VMEM((seq_len // QUAD, N_KV_HEADS, QUAD, rep * QT), F32),
            pltpu.VMEM((1, QT), F32),
            pltpu.VMEM((1, QT), I32),
        ],
        compiler_params=pltpu.CompilerParams(
            dimension_semantics=("arbitrary", "arbitrary"), vmem_limit_bytes=VMEM_LIMIT),
        name="sparse_attn",
    )(q3, kv3, qi3, ki3, dtw3, qi3, dtw3, bias)


SSD_STEP_CHUNKS = 2
SSD_STEP = SSD_STEP_CHUNKS * TILE


def _split_bf16(v, n):
    parts = []
    for _ in range(n):
        p = v.astype(BF16)
        parts.append(p)
        v = v - p.astype(F32)
    return parts


def _dot_01(v, mat01, n):
    out = None
    for p in _split_bf16(v, n):
        t = jnp.dot(p, mat01, preferred_element_type=F32)
        out = t if out is None else out + t
    return out


def _ssd_kernel(z_ref, xbc_ref, dtw_ref, cw_ref, cb_ref, dtb_ref, alog_ref, dsk_ref, nw_ref,
                o_ref, xpad_ref, st_ref):
    step = pl.program_id(1)
    gn = SSD_N_GROUPS * SSD_D_STATE
    hpg = SSD_N_HEADS // SSD_N_GROUPS
    gw = hpg * SSD_HEAD_DIM

    @pl.when(step == 0)
    def _():
        xpad_ref[0:SUBLANES, :] = jnp.zeros((SUBLANES, CONV_DIM), F32)
        st_ref[...] = jnp.zeros_like(st_ref)

    xpad_ref[SUBLANES:, :] = xbc_ref[0]
    xfull = xpad_ref[...]
    conv = cb_ref[...] + cw_ref[CONV_WIDTH - 1:CONV_WIDTH, :] * xfull[SUBLANES:]
    for k in range(CONV_WIDTH - 1):
        shifted = pltpu.roll(xfull, CONV_WIDTH - 1 - k, 0)
        conv = conv + cw_ref[k:k + 1, :] * shifted[SUBLANES:]
    xpad_ref[0:SUBLANES, :] = xfull[SSD_STEP:]
    act = conv * jax.nn.sigmoid(conv)
    xs = act[:, :SSD_D_INNER]
    bm = act[:, SSD_D_INNER:SSD_D_INNER + gn].astype(BF16)
    cm = act[:, SSD_D_INNER + gn:].astype(BF16)

    dt_in = dtw_ref[0] + dtb_ref[...]
    dt = jnp.maximum(dt_in, 0.0) + jnp.log1p(jnp.exp(-jnp.abs(dt_in)))
    adt = dt * (-jnp.exp(alog_ref[...]))
    row = lax.broadcasted_iota(I32, (SSD_STEP, SSD_STEP), 0)
    col = lax.broadcasted_iota(I32, (SSD_STEP, SSD_STEP), 1)
    same_chunk = (row // TILE) == (col // TILE)
    chunk_tril = jnp.where(same_chunk & (row >= col), 1.0, 0.0).astype(BF16)
    acs = None
    for p in _split_bf16(adt, 3):
        t = jnp.dot(chunk_tril, p, preferred_element_type=F32)
        acs = t if acs is None else acs + t
    acs_t = acs.T
    causal = (lax.broadcasted_iota(I32, (TILE, TILE), 0) >= lax.broadcasted_iota(I32, (TILE, TILE), 1))
    a_last = [acs[(c + 1) * TILE - 1:(c + 1) * TILE, :] for c in range(SSD_STEP_CHUNKS)]
    out_decay = jnp.concatenate(
        [jnp.exp(a_last[c] - acs[c * TILE:(c + 1) * TILE]) for c in range(SSD_STEP_CHUNKS)], axis=0)
    chunk_decay = jnp.concatenate(
        [jnp.exp(a) for a in a_last] + [jnp.zeros((SUBLANES - SSD_STEP_CHUNKS, LANES), F32)], axis=0)
    hsel = (lax.broadcasted_iota(I32, (LANES, SSD_D_INNER), 0)
            == lax.broadcasted_iota(I32, (LANES, SSD_D_INNER), 1) // SSD_HEAD_DIM).astype(BF16)
    expanded = _dot_01(jnp.concatenate([dt, jnp.exp(acs), out_decay, chunk_decay], axis=0), hsel, 2)
    dt_x = expanded[:SSD_STEP]
    in_decay_x = expanded[SSD_STEP:2 * SSD_STEP]
    out_decay_x = expanded[2 * SSD_STEP:3 * SSD_STEP]
    chunk_decay_x = expanded[3 * SSD_STEP:]

    x_dt = xs * dt_x
    x_dt_b = x_dt.astype(BF16)
    x_out_b = (x_dt * out_decay_x).astype(BF16)
    tn_dims = (((0,), (0,)), ((), ()))
    nt_dims = (((1,), (1,)), ((), ()))
    states = [st_ref[g] for g in range(SSD_N_GROUPS)]
    y_chunks = []
    for c in range(SSD_STEP_CHUNKS):
        rows = slice(c * TILE, (c + 1) * TILE)
        y_parts = []
        for g in range(SSD_N_GROUPS):
            bg = bm[rows, g * SSD_D_STATE:(g + 1) * SSD_D_STATE]
            cg = cm[rows, g * SSD_D_STATE:(g + 1) * SSD_D_STATE]
            lanes = slice(g * gw, (g + 1) * gw)
            cb = lax.dot_general(cg, bg, nt_dims, preferred_element_type=F32)
            y_diag = []
            for r in range(hpg):
                h = g * hpg + r
                seg = jnp.where(causal, acs[rows, h:h + 1] - acs_t[h:h + 1, rows], NEG_INF)
                w = (cb * jnp.exp(seg)).astype(BF16)
                y_diag.append(jnp.dot(w, x_dt_b[rows, h * SSD_HEAD_DIM:(h + 1) * SSD_HEAD_DIM],
                                      preferred_element_type=F32))
            y_off = (jnp.dot(cg, states[g].astype(BF16), preferred_element_type=F32)
                     * in_decay_x[rows, lanes])
            y_parts.append(jnp.concatenate(y_diag, axis=-1) + y_off)
            st_new = lax.dot_general(bg, x_out_b[rows, lanes], tn_dims, preferred_element_type=F32)
            states[g] = states[g] * chunk_decay_x[c:c + 1, lanes] + st_new
        y_chunks.append(jnp.concatenate(y_parts, axis=-1))
    for g in range(SSD_N_GROUPS):
        st_ref[g] = states[g]

    y = jnp.concatenate(y_chunks, axis=0) + xs * dsk_ref[...]
    zv = z_ref[0]
    y = y * (zv * jax.nn.sigmoid(zv))
    ng = SSD_D_INNER // SSD_N_GROUPS
    outs = []
    for g in range(SSD_N_GROUPS):
        yg = y[:, g * ng:(g + 1) * ng]
        outs.append(yg * lax.rsqrt(jnp.mean(yg * yg, axis=-1, keepdims=True) + EPS))
    o_ref[0] = (jnp.concatenate(outs, axis=-1) * nw_ref[...]).astype(BF16)


def _ssd(z3, xbc3, dtw3, conv_w, conv_b, dt_bias, a_log, d_skip, norm_w):
    bsz, seq_len, _ = z3.shape
    assert seq_len % SSD_STEP == 0
    nc = seq_len // SSD_STEP
    blk = lambda w: pl.BlockSpec((1, SSD_STEP, w), lambda b, c: (b, c, 0))
    par = lambda r, w: pl.BlockSpec((r, w), lambda b, c: (0, 0))
    return pl.pallas_call(
        _ssd_kernel,
        grid=(bsz, nc),
        in_specs=[blk(SSD_D_INNER), blk(CONV_DIM), blk(LANES),
                  par(CONV_WIDTH, CONV_DIM), par(1, CONV_DIM), par(1, LANES), par(1, LANES),
                  par(1, SSD_D_INNER), par(1, SSD_D_INNER)],
        out_specs=blk(SSD_D_INNER),
        out_shape=jax.ShapeDtypeStruct((bsz, seq_len, SSD_D_INNER), BF16),
        scratch_shapes=[
            pltpu.VMEM((SUBLANES + SSD_STEP, CONV_DIM), F32),
            pltpu.VMEM((SSD_N_GROUPS, SSD_D_STATE, SSD_D_INNER // SSD_N_GROUPS), F32),
        ],
        compiler_params=pltpu.CompilerParams(
            dimension_semantics=("arbitrary", "arbitrary"), vmem_limit_bytes=VMEM_LIMIT),
        name="ssd_mixer",
    )(z3, xbc3, dtw3, conv_w, conv_b, dt_bias, a_log, d_skip, norm_w)


FF_CHUNK = 1024


MLP_ROW_SPLIT = 2


def _mlp_kernel(x_ref, attn_ref, ssd_ref, wo_ref, g1_ref, g2_ref, wu_ref, wd_ref, g3_ref, o_ref):
    rows_per = x_ref.shape[0] // MLP_ROW_SPLIT
    groups = [slice(k * rows_per, (k + 1) * rows_per) for k in range(MLP_ROW_SPLIT)]
    h1, u, acc = [], [], []
    for rows in groups:
        mix = (jnp.dot(attn_ref[rows, :], wo_ref[:ATTN_WIDTH, :], preferred_element_type=F32)
               + jnp.dot(ssd_ref[rows, :], wo_ref[ATTN_WIDTH:, :], preferred_element_type=F32))
        h1.append(x_ref[rows, :] + _rms(mix, g1_ref[...]))
        u.append(_rms(h1[-1], g2_ref[...]).astype(BF16))
        acc.append(jnp.zeros(h1[-1].shape, F32))
    for c in range(0, D_FF, FF_CHUNK):
        for k in range(MLP_ROW_SPLIT):
            f = jnp.dot(u[k], wu_ref[:, c:c + FF_CHUNK], preferred_element_type=F32)
            f = jnp.square(jnp.maximum(f, 0.0)).astype(BF16)
            acc[k] = acc[k] + jnp.dot(f, wd_ref[c:c + FF_CHUNK, :], preferred_element_type=F32)
    for k, rows in enumerate(groups):
        o_ref[rows, :] = h1[k] + _rms(acc[k], g3_ref[...])


def _mlp(x2, attn2, ssd2, wo, g1, g2, wu, wd, g3, tm):
    n = x2.shape[0]
    row = lambda i: (i, 0)
    const = lambda i: (0, 0)
    single = dict(pipeline_mode=pl.Buffered(1))
    return pl.pallas_call(
        _mlp_kernel,
        grid=(n // tm,),
        in_specs=[
            pl.BlockSpec((tm, D_MODEL), row),
            pl.BlockSpec((tm, ATTN_WIDTH), row),
            pl.BlockSpec((tm, SSD_D_INNER), row),
            pl.BlockSpec((ATTN_WIDTH + SSD_D_INNER, D_MODEL), const, **single),
            pl.BlockSpec((1, D_MODEL), const),
            pl.BlockSpec((1, D_MODEL), const),
            pl.BlockSpec((D_MODEL, D_FF), const, **single),
            pl.BlockSpec((D_FF, D_MODEL), const, **single),
            pl.BlockSpec((1, D_MODEL), const),
        ],
        out_specs=pl.BlockSpec((tm, D_MODEL), row),
        out_shape=jax.ShapeDtypeStruct((n, D_MODEL), F32),
        compiler_params=pltpu.CompilerParams(
            dimension_semantics=("arbitrary",), vmem_limit_bytes=VMEM_LIMIT,
            allow_input_fusion=[False, False, False, True, False, False, True, True, False]),
        name="out_proj_mlp",
    )(x2, attn2, ssd2, wo, g1, g2, wu, wd, g3)


def _pad_lanes(v, n):
    return jnp.pad(v, (0, n - v.shape[0])).reshape(1, n)


def kernel(x, norm_pre_mix, norm_post_mix, norm_pre_mlp, norm_post_mlp, w_in, k_idx_ln_w, k_idx_ln_b, conv_w, conv_b, dt_bias, a_log, d_skip, ssd_norm_w, w_out, w_mlp_up, w_mlp_down, rel_bias):
    bsz, seq_len, d = x.shape
    n = bsz * seq_len
    assert d == D_MODEL and seq_len % TILE == 0
    tm = next(t for t in (1024, 512, TILE) if n % t == 0)
    n_sel = min(TOPK_MAX, seq_len // 4)
    bias = _bias_tiles(rel_bias)
    h = x.reshape(n, d)
    for i in range(norm_pre_mix.shape[0]):
        row = lambda v: v[i].reshape(1, -1)
        q, kv, qi, ki, z, xbc, dtw = _in_proj(h, row(norm_pre_mix), w_in[i].T,
                                              row(k_idx_ln_w), row(k_idx_ln_b), tm)
        r3 = lambda a: a.reshape(bsz, seq_len, a.shape[-1])
        dtw3 = r3(dtw)
        attn = _attention(r3(q), r3(kv), r3(qi), r3(ki), dtw3, bias, n_sel)
        ssd = _ssd(r3(z), r3(xbc), dtw3, conv_w[i], row(conv_b),
                   _pad_lanes(dt_bias[i], LANES), _pad_lanes(a_log[i], LANES),
                   jnp.repeat(d_skip[i], SSD_HEAD_DIM).reshape(1, -1), row(ssd_norm_w))
        h = _mlp(h, attn.reshape(n, -1), ssd.reshape(n, -1), w_out[i].astype(BF16),
                 row(norm_post_mix), row(norm_pre_mlp), w_mlp_up[i].astype(BF16),
                 w_mlp_down[i].astype(BF16), row(norm_post_mlp), tm)
    return h.reshape(bsz, seq_len, d)
```
